```python
import math
import jax, jax.numpy as jnp
from jax import lax
import numpy as np

D_MODEL = 2048
BATCH = 1
SEQ = 8192
DEPTH = 1

SSD_D_INNER = D_MODEL
SSD_HEAD_DIM = 64
SSD_HEADS = SSD_D_INNER // SSD_HEAD_DIM
SSD_GROUPS = 4
SSD_STATE = 128
SSD_CONV = 4
SSD_CONV_DIM = SSD_D_INNER + 2 * SSD_GROUPS * SSD_STATE
RET_HEADS = 8
RET_QK_DIM = 128
RET_V_DIM = 256
RET_QK_WIDTH = RET_HEADS * RET_QK_DIM
RET_V_WIDTH = RET_HEADS * RET_V_DIM
MIX_WIDTH = SSD_D_INNER + RET_V_WIDTH
CHUNK = 128
ROPE_BASE = 10000.0
EPS = 1e-6
OFF_Z = SSD_D_INNER
OFF_XBC = OFF_Z + SSD_CONV_DIM
OFF_DT = OFF_XBC + SSD_HEADS
OFF_Q = OFF_DT + RET_QK_WIDTH
OFF_K = OFF_Q + RET_QK_WIDTH
OFF_V = OFF_K + RET_V_WIDTH
IN_WIDTH = OFF_V + RET_V_WIDTH

kernel_name = "hymba_ssd_retention_adaln_block"


def rms_norm(x, w):
    xf = x.astype(jnp.float32)
    y = xf * lax.rsqrt(jnp.mean(xf * xf, axis=-1, keepdims=True) + EPS)
    return (y * w.astype(jnp.float32)).astype(x.dtype)


def gated_group_rms_norm(y, z, w):
    bsz, L, dn = y.shape
    u = (y * jax.nn.silu(z)).astype(jnp.float32).reshape(bsz, L, SSD_GROUPS, dn // SSD_GROUPS)
    u = u * lax.rsqrt(jnp.mean(u * u, axis=-1, keepdims=True) + EPS)
    return (u.reshape(bsz, L, dn) * w.astype(jnp.float32)).astype(y.dtype)


def head_group_norm(o, w):
    bsz, L, H, dv = o.shape
    of = o.astype(jnp.float32)
    mu = jnp.mean(of, axis=-1, keepdims=True)
    var = jnp.mean(jnp.square(of - mu), axis=-1, keepdims=True)
    y = ((of - mu) * lax.rsqrt(var + EPS)).reshape(bsz, L, H * dv)
    return (y * w.astype(jnp.float32)).astype(o.dtype)


def causal_dwconv(u, w, b):
    K = w.shape[0]
    y = lax.conv_general_dilated(u, w[:, None, :].astype(u.dtype), window_strides=(1,),
                                 padding=[(K - 1, 0)], dimension_numbers=('NWC', 'WIO', 'NWC'),
                                 feature_group_count=u.shape[-1])
    return y + b.astype(u.dtype)


def rotary(u, pos):
    d = u.shape[-1]
    half = d // 2
    inv = ROPE_BASE ** (-jnp.arange(half, dtype=jnp.float32) / half)
    ang = pos.astype(jnp.float32)[:, None] * inv[None, :]
    cos = jnp.cos(ang)[None, :, None, :]
    sin = jnp.sin(ang)[None, :, None, :]
    uf = u.astype(jnp.float32)
    u1, u2 = uf[..., :half], uf[..., half:]
    return jnp.concatenate([u1 * cos - u2 * sin, u1 * sin + u2 * cos], axis=-1).astype(u.dtype)


def ssd_chunked(xh, dt, a, bm, cm):
    bsz, L, H, P = xh.shape
    G, N = bm.shape[2], bm.shape[3]
    R = H // G
    nc = L // CHUNK
    dtype = xh.dtype
    x = xh.reshape(bsz, nc, CHUNK, G, R, P)
    dt = dt.reshape(bsz, nc, CHUNK, G, R)
    bm = bm.reshape(bsz, nc, CHUNK, G, N)
    cm = cm.reshape(bsz, nc, CHUNK, G, N)
    a_cs = jnp.cumsum(dt * a.reshape(G, R), axis=2)
    causal = jnp.tril(jnp.ones((CHUNK, CHUNK), dtype=bool))
    seg = a_cs[:, :, :, None] - a_cs[:, :, None, :]
    decay = jnp.exp(jnp.where(causal[:, :, None, None], seg, -jnp.inf)).astype(dtype)
    xdt = x * dt.astype(dtype)[..., None]
    cb = jnp.einsum('bctgn,bcsgn->bcgts', cm, bm)
    y_diag = jnp.einsum('bcgts,bctsgr,bcsgrp->bctgrp', cb, decay, xdt)
    decay_to_end = jnp.exp(a_cs[:, :, -1:] - a_cs).astype(dtype)
    states = jnp.einsum('bclgn,bclgr,bclgrp->bcgrpn', bm, decay_to_end, xdt)
    chunk_decay = jnp.exp(a_cs[:, :, -1])

    def step(carry, inp):
        st, dec = inp
        new = carry * dec[..., None, None] + st
        return new, carry

    init = jnp.zeros((bsz, G, R, P, N), jnp.float32)
    _, prev = lax.scan(step, init, (jnp.moveaxis(states.astype(jnp.float32), 1, 0),
                                     jnp.moveaxis(chunk_decay, 1, 0)))
    prev = jnp.moveaxis(prev, 0, 1).astype(dtype)
    y_off = jnp.einsum('bctgn,bcgrpn,bctgr->bctgrp', cm, prev, jnp.exp(a_cs).astype(dtype))
    return (y_diag + y_off).reshape(bsz, L, H, P)


def retention_chunked(q, k, v, log_gamma):
    bsz, L, H, Dk = q.shape
    Dv = v.shape[-1]
    nc = L // CHUNK
    dtype = q.dtype
    q = q.reshape(bsz, nc, CHUNK, H, Dk)
    k = k.reshape(bsz, nc, CHUNK, H, Dk)
    v = v.reshape(bsz, nc, CHUNK, H, Dv)
    idx = jnp.arange(CHUNK, dtype=jnp.float32)
    diff = idx[:, None] - idx[None, :]
    dmask = jnp.where((diff >= 0)[None], jnp.exp(diff[None] * log_gamma[:, None, None]), 0.0)
    scores = jnp.einsum('bcihd,bcjhd->bchij', q, k) * dmask.astype(dtype)
    o_intra = jnp.einsum('bchij,bcjhe->bcihe', scores, v)
    zeta = jnp.exp((CHUNK - 1 - idx)[None, :] * log_gamma[:, None])
    kv = jnp.einsum('bcjhd,hj,bcjhe->bchde', k, zeta.astype(dtype), v)
    chunk_decay = jnp.exp(CHUNK * log_gamma)[None, :, None, None]

    def step(carry, st):
        return carry * chunk_decay + st, carry

    init = jnp.zeros((bsz, H, Dk, Dv), jnp.float32)
    _, prev = lax.scan(step, init, jnp.moveaxis(kv.astype(jnp.float32), 1, 0))
    prev = jnp.moveaxis(prev, 0, 1).astype(dtype)
    xi = jnp.exp((idx + 1.0)[:, None] * log_gamma[None, :])
    o_cross = jnp.einsum('bcihd,bchde,ih->bcihe', q, prev, xi.astype(dtype))
    return (o_intra + o_cross).reshape(bsz, L, H, Dv)


def setup_inputs(seed: int = 0) -> dict:
    key = jax.random.key(seed)
    ks = jax.random.split(key, 16)
    f32 = jnp.float32
    x = jax.random.normal(ks[0], (BATCH, SEQ, D_MODEL), f32)
    c = jax.random.normal(ks[1], (BATCH, D_MODEL), f32)
    w_ada = jax.random.normal(ks[2], (DEPTH, D_MODEL, 3 * D_MODEL), f32) * (0.5 * D_MODEL ** -0.5)
    b_ada = 0.01 * jax.random.normal(ks[3], (DEPTH, 3 * D_MODEL), f32)
    norm_w = 1.0 + 0.02 * jax.random.normal(ks[4], (DEPTH, D_MODEL), f32)
    w_in = jax.random.normal(ks[5], (DEPTH, D_MODEL, IN_WIDTH), f32) * D_MODEL ** -0.5
    conv_w = jax.random.normal(ks[6], (DEPTH, SSD_CONV, SSD_CONV_DIM), f32) * SSD_CONV ** -0.5
    conv_b = 0.01 * jax.random.normal(ks[7], (DEPTH, SSD_CONV_DIM), f32)
    dt0 = jnp.exp(jax.random.uniform(ks[8], (DEPTH, SSD_HEADS), f32,
                                     math.log(1e-3), math.log(1e-1)))
    dt_bias = dt0 + jnp.log(-jnp.expm1(-dt0))
    a_log = jnp.log(jax.random.uniform(ks[9], (DEPTH, SSD_HEADS), f32, 1.0, 16.0))
    d_skip = 1.0 + 0.1 * jax.random.normal(ks[10], (DEPTH, SSD_HEADS), f32)
    ssd_norm_w = 1.0 + 0.02 * jax.random.normal(ks[11], (DEPTH, SSD_D_INNER), f32)
    ret_norm_w = 1.0 + 0.02 * jax.random.normal(ks[12], (DEPTH, RET_V_WIDTH), f32)
    w_out = jax.random.normal(ks[13], (DEPTH, MIX_WIDTH, D_MODEL), f32) * MIX_WIDTH ** -0.5
    final_norm_w = 1.0 + 0.02 * jax.random.normal(ks[14], (D_MODEL,), f32)
    return {"x": x, "c": c, "w_ada": w_ada, "b_ada": b_ada, "norm_w": norm_w, "w_in": w_in,
            "conv_w": conv_w, "conv_b": conv_b, "dt_bias": dt_bias, "a_log": a_log,
            "d_skip": d_skip, "ssd_norm_w": ssd_norm_w, "ret_norm_w": ret_norm_w,
            "w_out": w_out, "final_norm_w": final_norm_w}


def reference(x, c, w_ada, b_ada, norm_w, w_in, conv_w, conv_b, dt_bias, a_log, d_skip,
              ssd_norm_w, ret_norm_w, w_out, final_norm_w):
    bsz, L, _ = x.shape
    pos = jnp.arange(L, dtype=jnp.int32)
    log_gamma = jnp.log1p(-jnp.exp2(-5.0 - jnp.arange(RET_HEADS, dtype=jnp.float32)))
    cond = jax.nn.silu(c)
    h = x
    for layer in range(DEPTH):
        mod = cond @ w_ada[layer] + b_ada[layer]
        shift, scale, gate = jnp.split(mod, 3, axis=-1)
        u = rms_norm(h, norm_w[layer]) * (1.0 + scale[:, None, :]) + shift[:, None, :]
        proj = u @ w_in[layer]
        z, xbc, dt_raw, q, k, v, g = jnp.split(proj, [OFF_Z, OFF_XBC, OFF_DT, OFF_Q, OFF_K, OFF_V], axis=-1)

        xbc = jax.nn.silu(causal_dwconv(xbc, conv_w[layer], conv_b[layer]))
        xs, bm, cm = jnp.split(xbc, [SSD_D_INNER, SSD_D_INNER + SSD_GROUPS * SSD_STATE], axis=-1)
        xs = xs.reshape(bsz, L, SSD_HEADS, SSD_HEAD_DIM)
        bm = bm.reshape(bsz, L, SSD_GROUPS, SSD_STATE)
        cm = cm.reshape(bsz, L, SSD_GROUPS, SSD_STATE)
        dt = jax.nn.softplus(dt_raw.astype(jnp.float32) + dt_bias[layer].astype(jnp.float32))
        a = -jnp.exp(a_log[layer].astype(jnp.float32))
        y = ssd_chunked(xs, dt, a, bm, cm) + xs * d_skip[layer][:, None]
        y_ssd = gated_group_rms_norm(y.reshape(bsz, L, SSD_D_INNER), z, ssd_norm_w[layer])

        qh = rotary(q.reshape(bsz, L, RET_HEADS, RET_QK_DIM), pos)
        kh = rotary(k.reshape(bsz, L, RET_HEADS, RET_QK_DIM), pos) * (RET_QK_DIM ** -0.5)
        vh = v.reshape(bsz, L, RET_HEADS, RET_V_DIM)
        o = retention_chunked(qh, kh, vh, log_gamma)
        y_ret = head_group_norm(o, ret_norm_w[layer]) * jax.nn.silu(g)

        mixed = jnp.concatenate([y_ssd, y_ret], axis=-1) @ w_out[layer]
        h = h + gate[:, None, :] * mixed
    return rms_norm(h, final_norm_w)
```

```python
import functools

import numpy as np
import jax
import jax.numpy as jnp
from jax import lax
from jax.experimental import pallas as pl
from jax.experimental.pallas import tpu as pltpu

D_MODEL = 2048
SSD_D_INNER = D_MODEL
SSD_HEAD_DIM = 64
SSD_HEADS = SSD_D_INNER // SSD_HEAD_DIM
HEAD_DIM_LOG2 = SSD_HEAD_DIM.bit_length() - 1
SSD_GROUPS = 4
SSD_STATE = 128
SSD_CONV = 4
SSD_BC = SSD_GROUPS * SSD_STATE
SSD_CONV_DIM = SSD_D_INNER + 2 * SSD_BC
RET_HEADS = 8
RET_QK_DIM = 128
RET_V_DIM = 256
RET_QK_WIDTH = RET_HEADS * RET_QK_DIM
RET_V_WIDTH = RET_HEADS * RET_V_DIM
MIX_WIDTH = SSD_D_INNER + RET_V_WIDTH
CHUNK = 128
ROPE_BASE = 10000.0
EPS = 1e-6

OFF_Z = SSD_D_INNER
OFF_XBC = OFF_Z + SSD_CONV_DIM
OFF_DT = OFF_XBC + SSD_HEADS
OFF_Q = OFF_DT + RET_QK_WIDTH
OFF_K = OFF_Q + RET_QK_WIDTH
OFF_V = OFF_K + RET_V_WIDTH
IN_WIDTH = OFF_V + RET_V_WIDTH

PROJ_WIDTH = 4 * 2048 + 2 * RET_QK_WIDTH + 2 * SSD_BC
COL_Z, COL_X, COL_V, COL_G = 0, 1, 2, 3
COL_Q, COL_K = 8, 9
COL_B, COL_C = 20, 21

LANES = 128
SUBLANES = 8
QUAD = 4
QUAD_W = QUAD * SSD_HEAD_DIM
HEADS_PER_GROUP = SSD_HEADS // SSD_GROUPS
GROUP_W = SSD_D_INNER // SSD_GROUPS

TM_IN, TN_IN = 1024, 1024
TM_OUT = 512
MIB = 1024 * 1024

F32 = jnp.float32
BF16 = jnp.bfloat16


def _silu(v):
    return v / (1.0 + jnp.exp(-v))


def _dot(a, b):
    return jnp.dot(a, b, preferred_element_type=F32)


def _dot_nt(a, b):
    return lax.dot_general(a, b, (((1,), (1,)), ((), ())), preferred_element_type=F32)


def _dot_tn(a, b):
    return lax.dot_general(a, b, (((0,), (0,)), ((), ())), preferred_element_type=F32)


def _split3(v):
    hi = v.astype(BF16)
    r1 = v - hi.astype(F32)
    mid = r1.astype(BF16)
    lo = (r1 - mid.astype(F32)).astype(BF16)
    return hi, mid, lo


def _mod_kernel(c_ref, w_ref, b_ref, o_ref):
    cond = jnp.broadcast_to(_silu(c_ref[...]), (SUBLANES, D_MODEL))
    o_ref[...] = _dot(cond, w_ref[...])[0:1, :] + b_ref[...]


def _modulation(c, w_ada, b_ada):
    n = w_ada.shape[1]
    tn = 512
    return pl.pallas_call(
        _mod_kernel,
        grid=(n // tn,),
        in_specs=[pl.BlockSpec((1, D_MODEL), lambda j: (0, 0)),
                  pl.BlockSpec((D_MODEL, tn), lambda j: (0, j)),
                  pl.BlockSpec((1, tn), lambda j: (0, j))],
        out_specs=pl.BlockSpec((1, tn), lambda j: (0, j)),
        out_shape=jax.ShapeDtypeStruct((1, n), F32),
        compiler_params=pltpu.CompilerParams(dimension_semantics=("arbitrary",),
                                             vmem_limit_bytes=24 * MIB),
        name="adaln_modulation",
    )(c, w_ada, b_ada.reshape(1, n))


def _inproj_kernel(x_ref, nw_ref, shift_ref, scale_ref, w_ref, wdt_ref, o_ref, dt_ref, u_ref):
    @pl.when(pl.program_id(1) == 0)
    def _():
        x = x_ref[...]
        y = x * lax.rsqrt(jnp.mean(x * x, axis=-1, keepdims=True) + EPS) * nw_ref[...]
        u = (y * (1.0 + scale_ref[...]) + shift_ref[...]).astype(BF16)
        u_ref[...] = u
        dt_ref[...] = _dot(u, wdt_ref[...])

    o_ref[...] = _dot(u_ref[...], w_ref[...]).astype(BF16)


def _in_projection(h, norm_w, mod, w_perm, w_dt):
    L = h.shape[0]
    row = lambda i, j: (0, 0)
    return pl.pallas_call(
        _inproj_kernel,
        grid=(L // TM_IN, PROJ_WIDTH // TN_IN),
        in_specs=[pl.BlockSpec((TM_IN, D_MODEL), lambda i, j: (i, 0)),
                  pl.BlockSpec((1, D_MODEL), row),
                  pl.BlockSpec((1, D_MODEL), lambda i, j: (0, 0)),
                  pl.BlockSpec((1, D_MODEL), lambda i, j: (0, 1)),
                  pl.BlockSpec((D_MODEL, TN_IN), lambda i, j: (0, j)),
                  pl.BlockSpec((D_MODEL, LANES), row)],
        out_specs=[pl.BlockSpec((TM_IN, TN_IN), lambda i, j: (i, j)),
                   pl.BlockSpec((TM_IN, LANES), lambda i, j: (i, 0))],
        out_shape=[jax.ShapeDtypeStruct((L, PROJ_WIDTH), BF16),
                   jax.ShapeDtypeStruct((L, LANES), F32)],
        scratch_shapes=[pltpu.VMEM((TM_IN, D_MODEL), BF16)],
        compiler_params=pltpu.CompilerParams(dimension_semantics=("arbitrary", "arbitrary"),
                                             vmem_limit_bytes=48 * MIB),
        name="adaln_in_projection",
    )(h, norm_w.reshape(1, D_MODEL), mod, mod, w_perm, w_dt)


def _ssd_kernel(z_ref, x_ref, b_ref, c_ref, dtraw_ref, cw_ref, cb_ref, dtb_ref, alog_ref,
                dskip_ref, nw_ref, o_ref, xbuf, state, expand):
    int_iota = lambda shape, dim: lax.broadcasted_iota(jnp.int32, shape, dim)

    @pl.when(pl.program_id(0) == 0)
    def _():
        xbuf[0:SUBLANES, :] = jnp.zeros((SUBLANES, SSD_CONV_DIM), F32)
        state[...] = jnp.zeros_like(state)
        head_of_lane = jnp.right_shift(int_iota((LANES, SSD_D_INNER), 1), HEAD_DIM_LOG2)
        expand[...] = jnp.where(head_of_lane == int_iota((LANES, SSD_D_INNER), 0), 1.0, 0.0).astype(BF16)

    xbuf[SUBLANES:, 0:SSD_D_INNER] = x_ref[...].astype(F32)
    xbuf[SUBLANES:, SSD_D_INNER:SSD_D_INNER + SSD_BC] = b_ref[...].astype(F32)
    xbuf[SUBLANES:, SSD_D_INNER + SSD_BC:] = c_ref[...].astype(F32)
    acc = cb_ref[...] + cw_ref[SSD_CONV - 1:SSD_CONV, :] * xbuf[SUBLANES:, :]
    for k in range(SSD_CONV - 1):
        off = SUBLANES - (SSD_CONV - 1) + k
        acc = acc + cw_ref[k:k + 1, :] * xbuf[off:off + CHUNK, :]
    xbuf[0:SUBLANES, :] = xbuf[CHUNK:CHUNK + SUBLANES, :]
    xbc = _silu(acc)
    xs = xbc[:, 0:SSD_D_INNER]
    xs_b = xs.astype(BF16)
    bmat = xbc[:, SSD_D_INNER:SSD_D_INNER + SSD_BC]
    cmat = xbc[:, SSD_D_INNER + SSD_BC:]

    pre = dtraw_ref[...] + dtb_ref[...]
    dt = jnp.maximum(pre, 0.0) + jnp.log1p(jnp.exp(-jnp.abs(pre)))
    d_a = dt * (-jnp.exp(alog_ref[...]))
    rows = int_iota((CHUNK, CHUNK), 0)
    cols = int_iota((CHUNK, CHUNK), 1)
    causal = rows >= cols
    tril = jnp.where(causal, 1.0, 0.0).astype(BF16)
    acs = sum(_dot(tril, part) for part in _split3(d_a))
    acs_t = acs.T
    dt_t = dt.T
    w_t = jnp.exp(acs_t[:, CHUNK - 1:CHUNK] - acs_t) * dt_t
    end = jnp.broadcast_to(jnp.exp(acs[CHUNK - 1:CHUNK, :]), (SUBLANES, LANES))
    end_row = sum(_dot(part, expand[...]) for part in _split3(end))[0:1, :]

    lane_blk = jnp.right_shift(int_iota((CHUNK, QUAD_W), 1), HEAD_DIM_LOG2)

    def block_diag(v):
        return jnp.concatenate([jnp.where(lane_blk == j, v, jnp.zeros_like(v)) for j in range(QUAD)], axis=0)

    for g in range(SSD_GROUPS):
        gs = slice(g * SSD_STATE, (g + 1) * SSD_STATE)
        b_g = bmat[:, gs]
        c_g = cmat[:, gs]
        cb_g = _dot_nt(c_g.astype(BF16), b_g.astype(BF16))
        b_gt = b_g.T
        y_parts = []
        for qi in range(HEADS_PER_GROUP // QUAD):
            q = g * (HEADS_PER_GROUP // QUAD) + qi
            qs = slice(q * QUAD_W, (q + 1) * QUAD_W)
            lhs_diag, lhs_state, lhs_off = [], [], []
            for j in range(QUAD):
                hd = q * QUAD + j
                col = jnp.broadcast_to(acs[:, hd:hd + 1], (CHUNK, CHUNK))
                decay = jnp.exp(jnp.where(causal, col - acs_t[hd:hd + 1, :], -jnp.inf))
                lhs_diag.append((cb_g * decay * dt_t[hd:hd + 1, :]).astype(BF16))
                lhs_state.append((b_gt * w_t[hd:hd + 1, :]).astype(BF16))
                lhs_off.append((c_g * jnp.exp(col)).astype(BF16))
            x_blk = block_diag(xs_b[:, qs])
            s_prev = state[:, qs]
            s_blk = block_diag(s_prev.astype(BF16))
            y_q = _dot(jnp.concatenate(lhs_diag, axis=1), x_blk)
            y_q = y_q + _dot(jnp.concatenate(lhs_off, axis=1), s_blk)
            state[:, qs] = s_prev * end_row[:, qs] + _dot(jnp.concatenate(lhs_state, axis=1), x_blk)
            y_parts.append(y_q + xs[:, qs] * dskip_ref[:, qs])
        ws = slice(g * GROUP_W, (g + 1) * GROUP_W)
        u = jnp.concatenate(y_parts, axis=1) * _silu(z_ref[:, ws].astype(F32))
        u = u * lax.rsqrt(jnp.mean(u * u, axis=-1, keepdims=True) + EPS)
        o_ref[:, ws] = (u * nw_ref[:, ws]).astype(BF16)


def _ssd(proj, dt_raw, conv_w, conv_b, dt_bias, a_log, d_skip, ssd_norm_w):
    L = proj.shape[0]
    pad = lambda v: jnp.pad(v.reshape(1, SSD_HEADS), ((0, 0), (0, LANES - SSD_HEADS)))
    full = lambda shape: pl.BlockSpec(shape, lambda c: (0, 0))
    return pl.pallas_call(
        _ssd_kernel,
        grid=(L // CHUNK,),
        in_specs=[pl.BlockSpec((CHUNK, SSD_D_INNER), lambda c: (c, COL_Z)),
                  pl.BlockSpec((CHUNK, SSD_D_INNER), lambda c: (c, COL_X)),
                  pl.BlockSpec((CHUNK, SSD_BC), lambda c: (c, COL_B)),
                  pl.BlockSpec((CHUNK, SSD_BC), lambda c: (c, COL_C)),
                  pl.BlockSpec((CHUNK, LANES), lambda c: (c, 0)),
                  full((SSD_CONV, SSD_CONV_DIM)), full((1, SSD_CONV_DIM)),
                  full((1, LANES)), full((1, LANES)),
                  full((1, SSD_D_INNER)), full((1, SSD_D_INNER))],
        out_specs=pl.BlockSpec((CHUNK, SSD_D_INNER), lambda c: (c, 0)),
        out_shape=jax.ShapeDtypeStruct((L, SSD_D_INNER), BF16),
        scratch_shapes=[pltpu.VMEM((CHUNK + SUBLANES, SSD_CONV_DIM), F32),
                        pltpu.VMEM((SSD_STATE, SSD_D_INNER), F32),
                        pltpu.VMEM((LANES, SSD_D_INNER), BF16)],
        compiler_params=pltpu.CompilerParams(dimension_semantics=("arbitrary",),
                                             vmem_limit_bytes=32 * MIB),
        name="ssd_head_group",
    )(proj, proj, proj, proj, dt_raw, conv_w, conv_b.reshape(1, SSD_CONV_DIM), pad(dt_bias), pad(a_log),
      jnp.repeat(d_skip, SSD_HEAD_DIM).reshape(1, SSD_D_INNER), ssd_norm_w.reshape(1, SSD_D_INNER))


def _log_gammas():
    return [float(np.log1p(-np.exp2(np.float32(-5.0 - hd)), dtype=np.float32)) for hd in range(RET_HEADS)]


def _ret_kernel(q_ref, k_ref, v_ref, g_ref, inv_ref, nw_ref, o_ref, state, cos_in, sin_in, dmask, zeta, xi):
    c = pl.program_id(0)
    log_gammas = _log_gammas()
    half = RET_QK_DIM // 2

    @pl.when(c == 0)
    def _():
        state[...] = jnp.zeros_like(state)
        ri = lax.broadcasted_iota(jnp.int32, (CHUNK, CHUNK), 0).astype(F32)
        ci = lax.broadcasted_iota(jnp.int32, (CHUNK, CHUNK), 1).astype(F32)
        ang = ri * inv_ref[...]
        cos_in[...] = jnp.cos(ang)
        sin_in[...] = jnp.sin(ang)
        for hd, lg in enumerate(log_gammas):
            dmask[hd] = jnp.where(ri >= ci, jnp.exp((ri - ci) * lg), 0.0)
            zeta[hd] = jnp.exp((CHUNK - 1.0 - ri) * lg)
            xi[hd] = jnp.exp((ri + 1.0) * lg)

    base = jnp.broadcast_to((c * CHUNK).astype(F32) * inv_ref[...], (SUBLANES, RET_QK_DIM))
    cos_b = jnp.cos(base)[0:1, :]
    sin_b = jnp.sin(base)[0:1, :]
    cos_t = cos_in[...] * cos_b - sin_in[...] * sin_b
    sin_t = sin_in[...] * cos_b + cos_in[...] * sin_b
    sin_t = jnp.where(lax.broadcasted_iota(jnp.int32, (CHUNK, RET_QK_DIM), 1) < half, -sin_t, sin_t)
    k_scale = RET_QK_DIM ** -0.5
    cos_k = cos_t * k_scale
    sin_k = sin_t * k_scale

    for hd in range(RET_HEADS):
        ks = slice(hd * RET_QK_DIM, (hd + 1) * RET_QK_DIM)
        vs = slice(hd * RET_V_DIM, (hd + 1) * RET_V_DIM)
        q_h = q_ref[:, ks].astype(F32)
        k_h = k_ref[:, ks].astype(F32)
        q_r = q_h * cos_t + pltpu.roll(q_h, half, 1) * sin_t
        k_r = k_h * cos_k + pltpu.roll(k_h, half, 1) * sin_k
        v_h = v_ref[:, vs]
        scores = _dot_nt(q_r.astype(BF16), k_r.astype(BF16)) * dmask[hd]
        s_prev = state[hd]
        o = _dot(scores.astype(BF16), v_h) + _dot((q_r * xi[hd]).astype(BF16), s_prev.astype(BF16))
        state[hd] = s_prev * float(np.exp(np.float32(CHUNK * log_gammas[hd]))) + _dot_tn(
            (k_r * zeta[hd]).astype(BF16), v_h)
        mu = jnp.mean(o, axis=-1, keepdims=True)
        d = o - mu
        y = d * lax.rsqrt(jnp.mean(d * d, axis=-1, keepdims=True) + EPS) * nw_ref[:, vs]
        o_ref[:, vs] = (y * _silu(g_ref[:, vs].astype(F32))).astype(BF16)


def _retention(proj, ret_norm_w):
    L = proj.shape[0]
    half = RET_QK_DIM // 2
    inv = ROPE_BASE ** (-jnp.arange(half, dtype=F32) / half)
    inv = jnp.concatenate([inv, inv]).reshape(1, RET_QK_DIM)
    full = lambda shape: pl.BlockSpec(shape, lambda c: (0, 0))
    table = pltpu.VMEM((RET_HEADS, CHUNK, CHUNK), F32)
    return pl.pallas_call(
        _ret_kernel,
        grid=(L // CHUNK,),
        in_specs=[pl.BlockSpec((CHUNK, RET_QK_WIDTH), lambda c: (c, COL_Q)),
                  pl.BlockSpec((CHUNK, RET_QK_WIDTH), lambda c: (c, COL_K)),
                  pl.BlockSpec((CHUNK, RET_V_WIDTH), lambda c: (c, COL_V)),
                  pl.BlockSpec((CHUNK, RET_V_WIDTH), lambda c: (c, COL_G)),
                  full((1, RET_QK_DIM)), full((1, RET_V_WIDTH))],
        out_specs=pl.BlockSpec((CHUNK, RET_V_WIDTH), lambda c: (c, 0)),
        out_shape=jax.ShapeDtypeStruct((L, RET_V_WIDTH), BF16),
        scratch_shapes=[pltpu.VMEM((RET_HEADS, RET_QK_DIM, RET_V_DIM), F32),
                        pltpu.VMEM((CHUNK, RET_QK_DIM), F32), pltpu.VMEM((CHUNK, RET_QK_DIM), F32),
                        table, table, table],
        compiler_params=pltpu.CompilerParams(dimension_semantics=("arbitrary",),
                                             vmem_limit_bytes=32 * MIB),
        name="retention_head_group",
    )(proj, proj, proj, proj, inv, ret_norm_w.reshape(1, RET_V_WIDTH))


def _outproj_kernel(ys_ref, yr_ref, ws_ref, wr_ref, h_ref, gate_ref, fw_ref, o_ref, *, final):
    mixed = _dot(ys_ref[...], ws_ref[...]) + _dot(yr_ref[...], wr_ref[...])
    h = h_ref[...] + gate_ref[...] * mixed
    if final:
        h = h * lax.rsqrt(jnp.mean(h * h, axis=-1, keepdims=True) + EPS) * fw_ref[...]
    o_ref[...] = h


def _out_projection(y_ssd, y_ret, w_out_b, h, mod, final_norm_w, final):
    L = h.shape[0]
    once = pl.Buffered(1)
    return pl.pallas_call(
        functools.partial(_outproj_kernel, final=final),
        grid=(L // TM_OUT,),
        in_specs=[pl.BlockSpec((TM_OUT, SSD_D_INNER), lambda i: (i, 0)),
                  pl.BlockSpec((TM_OUT, RET_V_WIDTH), lambda i: (i, 0)),
                  pl.BlockSpec((SSD_D_INNER, D_MODEL), lambda i: (0, 0), pipeline_mode=once),
                  pl.BlockSpec((RET_V_WIDTH, D_MODEL), lambda i: (1, 0), pipeline_mode=once),
                  pl.BlockSpec((TM_OUT, D_MODEL), lambda i: (i, 0)),
                  pl.BlockSpec((1, D_MODEL), lambda i: (0, 2)),
                  pl.BlockSpec((1, D_MODEL), lambda i: (0, 0))],
        out_specs=pl.BlockSpec((TM_OUT, D_MODEL), lambda i: (i, 0)),
        out_shape=jax.ShapeDtypeStruct((L, D_MODEL), F32),
        compiler_params=pltpu.CompilerParams(dimension_semantics=("arbitrary",),
                                             vmem_limit_bytes=48 * MIB),
        name="out_projection_residual",
    )(y_ssd, y_ret, w_out_b, w_out_b, h, mod, final_norm_w.reshape(1, D_MODEL))


def kernel(x, c, w_ada, b_ada, norm_w, w_in, conv_w, conv_b, dt_bias, a_log, d_skip, ssd_norm_w, ret_norm_w,
           w_out, final_norm_w):
    bsz, L, d_model = x.shape
    assert bsz == 1 and d_model == D_MODEL and L % TM_IN == 0 and w_in.shape[-1] == IN_WIDTH
    depth = w_in.shape[0]
    h = x.reshape(L, D_MODEL)
    for layer in range(depth):
        w = w_in[layer]
        xbc = w[:, OFF_Z:OFF_XBC]
        w_perm = jnp.concatenate(
            [w[:, :OFF_Z], xbc[:, :SSD_D_INNER], w[:, OFF_K:OFF_V], w[:, OFF_V:], w[:, OFF_DT:OFF_Q],
             w[:, OFF_Q:OFF_K], xbc[:, SSD_D_INNER:]], axis=1).astype(BF16)
        w_dt = jnp.pad(w[:, OFF_XBC:OFF_DT], ((0, 0), (0, LANES - SSD_HEADS))).astype(BF16)
        mod = _modulation(c, w_ada[layer], b_ada[layer])
        proj, dt_raw = _in_projection(h, norm_w[layer], mod, w_perm, w_dt)
        y_ssd = _ssd(proj, dt_raw, conv_w[layer], conv_b[layer], dt_bias[layer], a_log[layer], d_skip[layer],
                     ssd_norm_w[layer])
        y_ret = _retention(proj, ret_norm_w[layer])
        h = _out_projection(y_ssd, y_ret, w_out[layer].astype(BF16), h, mod, final_norm_w,
                            final=layer == depth - 1)
    return h.reshape(bsz, L, D_MODEL)
```

```python
import functools

import numpy as np
import jax
import jax.numpy as jnp
from jax import lax
from jax.experimental import pallas as pl
from jax.experimental.pallas import tpu as pltpu

D_MODEL = 2048
SSD_D_INNER = D_MODEL
SSD_HEAD_DIM = 64
SSD_HEADS = SSD_D_INNER // SSD_HEAD_DIM
HEAD_DIM_LOG2 = SSD_HEAD_DIM.bit_length() - 1
SSD_GROUPS = 4
SSD_STATE = 128
SSD_CONV = 4
SSD_BC = SSD_GROUPS * SSD_STATE
SSD_CONV_DIM = SSD_D_INNER + 2 * SSD_BC
RET_HEADS = 8
RET_QK_DIM = 128
RET_V_DIM = 256
RET_QK_WIDTH = RET_HEADS * RET_QK_DIM
RET_V_WIDTH = RET_HEADS * RET_V_DIM
MIX_WIDTH = SSD_D_INNER + RET_V_WIDTH
CHUNK = 128
ROPE_BASE = 10000.0
EPS = 1e-6

OFF_Z = SSD_D_INNER
OFF_XBC = OFF_Z + SSD_CONV_DIM
OFF_DT = OFF_XBC + SSD_HEADS
OFF_Q = OFF_DT + RET_QK_WIDTH
OFF_K = OFF_Q + RET_QK_WIDTH
OFF_V = OFF_K + RET_V_WIDTH
IN_WIDTH = OFF_V + RET_V_WIDTH

PROJ_WIDTH = IN_WIDTH - SSD_HEADS
COL_Z, COL_X = 0, 1
COL_B, COL_C = 8, 9
COL_Q, COL_K, COL_V, COL_G = 5, 6, 7, 9

LANES = 128
SUBLANES = 8
QUAD = 4
QUAD_W = QUAD * SSD_HEAD_DIM
HEADS_PER_GROUP = SSD_HEADS // SSD_GROUPS
GROUP_W = SSD_D_INNER // SSD_GROUPS

TM_IN, TN_IN = 1024, 1024
N_TILES_BEFORE_DT, _rem = divmod(OFF_XBC, TN_IN)
assert _rem == 0 and PROJ_WIDTH % TN_IN == 0
TM_OUT = 512
MIB = 1024 * 1024

F32 = jnp.float32
BF16 = jnp.bfloat16


def _silu(v):
    return v / (1.0 + jnp.exp(-v))


def _dot(a, b):
    return jnp.dot(a, b, preferred_element_type=F32)


def _dot_nt(a, b):
    return lax.dot_general(a, b, (((1,), (1,)), ((), ())), preferred_element_type=F32)


def _dot_tn(a, b):
    return lax.dot_general(a, b, (((0,), (0,)), ((), ())), preferred_element_type=F32)


def _split3(v):
    hi = v.astype(BF16)
    r1 = v - hi.astype(F32)
    mid = r1.astype(BF16)
    lo = (r1 - mid.astype(F32)).astype(BF16)
    return hi, mid, lo


def _mod_kernel(c_ref, w_ref, b_ref, o_ref):
    cond = jnp.broadcast_to(_silu(c_ref[...]), (SUBLANES, D_MODEL))
    o_ref[...] = _dot(cond, w_ref[...])[0:1, :] + b_ref[...]


def _modulation(c, w_ada, b_ada):
    n = w_ada.shape[1]
    tn = 512
    return pl.pallas_call(
        _mod_kernel,
        grid=(n // tn,),
        in_specs=[pl.BlockSpec((1, D_MODEL), lambda j: (0, 0)),
                  pl.BlockSpec((D_MODEL, tn), lambda j: (0, j)),
                  pl.BlockSpec((1, tn), lambda j: (0, j))],
        out_specs=pl.BlockSpec((1, tn), lambda j: (0, j)),
        out_shape=jax.ShapeDtypeStruct((1, n), F32),
        compiler_params=pltpu.CompilerParams(dimension_semantics=("arbitrary",),
                                             vmem_limit_bytes=24 * MIB),
        name="adaln_modulation",
    )(c, w_ada, b_ada.reshape(1, n))


def _inproj_kernel(x_ref, nw_ref, shift_ref, scale_ref, wa_ref, wb_ref, wd_ref, o_ref, dt_ref, u_ref, w_ref):
    j = pl.program_id(1)

    @pl.when(j == 0)
    def _():
        x = x_ref[...]
        y = x * lax.rsqrt(jnp.mean(x * x, axis=-1, keepdims=True) + EPS) * nw_ref[...]
        u = (y * (1.0 + scale_ref[...]) + shift_ref[...]).astype(BF16)
        u_ref[...] = u
        is_dt = lax.broadcasted_iota(jnp.int32, (D_MODEL, LANES), 1) < SSD_HEADS
        dt_ref[...] = _dot(u, jnp.where(is_dt, wd_ref[...], 0.0).astype(BF16))

    @pl.when(j < N_TILES_BEFORE_DT)
    def _():
        w_ref[...] = wa_ref[...].astype(BF16)

    @pl.when(j >= N_TILES_BEFORE_DT)
    def _():
        w_ref[...] = jnp.concatenate([wa_ref[:, SSD_HEADS:], wb_ref[:, :SSD_HEADS]], axis=1).astype(BF16)

    o_ref[...] = _dot(u_ref[...], w_ref[...]).astype(BF16)


def _in_projection(h, norm_w, mod, w_in):
    L = h.shape[0]
    row = lambda i, j: (0, 0)
    lanes_per_tile = TN_IN // LANES
    return pl.pallas_call(
        _inproj_kernel,
        grid=(L // TM_IN, PROJ_WIDTH // TN_IN),
        in_specs=[pl.BlockSpec((TM_IN, D_MODEL), lambda i, j: (i, 0)),
                  pl.BlockSpec((1, D_MODEL), row),
                  pl.BlockSpec((1, D_MODEL), lambda i, j: (0, 0)),
                  pl.BlockSpec((1, D_MODEL), lambda i, j: (0, 1)),
                  pl.BlockSpec((D_MODEL, TN_IN), lambda i, j: (0, j)),
                  pl.BlockSpec((D_MODEL, LANES), lambda i, j: (0, (j + 1) * lanes_per_tile)),
                  pl.BlockSpec((D_MODEL, LANES), lambda i, j: (0, OFF_XBC // LANES))],
        out_specs=[pl.BlockSpec((TM_IN, TN_IN), lambda i, j: (i, j)),
                   pl.BlockSpec((TM_IN, LANES), lambda i, j: (i, 0))],
        out_shape=[jax.ShapeDtypeStruct((L, PROJ_WIDTH), BF16),
                   jax.ShapeDtypeStruct((L, LANES), F32)],
        scratch_shapes=[pltpu.VMEM((TM_IN, D_MODEL), BF16), pltpu.VMEM((D_MODEL, TN_IN), BF16)],
        compiler_params=pltpu.CompilerParams(dimension_semantics=("arbitrary", "arbitrary"),
                                             vmem_limit_bytes=56 * MIB),
        name="adaln_in_projection",
    )(h, norm_w.reshape(1, D_MODEL), mod, mod, w_in, w_in, w_in)


def _ssd_kernel(z_ref, x_ref, b_ref, c_ref, dtraw_ref, cw_ref, cb_ref, dtb_ref, alog_ref,
                dskip_ref, nw_ref, o_ref, xbuf, state, expand):
    int_iota = lambda shape, dim: lax.broadcasted_iota(jnp.int32, shape, dim)

    @pl.when(pl.program_id(0) == 0)
    def _():
        xbuf[0:SUBLANES, :] = jnp.zeros((SUBLANES, SSD_CONV_DIM), F32)
        state[...] = jnp.zeros_like(state)
        head_of_lane = jnp.right_shift(int_iota((LANES, SSD_D_INNER), 1), HEAD_DIM_LOG2)
        expand[...] = jnp.where(head_of_lane == int_iota((LANES, SSD_D_INNER), 0), 1.0, 0.0).astype(BF16)

    xbuf[SUBLANES:, 0:SSD_D_INNER] = x_ref[...].astype(F32)
    xbuf[SUBLANES:, SSD_D_INNER:SSD_D_INNER + SSD_BC] = b_ref[...].astype(F32)
    xbuf[SUBLANES:, SSD_D_INNER + SSD_BC:] = c_ref[...].astype(F32)
    acc = cb_ref[...] + cw_ref[SSD_CONV - 1:SSD_CONV, :] * xbuf[SUBLANES:, :]
    for k in range(SSD_CONV - 1):
        off = SUBLANES - (SSD_CONV - 1) + k
        acc = acc + cw_ref[k:k + 1, :] * xbuf[off:off + CHUNK, :]
    xbuf[0:SUBLANES, :] = xbuf[CHUNK:CHUNK + SUBLANES, :]
    xbc = _silu(acc)
    xs = xbc[:, 0:SSD_D_INNER]
    xs_b = xs.astype(BF16)
    bmat = xbc[:, SSD_D_INNER:SSD_D_INNER + SSD_BC]
    cmat = xbc[:, SSD_D_INNER + SSD_BC:]

    pre = dtraw_ref[...] + dtb_ref[...]
    dt = jnp.maximum(pre, 0.0) + jnp.log1p(jnp.exp(-jnp.abs(pre)))
    d_a = dt * (-jnp.exp(alog_ref[...]))
    rows = int_iota((CHUNK, CHUNK), 0)
    cols = int_iota((CHUNK, CHUNK), 1)
    causal = rows >= cols
    tril = jnp.where(causal, 1.0, 0.0).astype(BF16)
    acs = sum(_dot(tril, part) for part in _split3(d_a))
    acs_t = acs.T
    dt_t = dt.T
    w_t = jnp.exp(acs_t[:, CHUNK - 1:CHUNK] - acs_t) * dt_t
    end = jnp.broadcast_to(jnp.exp(acs[CHUNK - 1:CHUNK, :]), (SUBLANES, LANES))
    end_row = sum(_dot(part, expand[...]) for part in _split3(end))[0:1, :]

    lane_blk = jnp.right_shift(int_iota((CHUNK, QUAD_W), 1), HEAD_DIM_LOG2)

    def block_diag(v):
        return jnp.concatenate([jnp.where(lane_blk == j, v, jnp.zeros_like(v)) for j in range(QUAD)], axis=0)

    for g in range(SSD_GROUPS):
        gs = slice(g * SSD_STATE, (g + 1) * SSD_STATE)
        b_g = bmat[:, gs]
        c_g = cmat[:, gs]
        cb_g = _dot_nt(c_g.astype(BF16), b_g.astype(BF16))
        b_gt = b_g.T
        y_parts = []
        for qi in range(HEADS_PER_GROUP // QUAD):
            q = g * (HEADS_PER_GROUP // QUAD) + qi
            qs = slice(q * QUAD_W, (q + 1) * QUAD_W)
            lhs_diag, lhs_state, lhs_off = [], [], []
            for j in range(QUAD):
                hd = q * QUAD + j
                col = jnp.broadcast_to(acs[:, hd:hd + 1], (CHUNK, CHUNK))
                decay = jnp.exp(jnp.where(causal, col - acs_t[hd:hd + 1, :], -jnp.inf))
                lhs_diag.append((cb_g * decay * dt_t[hd:hd + 1, :]).astype(BF16))
                lhs_state.append((b_gt * w_t[hd:hd + 1, :]).astype(BF16))
                lhs_off.append((c_g * jnp.exp(col)).astype(BF16))
            x_blk = block_diag(xs_b[:, qs])
            s_prev = state[:, qs]
            s_blk = block_diag(s_prev.astype(BF16))
            y_q = _dot(jnp.concatenate(lhs_diag, axis=1), x_blk)
            y_q = y_q + _dot(jnp.concatenate(lhs_off, axis=1), s_blk)
            state[:, qs] = s_prev * end_row[:, qs] + _dot(jnp.concatenate(lhs_state, axis=1), x_blk)
            y_parts.append(y_q + xs[:, qs] * dskip_ref[:, qs])
        ws = slice(g * GROUP_W, (g + 1) * GROUP_W)
        u = jnp.concatenate(y_parts, axis=1) * _silu(z_ref[:, ws].astype(F32))
        u = u * lax.rsqrt(jnp.mean(u * u, axis=-1, keepdims=True) + EPS)
        o_ref[:, ws] = (u * nw_ref[:, ws]).astype(BF16)


def _ssd(proj, dt_raw, conv_w, conv_b, dt_bias, a_log, d_skip, ssd_norm_w):
    L = proj.shape[0]
    pad = lambda v: jnp.pad(v.reshape(1, SSD_HEADS), ((0, 0), (0, LANES - SSD_HEADS)))
    full = lambda shape: pl.BlockSpec(shape, lambda c: (0, 0))
    return pl.pallas_call(
        _ssd_kernel,
        grid=(L // CHUNK,),
        in_specs=[pl.BlockSpec((CHUNK, SSD_D_INNER), lambda c: (c, COL_Z)),
                  pl.BlockSpec((CHUNK, SSD_D_INNER), lambda c: (c, COL_X)),
                  pl.BlockSpec((CHUNK, SSD_BC), lambda c: (c, COL_B)),
                  pl.BlockSpec((CHUNK, SSD_BC), lambda c: (c, COL_C)),
                  pl.BlockSpec((CHUNK, LANES), lambda c: (c, 0)),
                  full((SSD_CONV, SSD_CONV_DIM)), full((1, SSD_CONV_DIM)),
                  full((1, LANES)), full((1, LANES)),
                  full((1, SSD_D_INNER)), full((1, SSD_D_INNER))],
        out_specs=pl.BlockSpec((CHUNK, SSD_D_INNER), lambda c: (c, 0)),
        out_shape=jax.ShapeDtypeStruct((L, SSD_D_INNER), BF16),
        scratch_shapes=[pltpu.VMEM((CHUNK + SUBLANES, SSD_CONV_DIM), F32),
                        pltpu.VMEM((SSD_STATE, SSD_D_INNER), F32),
                        pltpu.VMEM((LANES, SSD_D_INNER), BF16)],
        compiler_params=pltpu.CompilerParams(dimension_semantics=("arbitrary",),
                                             vmem_limit_bytes=32 * MIB),
        name="ssd_head_group",
    )(proj, proj, proj, proj, dt_raw, conv_w, conv_b.reshape(1, SSD_CONV_DIM), pad(dt_bias), pad(a_log),
      jnp.repeat(d_skip, SSD_HEAD_DIM).reshape(1, SSD_D_INNER), ssd_norm_w.reshape(1, SSD_D_INNER))


def _log_gammas():
    return [float(np.log1p(-np.exp2(np.float32(-5.0 - hd)), dtype=np.float32)) for hd in range(RET_HEADS)]


def _ret_kernel(q_ref, k_ref, v0_ref, v1_ref, g0_ref, g1_ref, inv_ref, nw_ref, o_ref, state, cos_in, sin_in,
                dmask, zeta, xi):
    c = pl.program_id(0)
    heads_per_block = RET_QK_WIDTH // RET_V_DIM
    log_gammas = _log_gammas()
    half = RET_QK_DIM // 2

    @pl.when(c == 0)
    def _():
        state[...] = jnp.zeros_like(state)
        ri = lax.broadcasted_iota(jnp.int32, (CHUNK, CHUNK), 0).astype(F32)
        ci = lax.broadcasted_iota(jnp.int32, (CHUNK, CHUNK), 1).astype(F32)
        ang = ri * inv_ref[...]
        cos_in[...] = jnp.cos(ang)
        sin_in[...] = jnp.sin(ang)
        for hd, lg in enumerate(log_gammas):
            dmask[hd] = jnp.where(ri >= ci, jnp.exp((ri - ci) * lg), 0.0)
            zeta[hd] = jnp.exp((CHUNK - 1.0 - ri) * lg)
            xi[hd] = jnp.exp((ri + 1.0) * lg)

    base = jnp.broadcast_to((c * CHUNK).astype(F32) * inv_ref[...], (SUBLANES, RET_QK_DIM))
    cos_b = jnp.cos(base)[0:1, :]
    sin_b = jnp.sin(base)[0:1, :]
    cos_t = cos_in[...] * cos_b - sin_in[...] * sin_b
    sin_t = sin_in[...] * cos_b + cos_in[...] * sin_b
    sin_t = jnp.where(lax.broadcasted_iota(jnp.int32, (CHUNK, RET_QK_DIM), 1) < half, -sin_t, sin_t)
    k_scale = RET_QK_DIM ** -0.5
    cos_k = cos_t * k_scale
    sin_k = sin_t * k_scale

    for hd in range(RET_HEADS):
        ks = slice(hd * RET_QK_DIM, (hd + 1) * RET_QK_DIM)
        vs = slice(hd * RET_V_DIM, (hd + 1) * RET_V_DIM)
        v_ref, g_ref = (v0_ref, g0_ref) if hd < heads_per_block else (v1_ref, g1_ref)
        bs = slice((hd % heads_per_block) * RET_V_DIM, (hd % heads_per_block + 1) * RET_V_DIM)
        q_h = q_ref[:, ks].astype(F32)
        k_h = k_ref[:, ks].astype(F32)
        q_r = q_h * cos_t + pltpu.roll(q_h, half, 1) * sin_t
        k_r = k_h * cos_k + pltpu.roll(k_h, half, 1) * sin_k
        v_h = v_ref[:, bs]
        scores = _dot_nt(q_r.astype(BF16), k_r.astype(BF16)) * dmask[hd]
        s_prev = state[hd]
        o = _dot(scores.astype(BF16), v_h) + _dot((q_r * xi[hd]).astype(BF16), s_prev.astype(BF16))
        state[hd] = s_prev * float(np.exp(np.float32(CHUNK * log_gammas[hd]))) + _dot_tn(
            (k_r * zeta[hd]).astype(BF16), v_h)
        mu = jnp.mean(o, axis=-1, keepdims=True)
        d = o - mu
        y = d * lax.rsqrt(jnp.mean(d * d, axis=-1, keepdims=True) + EPS) * nw_ref[:, vs]
        o_ref[:, vs] = (y * _silu(g_ref[:, bs].astype(F32))).astype(BF16)


def _retention(proj, ret_norm_w):
    L = proj.shape[0]
    half = RET_QK_DIM // 2
    inv = ROPE_BASE ** (-jnp.arange(half, dtype=F32) / half)
    inv = jnp.concatenate([inv, inv]).reshape(1, RET_QK_DIM)
    full = lambda shape: pl.BlockSpec(shape, lambda c: (0, 0))
    table = pltpu.VMEM((RET_HEADS, CHUNK, CHUNK), F32)
    return pl.pallas_call(
        _ret_kernel,
        grid=(L // CHUNK,),
        in_specs=[pl.BlockSpec((CHUNK, RET_QK_WIDTH), lambda c: (c, COL_Q)),
                  pl.BlockSpec((CHUNK, RET_QK_WIDTH), lambda c: (c, COL_K)),
                  pl.BlockSpec((CHUNK, RET_QK_WIDTH), lambda c: (c, COL_V)),
                  pl.BlockSpec((CHUNK, RET_QK_WIDTH), lambda c: (c, COL_V + 1)),
                  pl.BlockSpec((CHUNK, RET_QK_WIDTH), lambda c: (c, COL_G)),
                  pl.BlockSpec((CHUNK, RET_QK_WIDTH), lambda c: (c, COL_G + 1)),
                  full((1, RET_QK_DIM)), full((1, RET_V_WIDTH))],
        out_specs=pl.BlockSpec((CHUNK, RET_V_WIDTH), lambda c: (c, 0)),
        out_shape=jax.ShapeDtypeStruct((L, RET_V_WIDTH), BF16),
        scratch_shapes=[pltpu.VMEM((RET_HEADS, RET_QK_DIM, RET_V_DIM), F32),
                        pltpu.VMEM((CHUNK, RET_QK_DIM), F32), pltpu.VMEM((CHUNK, RET_QK_DIM), F32),
                        table, table, table],
        compiler_params=pltpu.CompilerParams(dimension_semantics=("arbitrary",),
                                             vmem_limit_bytes=32 * MIB),
        name="retention_head_group",
    )(proj, proj, proj, proj, proj, proj, inv, ret_norm_w.reshape(1, RET_V_WIDTH))


def _outproj_kernel(ys_ref, yr_ref, ws_ref, wr_ref, h_ref, gate_ref, fw_ref, o_ref, *, final):
    mixed = _dot(ys_ref[...], ws_ref[...]) + _dot(yr_ref[...], wr_ref[...])
    h = h_ref[...] + gate_ref[...] * mixed
    if final:
        h = h * lax.rsqrt(jnp.mean(h * h, axis=-1, keepdims=True) + EPS) * fw_ref[...]
    o_ref[...] = h


def _out_projection(y_ssd, y_ret, w_out_b, h, mod, final_norm_w, final):
    L = h.shape[0]
    once = pl.Buffered(1)
    return pl.pallas_call(
        functools.partial(_outproj_kernel, final=final),
        grid=(L // TM_OUT,),
        in_specs=[pl.BlockSpec((TM_OUT, SSD_D_INNER), lambda i: (i, 0)),
                  pl.BlockSpec((TM_OUT, RET_V_WIDTH), lambda i: (i, 0)),
                  pl.BlockSpec((SSD_D_INNER, D_MODEL), lambda i: (0, 0), pipeline_mode=once),
                  pl.BlockSpec((RET_V_WIDTH, D_MODEL), lambda i: (1, 0), pipeline_mode=once),
                  pl.BlockSpec((TM_OUT, D_MODEL), lambda i: (i, 0)),
                  pl.BlockSpec((1, D_MODEL), lambda i: (0, 2)),
                  pl.BlockSpec((1, D_MODEL), lambda i: (0, 0))],
        out_specs=pl.BlockSpec((TM_OUT, D_MODEL), lambda i: (i, 0)),
        out_shape=jax.ShapeDtypeStruct((L, D_MODEL), F32),
        compiler_params=pltpu.CompilerParams(dimension_semantics=("arbitrary",),
                                             vmem_limit_bytes=48 * MIB),
        name="out_projection_residual",
    )(y_ssd, y_ret, w_out_b, w_out_b, h, mod, final_norm_w.reshape(1, D_MODEL))


def kernel(x, c, w_ada, b_ada, norm_w, w_in, conv_w, conv_b, dt_bias, a_log, d_skip, ssd_norm_w, ret_norm_w,
           w_out, final_norm_w):
    bsz, L, d_model = x.shape
    assert bsz == 1 and d_model == D_MODEL and L % TM_IN == 0 and w_in.shape[-1] == IN_WIDTH
    depth = w_in.shape[0]
    h = x.reshape(L, D_MODEL)
    for layer in range(depth):
        mod = _modulation(c, w_ada[layer], b_ada[layer])
        proj, dt_raw = _in_projection(h, norm_w[layer], mod, w_in[layer])
        y_ssd = _ssd(proj, dt_raw, conv_w[layer], conv_b[layer], dt_bias[layer], a_log[layer], d_skip[layer],
                     ssd_norm_w[layer])
        y_ret = _retention(proj, ret_norm_w[layer])
        h = _out_projection(y_ssd, y_ret, w_out[layer].astype(BF16), h, mod, final_norm_w,
                            final=layer == depth - 1)
    return h.reshape(bsz, L, D_MODEL)
```

```python
import functools
import math

import numpy as np
import jax
import jax.numpy as jnp
from jax import lax
from jax.experimental import pallas as pl
from jax.experimental.pallas import tpu as pltpu

D_MODEL = 2048
SSD_D_INNER = D_MODEL
SSD_HEAD_DIM = 64
SSD_HEADS = SSD_D_INNER // SSD_HEAD_DIM
HEAD_DIM_LOG2 = SSD_HEAD_DIM.bit_length() - 1
SSD_GROUPS = 4
SSD_STATE = 128
SSD_CONV = 4
SSD_BC = SSD_GROUPS * SSD_STATE
SSD_CONV_DIM = SSD_D_INNER + 2 * SSD_BC
RET_HEADS = 8
RET_QK_DIM = 128
RET_V_DIM = 256
RET_QK_WIDTH = RET_HEADS * RET_QK_DIM
RET_V_WIDTH = RET_HEADS * RET_V_DIM
MIX_WIDTH = SSD_D_INNER + RET_V_WIDTH
CHUNK = 128
ROPE_BASE = 10000.0
EPS = 1e-6
LOG2_E = math.log2(math.e)

OFF_Z = SSD_D_INNER
OFF_XBC = OFF_Z + SSD_CONV_DIM
OFF_DT = OFF_XBC + SSD_HEADS
OFF_Q = OFF_DT + RET_QK_WIDTH
OFF_K = OFF_Q + RET_QK_WIDTH
OFF_V = OFF_K + RET_V_WIDTH
IN_WIDTH = OFF_V + RET_V_WIDTH

PROJ_WIDTH = IN_WIDTH - SSD_HEADS
COL_Z, COL_X = 0, 1
COL_B, COL_C = 8, 9
COL_Q, COL_K, COL_V, COL_G = 5, 6, 7, 9

LANES = 128
SUBLANES = 8
CONV_TAIL = 16
W_OUT_ROWS = 256
QUAD = 4
QUAD_W = QUAD * SSD_HEAD_DIM
HEADS_PER_GROUP = SSD_HEADS // SSD_GROUPS
GROUP_W = SSD_D_INNER // SSD_GROUPS

TM_IN, TN_IN = 1024, 1024
N_TILES_BEFORE_DT, _rem = divmod(OFF_XBC, TN_IN)
assert _rem == 0 and PROJ_WIDTH % TN_IN == 0
MIB = 1024 * 1024

F32 = jnp.float32
BF16 = jnp.bfloat16


def _silu(v):
    return v / (1.0 + jnp.exp(-v))


def _dot(a, b):
    return jnp.dot(a, b, preferred_element_type=F32)


def _dot_nt(a, b):
    return lax.dot_general(a, b, (((1,), (1,)), ((), ())), preferred_element_type=F32)


def _dot_tn(a, b):
    return lax.dot_general(a, b, (((0,), (0,)), ((), ())), preferred_element_type=F32)


def _split3(v):
    hi = v.astype(BF16)
    r1 = v - hi.astype(F32)
    mid = r1.astype(BF16)
    lo = (r1 - mid.astype(F32)).astype(BF16)
    return hi, mid, lo


def _mod_kernel(c_ref, w_ref, b_ref, o_ref):
    cond = jnp.broadcast_to(_silu(c_ref[...]), (SUBLANES, D_MODEL))
    o_ref[...] = _dot(cond, w_ref[...])[0:1, :] + b_ref[...]


def _modulation(c, w_ada, b_ada):
    n = w_ada.shape[1]
    tn = 512
    return pl.pallas_call(
        _mod_kernel,
        grid=(n // tn,),
        in_specs=[pl.BlockSpec((1, D_MODEL), lambda j: (0, 0)),
                  pl.BlockSpec((D_MODEL, tn), lambda j: (0, j)),
                  pl.BlockSpec((1, tn), lambda j: (0, j))],
        out_specs=pl.BlockSpec((1, tn), lambda j: (0, j)),
        out_shape=jax.ShapeDtypeStruct((1, n), F32),
        compiler_params=pltpu.CompilerParams(dimension_semantics=("arbitrary",),
                                             vmem_limit_bytes=24 * MIB),
        name="adaln_modulation",
    )(c, w_ada, b_ada.reshape(1, n))


def _inproj_kernel(x_ref, nw_ref, shift_ref, scale_ref, wa_ref, wb_ref, wd_ref, o_ref, dt_ref, u_ref):
    j = pl.program_id(1)

    @pl.when(j == 0)
    def _():
        x = x_ref[...]
        y = x * lax.rsqrt(jnp.mean(x * x, axis=-1, keepdims=True) + EPS) * nw_ref[...]
        u = (y * (1.0 + scale_ref[...]) + shift_ref[...]).astype(BF16)
        u_ref[...] = u
        is_dt = lax.broadcasted_iota(jnp.int32, (LANES, D_MODEL), 0) < SSD_HEADS
        dt_ref[...] = _dot_nt(u, jnp.where(is_dt, wd_ref[...], 0.0).astype(BF16))

    after_dt = j >= N_TILES_BEFORE_DT
    skip = pl.multiple_of(jnp.where(after_dt, SSD_HEADS, 0), SSD_HEADS)
    head = wa_ref[pl.ds(skip, TN_IN - SSD_HEADS), :]
    tail = jnp.where(after_dt, wb_ref[...], wa_ref[TN_IN - SSD_HEADS:, :])
    w = jnp.concatenate([head, tail], axis=0).astype(BF16)
    o_ref[...] = _dot_nt(u_ref[...], w).astype(BF16)


def _in_projection(h, norm_w, mod, w_in_t):
    L = h.shape[0]
    row = lambda i, j: (0, 0)
    return pl.pallas_call(
        _inproj_kernel,
        grid=(L // TM_IN, PROJ_WIDTH // TN_IN),
        in_specs=[pl.BlockSpec((TM_IN, D_MODEL), lambda i, j: (i, 0)),
                  pl.BlockSpec((1, D_MODEL), row),
                  pl.BlockSpec((1, D_MODEL), lambda i, j: (0, 0)),
                  pl.BlockSpec((1, D_MODEL), lambda i, j: (0, 1)),
                  pl.BlockSpec((TN_IN, D_MODEL), lambda i, j: (j, 0)),
                  pl.BlockSpec((SSD_HEADS, D_MODEL), lambda i, j: ((j + 1) * (TN_IN // SSD_HEADS), 0)),
                  pl.BlockSpec((LANES, D_MODEL), lambda i, j: (OFF_XBC // LANES, 0))],
        out_specs=[pl.BlockSpec((TM_IN, TN_IN), lambda i, j: (i, j)),
                   pl.BlockSpec((TM_IN, LANES), lambda i, j: (i, 0))],
        out_shape=[jax.ShapeDtypeStruct((L, PROJ_WIDTH), BF16),
                   jax.ShapeDtypeStruct((L, LANES), F32)],
        scratch_shapes=[pltpu.VMEM((TM_IN, D_MODEL), BF16)],
        compiler_params=pltpu.CompilerParams(dimension_semantics=("arbitrary", "arbitrary"),
                                             vmem_limit_bytes=56 * MIB),
        name="adaln_in_projection",
    )(h, norm_w.reshape(1, D_MODEL), mod, mod, w_in_t, w_in_t, w_in_t)


def _ssd_init(xbuf, shifts, state, expand):
    xbuf[0:CONV_TAIL, :] = jnp.zeros((CONV_TAIL, SSD_CONV_DIM), BF16)
    row = lax.broadcasted_iota(jnp.int32, shifts.shape, 0)
    col = lax.broadcasted_iota(jnp.int32, shifts.shape, 1)
    delay = jnp.right_shift(row, CHUNK.bit_length() - 1) + 1
    t = jnp.bitwise_and(row, CHUNK - 1)
    shifts[...] = jnp.where(col == CONV_TAIL + t - delay, 1.0, 0.0).astype(BF16)
    state[...] = jnp.zeros_like(state)
    head_of_lane = jnp.right_shift(lax.broadcasted_iota(jnp.int32, (LANES, SSD_D_INNER), 1), HEAD_DIM_LOG2)
    row = lax.broadcasted_iota(jnp.int32, (LANES, SSD_D_INNER), 0)
    expand[...] = jnp.where(head_of_lane == row, 1.0, 0.0).astype(BF16)


def _ssd_chunk(z_ref, x_ref, b_ref, c_ref, dtraw_ref, cw_ref, cb_ref, dtb_ref, alog_ref, dskip_ref, nw_ref,
               y_ref, xbuf, shifts, state, expand):
    int_iota = lambda shape, dim: lax.broadcasted_iota(jnp.int32, shape, dim)

    xbuf[CONV_TAIL:, 0:SSD_D_INNER] = x_ref[...]
    xbuf[CONV_TAIL:, SSD_D_INNER:SSD_D_INNER + SSD_BC] = b_ref[...]
    xbuf[CONV_TAIL:, SSD_D_INNER + SSD_BC:] = c_ref[...]
    delayed = _dot(shifts[...], xbuf[...])
    acc = cb_ref[...] + cw_ref[SSD_CONV - 1:SSD_CONV, :] * xbuf[CONV_TAIL:, :].astype(F32)
    for k in range(SSD_CONV - 1):
        tap = SSD_CONV - 2 - k
        acc = acc + cw_ref[tap:tap + 1, :] * delayed[k * CHUNK:(k + 1) * CHUNK, :]
    xbuf[0:CONV_TAIL, :] = xbuf[CHUNK:CHUNK + CONV_TAIL, :]
    xbc = _silu(acc)
    xs = xbc[:, 0:SSD_D_INNER]
    xs_b = xs.astype(BF16)
    bmat = xbc[:, SSD_D_INNER:SSD_D_INNER + SSD_BC]
    cmat = xbc[:, SSD_D_INNER + SSD_BC:]

    pre = dtraw_ref[...] + dtb_ref[...]
    dt = jnp.maximum(pre, 0.0) + jnp.log1p(jnp.exp(-jnp.abs(pre)))
    d_a = dt * (-LOG2_E * jnp.exp(alog_ref[...]))
    causal = int_iota((CHUNK, CHUNK), 0) >= int_iota((CHUNK, CHUNK), 1)
    tril = jnp.where(causal, 1.0, 0.0).astype(BF16)
    acs = sum(_dot(tril, part) for part in _split3(d_a))
    acs_t = acs.T
    src_t = acs_t - jnp.log2(dt).T
    w_t = jnp.exp2(acs_t[:, CHUNK - 1:CHUNK] - src_t)
    end = jnp.broadcast_to(jnp.exp2(acs[CHUNK - 1:CHUNK, :]), (SUBLANES, LANES))
    end_row = sum(_dot(part, expand[...]) for part in _split3(end))[0:1, :]

    lane_blk = jnp.right_shift(int_iota((CHUNK, QUAD_W), 1), HEAD_DIM_LOG2)

    def block_diag(v):
        return jnp.concatenate([jnp.where(lane_blk == j, v, jnp.zeros_like(v)) for j in range(QUAD)], axis=0)

    for g in range(SSD_GROUPS):
        gs = slice(g * SSD_STATE, (g + 1) * SSD_STATE)
        b_g = bmat[:, gs]
        c_g = cmat[:, gs]
        cb_g = _dot_nt(c_g.astype(BF16), b_g.astype(BF16))
        b_gt = b_g.T
        y_parts = []
        for qi in range(HEADS_PER_GROUP // QUAD):
            q = g * (HEADS_PER_GROUP // QUAD) + qi
            qs = slice(q * QUAD_W, (q + 1) * QUAD_W)
            lhs_diag, lhs_state, lhs_off = [], [], []
            for j in range(QUAD):
                hd = q * QUAD + j
                col = jnp.broadcast_to(acs[:, hd:hd + 1], (CHUNK, CHUNK))
                decay_dt = jnp.exp2(jnp.where(causal, col - src_t[hd:hd + 1, :], -jnp.inf))
                lhs_diag.append((cb_g * decay_dt).astype(BF16))
                lhs_state.append((b_gt * w_t[hd:hd + 1, :]).astype(BF16))
                lhs_off.append((c_g * jnp.exp2(col)).astype(BF16))
            x_blk = block_diag(xs_b[:, qs])
            s_prev = state[:, qs]
            s_blk = block_diag(s_prev.astype(BF16))
            y_q = _dot(jnp.concatenate(lhs_diag, axis=1), x_blk)
            y_q = y_q + _dot(jnp.concatenate(lhs_off, axis=1), s_blk)
            state[:, qs] = s_prev * end_row[:, qs] + _dot(jnp.concatenate(lhs_state, axis=1), x_blk)
            y_parts.append(y_q + xs[:, qs] * dskip_ref[:, qs])
        ws = slice(g * GROUP_W, (g + 1) * GROUP_W)
        u = jnp.concatenate(y_parts, axis=1) * _silu(z_ref[:, ws].astype(F32))
        u = u * lax.rsqrt(jnp.mean(u * u, axis=-1, keepdims=True) + EPS)
        y_ref[:, ws] = (u * nw_ref[:, ws]).astype(BF16)


def _log_gammas():
    return [float(np.log1p(-np.exp2(np.float32(-5.0 - hd)), dtype=np.float32)) for hd in range(RET_HEADS)]


def _ret_init(inv_ref, state, cos_in, sin_in, dmask, zeta, xi):
    state[...] = jnp.zeros_like(state)
    ri = lax.broadcasted_iota(jnp.int32, (CHUNK, CHUNK), 0).astype(F32)
    ci = lax.broadcasted_iota(jnp.int32, (CHUNK, CHUNK), 1).astype(F32)
    ang = ri * inv_ref[...]
    cos_in[...] = jnp.cos(ang)
    sin_in[...] = jnp.sin(ang)
    for hd, lg in enumerate(_log_gammas()):
        dmask[hd] = jnp.where(ri >= ci, jnp.exp((ri - ci) * lg), 0.0)
        zeta[hd] = jnp.exp((CHUNK - 1.0 - ri) * lg)
        xi[hd] = jnp.exp((ri + 1.0) * lg)


def _ret_chunk(chunk, q_ref, k_ref, v0_ref, v1_ref, g0_ref, g1_ref, inv_ref, nw_ref, y_ref, state, cos_in, sin_in,
               dmask, zeta, xi):
    heads_per_block = RET_QK_WIDTH // RET_V_DIM
    log_gammas = _log_gammas()
    half = RET_QK_DIM // 2

    base = jnp.broadcast_to((chunk * CHUNK).astype(F32) * inv_ref[...], (SUBLANES, RET_QK_DIM))
    cos_b = jnp.cos(base)[0:1, :]
    sin_b = jnp.sin(base)[0:1, :]
    cos_t = cos_in[...] * cos_b - sin_in[...] * sin_b
    sin_t = sin_in[...] * cos_b + cos_in[...] * sin_b
    sin_t = jnp.where(lax.broadcasted_iota(jnp.int32, (CHUNK, RET_QK_DIM), 1) < half, -sin_t, sin_t)
    k_scale = RET_QK_DIM ** -0.5
    cos_k = cos_t * k_scale
    sin_k = sin_t * k_scale

    for hd in range(RET_HEADS):
        ks = slice(hd * RET_QK_DIM, (hd + 1) * RET_QK_DIM)
        vs = slice(hd * RET_V_DIM, (hd + 1) * RET_V_DIM)
        v_ref, g_ref = (v0_ref, g0_ref) if hd < heads_per_block else (v1_ref, g1_ref)
        bs = slice((hd % heads_per_block) * RET_V_DIM, (hd % heads_per_block + 1) * RET_V_DIM)
        q_h = q_ref[:, ks].astype(F32)
        k_h = k_ref[:, ks].astype(F32)
        q_r = q_h * cos_t + pltpu.roll(q_h, half, 1) * sin_t
        k_r = k_h * cos_k + pltpu.roll(k_h, half, 1) * sin_k
        v_h = v_ref[:, bs]
        scores = _dot_nt(q_r.astype(BF16), k_r.astype(BF16)) * dmask[hd]
        s_prev = state[hd]
        o = _dot(scores.astype(BF16), v_h) + _dot((q_r * xi[hd]).astype(BF16), s_prev.astype(BF16))
        state[hd] = s_prev * float(np.exp(np.float32(CHUNK * log_gammas[hd]))) + _dot_tn(
            (k_r * zeta[hd]).astype(BF16), v_h)
        mu = jnp.mean(o, axis=-1, keepdims=True)
        d = o - mu
        y = d * lax.rsqrt(jnp.mean(d * d, axis=-1, keepdims=True) + EPS) * nw_ref[:, vs]
        y = (y * _silu(g_ref[:, bs].astype(F32))).astype(BF16)
        y_ref[:, SSD_D_INNER + hd * RET_V_DIM:SSD_D_INNER + (hd + 1) * RET_V_DIM] = y


def _mixer_kernel(z_ref, x_ref, b_ref, c_ref, dtraw_ref, cw_ref, cb_ref, dtb_ref, alog_ref, dskip_ref, snw_ref,
                  q_ref, k_ref, v0_ref, v1_ref, g0_ref, g1_ref, inv_ref, rnw_ref,
                  wout_hbm, h_ref, gate_ref, fw_ref, o_ref,
                  xbuf, shifts, ssd_state, expand, ret_state, cos_in, sin_in, dmask, zeta, xi, ybuf,
                  wout, wstage, wsem, *, n_chunks, final):
    step = pl.program_id(0)

    def wout_copy(i, slot):
        return pltpu.make_async_copy(wout_hbm.at[pl.ds(i * W_OUT_ROWS, W_OUT_ROWS), :], wstage.at[slot],
                                     wsem.at[slot])

    @pl.when(step == 0)
    def _():
        n_stage = MIX_WIDTH // W_OUT_ROWS
        wout_copy(0, 0).start()

        def stage(i, carry):
            slot = lax.rem(i, 2)

            @pl.when(i + 1 < n_stage)
            def _():
                wout_copy(i + 1, 1 - slot).start()

            wout_copy(i, slot).wait()
            wout[pl.ds(pl.multiple_of(i * W_OUT_ROWS, W_OUT_ROWS), W_OUT_ROWS), :] = wstage[slot].astype(BF16)
            return carry

        lax.fori_loop(0, n_stage, stage, 0)
        _ssd_init(xbuf, shifts, ssd_state, expand)
        _ret_init(inv_ref, ret_state, cos_in, sin_in, dmask, zeta, xi)
        ybuf[...] = jnp.zeros_like(ybuf)

    slot = lax.rem(step, 2)
    y_new = ybuf.at[slot]
    _ssd_chunk(z_ref, x_ref, b_ref, c_ref, dtraw_ref, cw_ref, cb_ref, dtb_ref, alog_ref, dskip_ref, snw_ref,
               y_new, xbuf, shifts, ssd_state, expand)
    _ret_chunk(jnp.minimum(step, n_chunks - 1), q_ref, k_ref, v0_ref, v1_ref, g0_ref, g1_ref, inv_ref, rnw_ref,
               y_new, ret_state, cos_in, sin_in, dmask, zeta, xi)

    h = h_ref[...] + gate_ref[...] * _dot(ybuf[1 - slot], wout[...])
    if final:
        h = h * lax.rsqrt(jnp.mean(h * h, axis=-1, keepdims=True) + EPS) * fw_ref[...]
    o_ref[...] = h


def _mix_and_project(proj, dt_raw, h, mod, conv_w, conv_b, dt_bias, a_log, d_skip, ssd_norm_w, ret_norm_w,
                     w_out, final_norm_w, final):
    L = h.shape[0]
    n_chunks = L // CHUNK
    half = RET_QK_DIM // 2
    inv = ROPE_BASE ** (-jnp.arange(half, dtype=F32) / half)
    inv = jnp.concatenate([inv, inv]).reshape(1, RET_QK_DIM)
    pad = lambda v: jnp.pad(v.reshape(1, SSD_HEADS), ((0, 0), (0, LANES - SSD_HEADS)))
    full = lambda shape: pl.BlockSpec(shape, lambda s: (0, 0))
    mixed = lambda width, col: pl.BlockSpec((CHUNK, width), lambda s: (jnp.minimum(s, n_chunks - 1), col))
    projected = lambda width, col: pl.BlockSpec((CHUNK, width), lambda s: (jnp.maximum(s - 1, 0), col))
    table = pltpu.VMEM((RET_HEADS, CHUNK, CHUNK), F32)
    return pl.pallas_call(
        functools.partial(_mixer_kernel, n_chunks=n_chunks, final=final),
        grid=(n_chunks + 1,),
        in_specs=[mixed(SSD_D_INNER, COL_Z), mixed(SSD_D_INNER, COL_X), mixed(SSD_BC, COL_B), mixed(SSD_BC, COL_C),
                  mixed(LANES, 0),
                  full((SSD_CONV, SSD_CONV_DIM)), full((1, SSD_CONV_DIM)), full((1, LANES)), full((1, LANES)),
                  full((1, SSD_D_INNER)), full((1, SSD_D_INNER)),
                  mixed(RET_QK_WIDTH, COL_Q), mixed(RET_QK_WIDTH, COL_K),
                  mixed(RET_QK_WIDTH, COL_V), mixed(RET_QK_WIDTH, COL_V + 1),
                  mixed(RET_QK_WIDTH, COL_G), mixed(RET_QK_WIDTH, COL_G + 1),
                  full((1, RET_QK_DIM)), full((1, RET_V_WIDTH)),
                  pl.BlockSpec(memory_space=pl.ANY),
                  projected(D_MODEL, 0),
                  pl.BlockSpec((1, D_MODEL), lambda s: (0, 2)),
                  full((1, D_MODEL))],
        out_specs=projected(D_MODEL, 0),
        out_shape=jax.ShapeDtypeStruct((L, D_MODEL), F32),
        scratch_shapes=[pltpu.VMEM((CONV_TAIL + CHUNK, SSD_CONV_DIM), BF16),
                        pltpu.VMEM(((SSD_CONV - 1) * CHUNK, CONV_TAIL + CHUNK), BF16),
                        pltpu.VMEM((SSD_STATE, SSD_D_INNER), F32),
                        pltpu.VMEM((LANES, SSD_D_INNER), BF16),
                        pltpu.VMEM((RET_HEADS, RET_QK_DIM, RET_V_DIM), F32),
                        pltpu.VMEM((CHUNK, RET_QK_DIM), F32), pltpu.VMEM((CHUNK, RET_QK_DIM), F32),
                        table, table, table,
                        pltpu.VMEM((2, CHUNK, MIX_WIDTH), BF16),
                        pltpu.VMEM((MIX_WIDTH, D_MODEL), BF16),
                        pltpu.VMEM((2, W_OUT_ROWS, D_MODEL), F32),
                        pltpu.SemaphoreType.DMA((2,))],
        compiler_params=pltpu.CompilerParams(dimension_semantics=("arbitrary",),
                                             vmem_limit_bytes=48 * MIB),
        name="mixers_out_projection",
    )(proj, proj, proj, proj, dt_raw, conv_w, conv_b.reshape(1, SSD_CONV_DIM), pad(dt_bias), pad(a_log),
      jnp.repeat(d_skip, SSD_HEAD_DIM).reshape(1, SSD_D_INNER), ssd_norm_w.reshape(1, SSD_D_INNER),
      proj, proj, proj, proj, proj, proj, inv, ret_norm_w.reshape(1, RET_V_WIDTH),
      w_out, h, mod, final_norm_w.reshape(1, D_MODEL))


def kernel(x, c, w_ada, b_ada, norm_w, w_in, conv_w, conv_b, dt_bias, a_log, d_skip, ssd_norm_w, ret_norm_w,
           w_out, final_norm_w):
    bsz, L, d_model = x.shape
    assert bsz == 1 and d_model == D_MODEL and L % TM_IN == 0 and w_in.shape[-1] == IN_WIDTH
    depth = w_in.shape[0]
    h = x.reshape(L, D_MODEL)
    for layer in range(depth):
        mod = _modulation(c, w_ada[layer], b_ada[layer])
        proj, dt_raw = _in_projection(h, norm_w[layer], mod, jnp.swapaxes(w_in[layer], 0, 1))
        h = _mix_and_project(proj, dt_raw, h, mod, conv_w[layer], conv_b[layer], dt_bias[layer], a_log[layer],
                             d_skip[layer], ssd_norm_w[layer], ret_norm_w[layer], w_out[layer],
                             final_norm_w, final=layer == depth - 1)
    return h.reshape(bsz, L, D_MODEL)
```

```python
import functools
import math

import numpy as np
import jax
import jax.numpy as jnp
from jax import lax
from jax.experimental import pallas as pl
from jax.experimental.pallas import tpu as pltpu

D_MODEL = 2048
SSD_D_INNER = D_MODEL
SSD_HEAD_DIM = 64
SSD_HEADS = SSD_D_INNER // SSD_HEAD_DIM
HEAD_DIM_LOG2 = SSD_HEAD_DIM.bit_length() - 1
SSD_GROUPS = 4
SSD_STATE = 128
SSD_CONV = 4
SSD_BC = SSD_GROUPS * SSD_STATE
SSD_CONV_DIM = SSD_D_INNER + 2 * SSD_BC
RET_HEADS = 8
RET_QK_DIM = 128
RET_V_DIM = 256
RET_QK_WIDTH = RET_HEADS * RET_QK_DIM
RET_V_WIDTH = RET_HEADS * RET_V_DIM
MIX_WIDTH = SSD_D_INNER + RET_V_WIDTH
CHUNK = 128
ROPE_BASE = 10000.0
EPS = 1e-6
LOG2_E = math.log2(math.e)

OFF_Z = SSD_D_INNER
OFF_XBC = OFF_Z + SSD_CONV_DIM
OFF_DT = OFF_XBC + SSD_HEADS
OFF_Q = OFF_DT + RET_QK_WIDTH
OFF_K = OFF_Q + RET_QK_WIDTH
OFF_V = OFF_K + RET_V_WIDTH
IN_WIDTH = OFF_V + RET_V_WIDTH

PROJ_WIDTH = IN_WIDTH - SSD_HEADS
COL_Z, COL_X = 0, 1
COL_B, COL_C = 8, 9
COL_Q, COL_K, COL_V, COL_G = 5, 6, 7, 9

LANES = 128
SUBLANES = 8
CONV_TAIL = 16
CHUNKS_PER_STEP = 2
W_OUT_ROWS = 256
QUAD = 4
QUAD_W = QUAD * SSD_HEAD_DIM
HEADS_PER_GROUP = SSD_HEADS // SSD_GROUPS
GROUP_W = SSD_D_INNER // SSD_GROUPS

TM_IN, TN_IN = 1024, 1024
N_TILES_BEFORE_DT, _rem = divmod(OFF_XBC, TN_IN)
assert _rem == 0 and PROJ_WIDTH % TN_IN == 0
MIB = 1024 * 1024

F32 = jnp.float32
BF16 = jnp.bfloat16


def _silu(v):
    return v / (1.0 + jnp.exp(-v))


def _dot(a, b):
    return jnp.dot(a, b, preferred_element_type=F32)


def _dot_nt(a, b):
    return lax.dot_general(a, b, (((1,), (1,)), ((), ())), preferred_element_type=F32)


def _dot_tn(a, b):
    return lax.dot_general(a, b, (((0,), (0,)), ((), ())), preferred_element_type=F32)


def _split3(v):
    hi = v.astype(BF16)
    r1 = v - hi.astype(F32)
    mid = r1.astype(BF16)
    lo = (r1 - mid.astype(F32)).astype(BF16)
    return hi, mid, lo


def _mod_kernel(c_ref, w_ref, b_ref, o_ref):
    cond = jnp.broadcast_to(_silu(c_ref[...]), (SUBLANES, D_MODEL))
    o_ref[...] = _dot(cond, w_ref[...])[0:1, :] + b_ref[...]


def _modulation(c, w_ada, b_ada):
    n = w_ada.shape[1]
    tn = 512
    return pl.pallas_call(
        _mod_kernel,
        grid=(n // tn,),
        in_specs=[pl.BlockSpec((1, D_MODEL), lambda j: (0, 0)),
                  pl.BlockSpec((D_MODEL, tn), lambda j: (0, j)),
                  pl.BlockSpec((1, tn), lambda j: (0, j))],
        out_specs=pl.BlockSpec((1, tn), lambda j: (0, j)),
        out_shape=jax.ShapeDtypeStruct((1, n), F32),
        compiler_params=pltpu.CompilerParams(dimension_semantics=("arbitrary",),
                                             vmem_limit_bytes=24 * MIB),
        name="adaln_modulation",
    )(c, w_ada, b_ada.reshape(1, n))


def _inproj_kernel(x_ref, nw_ref, shift_ref, scale_ref, wa_ref, wb_ref, wd_ref, o_ref, dt_ref, u_ref):
    j = pl.program_id(1)

    @pl.when(j == 0)
    def _():
        x = x_ref[...]
        y = x * lax.rsqrt(jnp.mean(x * x, axis=-1, keepdims=True) + EPS) * nw_ref[...]
        u = (y * (1.0 + scale_ref[...]) + shift_ref[...]).astype(BF16)
        u_ref[...] = u
        is_dt = lax.broadcasted_iota(jnp.int32, (LANES, D_MODEL), 0) < SSD_HEADS
        dt_ref[...] = _dot_nt(u, jnp.where(is_dt, wd_ref[...], 0.0).astype(BF16))

    after_dt = j >= N_TILES_BEFORE_DT
    skip = pl.multiple_of(jnp.where(after_dt, SSD_HEADS, 0), SSD_HEADS)
    head = wa_ref[pl.ds(skip, TN_IN - SSD_HEADS), :]
    tail = jnp.where(after_dt, wb_ref[...], wa_ref[TN_IN - SSD_HEADS:, :])
    w = jnp.concatenate([head, tail], axis=0).astype(BF16)
    o_ref[...] = _dot_nt(u_ref[...], w).astype(BF16)


def _in_projection(h, norm_w, mod, w_in_t):
    L = h.shape[0]
    row = lambda i, j: (0, 0)
    return pl.pallas_call(
        _inproj_kernel,
        grid=(L // TM_IN, PROJ_WIDTH // TN_IN),
        in_specs=[pl.BlockSpec((TM_IN, D_MODEL), lambda i, j: (i, 0)),
                  pl.BlockSpec((1, D_MODEL), row),
                  pl.BlockSpec((1, D_MODEL), lambda i, j: (0, 0)),
                  pl.BlockSpec((1, D_MODEL), lambda i, j: (0, 1)),
                  pl.BlockSpec((TN_IN, D_MODEL), lambda i, j: (j, 0)),
                  pl.BlockSpec((SSD_HEADS, D_MODEL), lambda i, j: ((j + 1) * (TN_IN // SSD_HEADS), 0)),
                  pl.BlockSpec((LANES, D_MODEL), lambda i, j: (OFF_XBC // LANES, 0))],
        out_specs=[pl.BlockSpec((TM_IN, TN_IN), lambda i, j: (i, j)),
                   pl.BlockSpec((TM_IN, LANES), lambda i, j: (i, 0))],
        out_shape=[jax.ShapeDtypeStruct((L, PROJ_WIDTH), BF16),
                   jax.ShapeDtypeStruct((L, LANES), F32)],
        scratch_shapes=[pltpu.VMEM((TM_IN, D_MODEL), BF16)],
        compiler_params=pltpu.CompilerParams(dimension_semantics=("arbitrary", "arbitrary"),
                                             vmem_limit_bytes=56 * MIB),
        name="adaln_in_projection",
    )(h, norm_w.reshape(1, D_MODEL), mod, mod, w_in_t, w_in_t, w_in_t)


def _ssd_init(xbuf, shifts, state, expand):
    xbuf[0:CONV_TAIL, :] = jnp.zeros((CONV_TAIL, SSD_CONV_DIM), BF16)
    row = lax.broadcasted_iota(jnp.int32, shifts.shape, 0)
    col = lax.broadcasted_iota(jnp.int32, shifts.shape, 1)
    delay = jnp.right_shift(row, CHUNK.bit_length() - 1) + 1
    t = jnp.bitwise_and(row, CHUNK - 1)
    shifts[...] = jnp.where(col == CONV_TAIL + t - delay, 1.0, 0.0).astype(BF16)
    state[...] = jnp.zeros_like(state)
    head_of_lane = jnp.right_shift(lax.broadcasted_iota(jnp.int32, (LANES, SSD_D_INNER), 1), HEAD_DIM_LOG2)
    row = lax.broadcasted_iota(jnp.int32, (LANES, SSD_D_INNER), 0)
    expand[...] = jnp.where(head_of_lane == row, 1.0, 0.0).astype(BF16)


def _ssd_chunk(z_ref, x_ref, b_ref, c_ref, dtraw_ref, cw_ref, cb_ref, dtb_ref, alog_ref, dskip_ref, nw_ref,
               y_ref, xbuf, shifts, state, expand):
    int_iota = lambda shape, dim: lax.broadcasted_iota(jnp.int32, shape, dim)

    xbuf[CONV_TAIL:, 0:SSD_D_INNER] = x_ref[...]
    xbuf[CONV_TAIL:, SSD_D_INNER:SSD_D_INNER + SSD_BC] = b_ref[...]
    xbuf[CONV_TAIL:, SSD_D_INNER + SSD_BC:] = c_ref[...]
    delayed = _dot(shifts[...], xbuf[...])
    acc = cb_ref[...] + cw_ref[SSD_CONV - 1:SSD_CONV, :] * xbuf[CONV_TAIL:, :].astype(F32)
    for k in range(SSD_CONV - 1):
        tap = SSD_CONV - 2 - k
        acc = acc + cw_ref[tap:tap + 1, :] * delayed[k * CHUNK:(k + 1) * CHUNK, :]
    xbuf[0:CONV_TAIL, :] = xbuf[CHUNK:CHUNK + CONV_TAIL, :]
    xbc = _silu(acc)
    xs = xbc[:, 0:SSD_D_INNER]
    xs_b = xs.astype(BF16)
    bmat = xbc[:, SSD_D_INNER:SSD_D_INNER + SSD_BC]
    cmat = xbc[:, SSD_D_INNER + SSD_BC:]

    pre = dtraw_ref[...] + dtb_ref[...]
    dt = jnp.maximum(pre, 0.0) + jnp.log1p(jnp.exp(-jnp.abs(pre)))
    d_a = dt * (-LOG2_E * jnp.exp(alog_ref[...]))
    causal = int_iota((CHUNK, CHUNK), 0) >= int_iota((CHUNK, CHUNK), 1)
    tril = jnp.where(causal, 1.0, 0.0).astype(BF16)
    acs = sum(_dot(tril, part) for part in _split3(d_a))
    acs_t = acs.T
    src_t = acs_t - jnp.log2(dt).T
    w_t = jnp.exp2(acs_t[:, CHUNK - 1:CHUNK] - src_t)
    end = jnp.broadcast_to(jnp.exp2(acs[CHUNK - 1:CHUNK, :]), (SUBLANES, LANES))
    end_row = sum(_dot(part, expand[...]) for part in _split3(end))[0:1, :]

    lane_blk = jnp.right_shift(int_iota((CHUNK, QUAD_W), 1), HEAD_DIM_LOG2)

    def block_diag(v):
        return jnp.concatenate([jnp.where(lane_blk == j, v, jnp.zeros_like(v)) for j in range(QUAD)], axis=0)

    def group(g, between_quads=lambda q: None):
        gs = slice(g * SSD_STATE, (g + 1) * SSD_STATE)
        b_g = bmat[:, gs]
        c_g = cmat[:, gs]
        cb_g = _dot_nt(c_g.astype(BF16), b_g.astype(BF16))
        b_gt = b_g.T
        y_parts = []
        for qi in range(HEADS_PER_GROUP // QUAD):
            q = g * (HEADS_PER_GROUP // QUAD) + qi
            between_quads(q)
            qs = slice(q * QUAD_W, (q + 1) * QUAD_W)
            lhs_diag, lhs_state, lhs_off = [], [], []
            for j in range(QUAD):
                hd = q * QUAD + j
                col = jnp.broadcast_to(acs[:, hd:hd + 1], (CHUNK, CHUNK))
                decay_dt = jnp.exp2(jnp.where(causal, col - src_t[hd:hd + 1, :], -jnp.inf))
                lhs_diag.append((cb_g * decay_dt).astype(BF16))
                lhs_state.append((b_gt * w_t[hd:hd + 1, :]).astype(BF16))
                lhs_off.append((c_g * jnp.exp2(col)).astype(BF16))
            x_blk = block_diag(xs_b[:, qs])
            s_prev = state[:, qs]
            s_blk = block_diag(s_prev.astype(BF16))
            y_q = _dot(jnp.concatenate(lhs_diag, axis=1), x_blk)
            y_q = y_q + _dot(jnp.concatenate(lhs_off, axis=1), s_blk)
            state[:, qs] = s_prev * end_row[:, qs] + _dot(jnp.concatenate(lhs_state, axis=1), x_blk)
            y_parts.append(y_q + xs[:, qs] * dskip_ref[:, qs])
        ws = slice(g * GROUP_W, (g + 1) * GROUP_W)
        u = jnp.concatenate(y_parts, axis=1) * _silu(z_ref[:, ws].astype(F32))
        u = u * lax.rsqrt(jnp.mean(u * u, axis=-1, keepdims=True) + EPS)
        y_ref[:, ws] = (u * nw_ref[:, ws]).astype(BF16)

    return group


def _log_gammas():
    return [float(np.log1p(-np.exp2(np.float32(-5.0 - hd)), dtype=np.float32)) for hd in range(RET_HEADS)]


def _ret_init(inv_ref, state, cos_in, sin_in, dmask, zeta, xi):
    state[...] = jnp.zeros_like(state)
    ri = lax.broadcasted_iota(jnp.int32, (CHUNK, CHUNK), 0).astype(F32)
    ci = lax.broadcasted_iota(jnp.int32, (CHUNK, CHUNK), 1).astype(F32)
    ang = ri * inv_ref[...]
    cos_in[...] = jnp.cos(ang)
    sin_in[...] = jnp.sin(ang)
    for hd, lg in enumerate(_log_gammas()):
        dmask[hd] = jnp.where(ri >= ci, jnp.exp((ri - ci) * lg), 0.0)
        zeta[hd] = jnp.exp((CHUNK - 1.0 - ri) * lg)
        xi[hd] = jnp.exp((ri + 1.0) * lg)


def _ret_chunk(chunk, q_ref, k_ref, v0_ref, v1_ref, g0_ref, g1_ref, inv_ref, nw_ref, y_ref, state, cos_in, sin_in,
               dmask, zeta, xi):
    heads_per_block = RET_QK_WIDTH // RET_V_DIM
    log_gammas = _log_gammas()
    half = RET_QK_DIM // 2

    base = jnp.broadcast_to((chunk * CHUNK).astype(F32) * inv_ref[...], (SUBLANES, RET_QK_DIM))
    cos_b = jnp.cos(base)[0:1, :]
    sin_b = jnp.sin(base)[0:1, :]
    cos_t = cos_in[...] * cos_b - sin_in[...] * sin_b
    sin_t = sin_in[...] * cos_b + cos_in[...] * sin_b
    sin_t = jnp.where(lax.broadcasted_iota(jnp.int32, (CHUNK, RET_QK_DIM), 1) < half, -sin_t, sin_t)
    k_scale = RET_QK_DIM ** -0.5
    cos_k = cos_t * k_scale
    sin_k = sin_t * k_scale

    def head(hd):
        ks = slice(hd * RET_QK_DIM, (hd + 1) * RET_QK_DIM)
        vs = slice(hd * RET_V_DIM, (hd + 1) * RET_V_DIM)
        v_ref, g_ref = (v0_ref, g0_ref) if hd < heads_per_block else (v1_ref, g1_ref)
        bs = slice((hd % heads_per_block) * RET_V_DIM, (hd % heads_per_block + 1) * RET_V_DIM)
        q_h = q_ref[:, ks].astype(F32)
        k_h = k_ref[:, ks].astype(F32)
        q_r = q_h * cos_t + pltpu.roll(q_h, half, 1) * sin_t
        k_r = k_h * cos_k + pltpu.roll(k_h, half, 1) * sin_k
        v_h = v_ref[:, bs]
        scores = _dot_nt(q_r.astype(BF16), k_r.astype(BF16)) * dmask[hd]
        s_prev = state[hd]
        o = _dot(scores.astype(BF16), v_h) + _dot((q_r * xi[hd]).astype(BF16), s_prev.astype(BF16))
        state[hd] = s_prev * float(np.exp(np.float32(CHUNK * log_gammas[hd]))) + _dot_tn(
            (k_r * zeta[hd]).astype(BF16), v_h)
        mu = jnp.mean(o, axis=-1, keepdims=True)
        d = o - mu
        y = d * lax.rsqrt(jnp.mean(d * d, axis=-1, keepdims=True) + EPS) * nw_ref[:, vs]
        y = (y * _silu(g_ref[:, bs].astype(F32))).astype(BF16)
        y_ref[:, SSD_D_INNER + hd * RET_V_DIM:SSD_D_INNER + (hd + 1) * RET_V_DIM] = y

    return head


def _mixer_kernel(z_ref, x_ref, b_ref, c_ref, dtraw_ref, cw_ref, cb_ref, dtb_ref, alog_ref, dskip_ref, snw_ref,
                  q_ref, k_ref, v0_ref, v1_ref, g0_ref, g1_ref, inv_ref, rnw_ref,
                  wout_hbm, h_ref, gate_ref, fw_ref, o_ref,
                  xbuf, shifts, ssd_state, expand, ret_state, cos_in, sin_in, dmask, zeta, xi, ybuf,
                  wout, wstage, wsem, *, n_steps, final):
    step = pl.program_id(0)

    def wout_copy(i, slot):
        return pltpu.make_async_copy(wout_hbm.at[pl.ds(i * W_OUT_ROWS, W_OUT_ROWS), :], wstage.at[slot],
                                     wsem.at[slot])

    @pl.when(step == 0)
    def _():
        n_stage = MIX_WIDTH // W_OUT_ROWS
        wout_copy(0, 0).start()

        def stage(i, carry):
            slot = lax.rem(i, 2)

            @pl.when(i + 1 < n_stage)
            def _():
                wout_copy(i + 1, 1 - slot).start()

            wout_copy(i, slot).wait()
            wout[pl.ds(pl.multiple_of(i * W_OUT_ROWS, W_OUT_ROWS), W_OUT_ROWS), :] = wstage[slot].astype(BF16)
            return carry

        lax.fori_loop(0, n_stage, stage, 0)
        _ssd_init(xbuf, shifts, ssd_state, expand)
        _ret_init(inv_ref, ret_state, cos_in, sin_in, dmask, zeta, xi)
        ybuf[...] = jnp.zeros_like(ybuf)

    slot = lax.rem(step, 2)
    first_chunk = jnp.minimum(step, n_steps - 1) * CHUNKS_PER_STEP
    y_old = ybuf[1 - slot]
    piece_w = D_MODEL // RET_HEADS
    pieces = []
    for sub in range(CHUNKS_PER_STEP):
        rows = pl.ds(sub * CHUNK, CHUNK)
        sub_refs = lambda *refs: [r.at[rows] for r in refs]
        y_new = ybuf.at[slot, rows]
        z_c, x_c, b_c, c_c, dtraw_c = sub_refs(z_ref, x_ref, b_ref, c_ref, dtraw_ref)
        ssd_group = _ssd_chunk(z_c, x_c, b_c, c_c, dtraw_c, cw_ref, cb_ref, dtb_ref, alog_ref, dskip_ref, snw_ref,
                               y_new, xbuf, shifts, ssd_state, expand)
        q_c, k_c, v0_c, v1_c, g0_c, g1_c = sub_refs(q_ref, k_ref, v0_ref, v1_ref, g0_ref, g1_ref)
        ret_head = _ret_chunk(first_chunk + sub, q_c, k_c, v0_c, v1_c, g0_c, g1_c, inv_ref, rnw_ref,
                              y_new, ret_state, cos_in, sin_in, dmask, zeta, xi)
        def between_quads(q):
            if q % CHUNKS_PER_STEP == 0:
                n = len(pieces)
                pieces.append(_dot(y_old, wout[:, n * piece_w:(n + 1) * piece_w]))
            ret_head(q)

        for g in range(SSD_GROUPS):
            ssd_group(g, between_quads)

    h = h_ref[...] + gate_ref[...] * jnp.concatenate(pieces, axis=1)
    if final:
        h = h * lax.rsqrt(jnp.mean(h * h, axis=-1, keepdims=True) + EPS) * fw_ref[...]
    o_ref[...] = h


def _mix_and_project(proj, dt_raw, h, mod, conv_w, conv_b, dt_bias, a_log, d_skip, ssd_norm_w, ret_norm_w,
                     w_out, final_norm_w, final):
    L = h.shape[0]
    rows = CHUNKS_PER_STEP * CHUNK
    n_steps = L // rows
    half = RET_QK_DIM // 2
    inv = ROPE_BASE ** (-jnp.arange(half, dtype=F32) / half)
    inv = jnp.concatenate([inv, inv]).reshape(1, RET_QK_DIM)
    pad = lambda v: jnp.pad(v.reshape(1, SSD_HEADS), ((0, 0), (0, LANES - SSD_HEADS)))
    full = lambda shape: pl.BlockSpec(shape, lambda s: (0, 0))
    mixed = lambda width, col: pl.BlockSpec((rows, width), lambda s: (jnp.minimum(s, n_steps - 1), col))
    projected = lambda width, col: pl.BlockSpec((rows, width), lambda s: (jnp.maximum(s - 1, 0), col))
    table = pltpu.VMEM((RET_HEADS, CHUNK, CHUNK), F32)
    return pl.pallas_call(
        functools.partial(_mixer_kernel, n_steps=n_steps, final=final),
        grid=(n_steps + 1,),
        in_specs=[mixed(SSD_D_INNER, COL_Z), mixed(SSD_D_INNER, COL_X), mixed(SSD_BC, COL_B), mixed(SSD_BC, COL_C),
                  mixed(LANES, 0),
                  full((SSD_CONV, SSD_CONV_DIM)), full((1, SSD_CONV_DIM)), full((1, LANES)), full((1, LANES)),
                  full((1, SSD_D_INNER)), full((1, SSD_D_INNER)),
                  mixed(RET_QK_WIDTH, COL_Q), mixed(RET_QK_WIDTH, COL_K),
                  mixed(RET_QK_WIDTH, COL_V), mixed(RET_QK_WIDTH, COL_V + 1),
                  mixed(RET_QK_WIDTH, COL_G), mixed(RET_QK_WIDTH, COL_G + 1),
                  full((1, RET_QK_DIM)), full((1, RET_V_WIDTH)),
                  pl.BlockSpec(memory_space=pl.ANY),
                  projected(D_MODEL, 0),
                  pl.BlockSpec((1, D_MODEL), lambda s: (0, 2)),
                  full((1, D_MODEL))],
        out_specs=projected(D_MODEL, 0),
        out_shape=jax.ShapeDtypeStruct((L, D_MODEL), F32),
        scratch_shapes=[pltpu.VMEM((CONV_TAIL + CHUNK, SSD_CONV_DIM), BF16),
                        pltpu.VMEM(((SSD_CONV - 1) * CHUNK, CONV_TAIL + CHUNK), BF16),
                        pltpu.VMEM((SSD_STATE, SSD_D_INNER), F32),
                        pltpu.VMEM((LANES, SSD_D_INNER), BF16),
                        pltpu.VMEM((RET_HEADS, RET_QK_DIM, RET_V_DIM), F32),
                        pltpu.VMEM((CHUNK, RET_QK_DIM), F32), pltpu.VMEM((CHUNK, RET_QK_DIM), F32),
                        table, table, table,
                        pltpu.VMEM((2, rows, MIX_WIDTH), BF16),
                        pltpu.VMEM((MIX_WIDTH, D_MODEL), BF16),
                        pltpu.VMEM((2, W_OUT_ROWS, D_MODEL), F32),
                        pltpu.SemaphoreType.DMA((2,))],
        compiler_params=pltpu.CompilerParams(dimension_semantics=("arbitrary",),
                                             vmem_limit_bytes=60 * MIB),
        name="mixers_out_projection",
    )(proj, proj, proj, proj, dt_raw, conv_w, conv_b.reshape(1, SSD_CONV_DIM), pad(dt_bias), pad(a_log),
      jnp.repeat(d_skip, SSD_HEAD_DIM).reshape(1, SSD_D_INNER), ssd_norm_w.reshape(1, SSD_D_INNER),
      proj, proj, proj, proj, proj, proj, inv, ret_norm_w.reshape(1, RET_V_WIDTH),
      w_out, h, mod, final_norm_w.reshape(1, D_MODEL))


def kernel(x, c, w_ada, b_ada, norm_w, w_in, conv_w, conv_b, dt_bias, a_log, d_skip, ssd_norm_w, ret_norm_w,
           w_out, final_norm_w):
    bsz, L, d_model = x.shape
    assert bsz == 1 and d_model == D_MODEL and L % TM_IN == 0 and w_in.shape[-1] == IN_WIDTH
    depth = w_in.shape[0]
    h = x.reshape(L, D_MODEL)
    for layer in range(depth):
        mod = _modulation(c, w_ada[layer], b_ada[layer])
        proj, dt_raw = _in_projection(h, norm_w[layer], mod, jnp.swapaxes(w_in[layer], 0, 1))
        h = _mix_and_project(proj, dt_raw, h, mod, conv_w[layer], conv_b[layer], dt_bias[layer], a_log[layer],
                             d_skip[layer], ssd_norm_w[layer], ret_norm_w[layer], w_out[layer],
                             final_norm_w, final=layer == depth - 1)
    return h.reshape(bsz, L, D_MODEL)
```

```python
import functools
import math

import numpy as np
import jax
import jax.numpy as jnp
from jax import lax
from jax.experimental import pallas as pl
from jax.experimental.pallas import tpu as pltpu

D_MODEL = 2048
SSD_D_INNER = D_MODEL
SSD_HEAD_DIM = 64
SSD_HEADS = SSD_D_INNER // SSD_HEAD_DIM
HEAD_DIM_LOG2 = SSD_HEAD_DIM.bit_length() - 1
SSD_GROUPS = 4
SSD_STATE = 128
SSD_CONV = 4
SSD_BC = SSD_GROUPS * SSD_STATE
SSD_CONV_DIM = SSD_D_INNER + 2 * SSD_BC
RET_HEADS = 8
RET_QK_DIM = 128
RET_V_DIM = 256
RET_QK_WIDTH = RET_HEADS * RET_QK_DIM
RET_V_WIDTH = RET_HEADS * RET_V_DIM
MIX_WIDTH = SSD_D_INNER + RET_V_WIDTH
CHUNK = 128
ROPE_BASE = 10000.0
EPS = 1e-6
LOG2_E = math.log2(math.e)

OFF_Z = SSD_D_INNER
OFF_XBC = OFF_Z + SSD_CONV_DIM
OFF_DT = OFF_XBC + SSD_HEADS
OFF_Q = OFF_DT + RET_QK_WIDTH
OFF_K = OFF_Q + RET_QK_WIDTH
OFF_V = OFF_K + RET_V_WIDTH
IN_WIDTH = OFF_V + RET_V_WIDTH

PROJ_WIDTH = IN_WIDTH - SSD_HEADS
COL_Z, COL_X = 0, 1
COL_B, COL_C = 8, 9
COL_Q, COL_K, COL_V, COL_G = 5, 6, 7, 9

LANES = 128
SUBLANES = 8
VREG_ROWS = CHUNK // SUBLANES
CONV_TAIL = (SSD_CONV - 1) * SUBLANES
CHUNKS_PER_STEP = 2
W_OUT_ROWS = 64
QUAD = 4
QUAD_W = QUAD * SSD_HEAD_DIM
HEADS_PER_GROUP = SSD_HEADS // SSD_GROUPS
GROUP_W = SSD_D_INNER // SSD_GROUPS

TM_IN, TN_IN = 1024, 1024
N_TILES_BEFORE_DT, _rem = divmod(OFF_XBC, TN_IN)
assert _rem == 0 and PROJ_WIDTH % TN_IN == 0
MIB = 1024 * 1024

F32 = jnp.float32
BF16 = jnp.bfloat16


def _silu(v):
    return v / (1.0 + jnp.exp(-v))


def _dot(a, b):
    return jnp.dot(a, b, preferred_element_type=F32)


def _dot_nt(a, b):
    return lax.dot_general(a, b, (((1,), (1,)), ((), ())), preferred_element_type=F32)


def _dot_tn(a, b):
    return lax.dot_general(a, b, (((0,), (0,)), ((), ())), preferred_element_type=F32)


def _time_iota(shape, dim):
    p = lax.broadcasted_iota(jnp.int32, shape, dim)
    return jnp.right_shift(p, 3) + VREG_ROWS * jnp.bitwise_and(p, SUBLANES - 1)


def _to_time_permuted(v):
    n, width = v.shape[0] // CHUNK, v.shape[1]
    return jnp.swapaxes(v.reshape(n, SUBLANES, VREG_ROWS, width), 1, 2).reshape(n * CHUNK, width)


def _from_time_permuted(v):
    n, width = v.shape[0] // CHUNK, v.shape[1]
    return jnp.swapaxes(v.reshape(n, VREG_ROWS, SUBLANES, width), 1, 2).reshape(n * CHUNK, width)


def _split3(v):
    hi = v.astype(BF16)
    r1 = v - hi.astype(F32)
    mid = r1.astype(BF16)
    lo = (r1 - mid.astype(F32)).astype(BF16)
    return hi, mid, lo


def _mod_kernel(c_ref, w_ref, b_ref, o_ref):
    cond = jnp.broadcast_to(_silu(c_ref[...]), (SUBLANES, D_MODEL))
    o_ref[...] = _dot(cond, w_ref[...])[0:1, :] + b_ref[...]


def _modulation(c, w_ada, b_ada):
    n = w_ada.shape[1]
    tn = 512
    return pl.pallas_call(
        _mod_kernel,
        grid=(n // tn,),
        in_specs=[pl.BlockSpec((1, D_MODEL), lambda j: (0, 0)),
                  pl.BlockSpec((D_MODEL, tn), lambda j: (0, j)),
                  pl.BlockSpec((1, tn), lambda j: (0, j))],
        out_specs=pl.BlockSpec((1, tn), lambda j: (0, j)),
        out_shape=jax.ShapeDtypeStruct((1, n), F32),
        compiler_params=pltpu.CompilerParams(dimension_semantics=("arbitrary",),
                                             vmem_limit_bytes=24 * MIB),
        name="adaln_modulation",
    )(c, w_ada, b_ada.reshape(1, n))


def _inproj_kernel(x_ref, nw_ref, shift_ref, scale_ref, wa_ref, wb_ref, wd_ref, o_ref, dt_ref, u_ref):
    j = pl.program_id(1)

    @pl.when(j == 0)
    def _():
        x = _to_time_permuted(x_ref[...])
        y = x * lax.rsqrt(jnp.mean(x * x, axis=-1, keepdims=True) + EPS) * nw_ref[...]
        u = (y * (1.0 + scale_ref[...]) + shift_ref[...]).astype(BF16)
        u_ref[...] = u
        is_dt = lax.broadcasted_iota(jnp.int32, (LANES, D_MODEL), 0) < SSD_HEADS
        dt_ref[...] = _dot_nt(u, jnp.where(is_dt, wd_ref[...], 0.0).astype(BF16))

    after_dt = j >= N_TILES_BEFORE_DT
    skip = pl.multiple_of(jnp.where(after_dt, SSD_HEADS, 0), SSD_HEADS)
    head = wa_ref[pl.ds(skip, TN_IN - SSD_HEADS), :]
    tail = jnp.where(after_dt, wb_ref[...], wa_ref[TN_IN - SSD_HEADS:, :])
    w = jnp.concatenate([head, tail], axis=0).astype(BF16)
    o_ref[...] = _dot_nt(u_ref[...], w).astype(BF16)


def _in_projection(h, norm_w, mod, w_in_t):
    L = h.shape[0]
    row = lambda i, j: (0, 0)
    return pl.pallas_call(
        _inproj_kernel,
        grid=(L // TM_IN, PROJ_WIDTH // TN_IN),
        in_specs=[pl.BlockSpec((TM_IN, D_MODEL), lambda i, j: (i, 0)),
                  pl.BlockSpec((1, D_MODEL), row),
                  pl.BlockSpec((1, D_MODEL), lambda i, j: (0, 0)),
                  pl.BlockSpec((1, D_MODEL), lambda i, j: (0, 1)),
                  pl.BlockSpec((TN_IN, D_MODEL), lambda i, j: (j, 0)),
                  pl.BlockSpec((SSD_HEADS, D_MODEL), lambda i, j: ((j + 1) * (TN_IN // SSD_HEADS), 0)),
                  pl.BlockSpec((LANES, D_MODEL), lambda i, j: (OFF_XBC // LANES, 0))],
        out_specs=[pl.BlockSpec((TM_IN, TN_IN), lambda i, j: (i, j)),
                   pl.BlockSpec((TM_IN, LANES), lambda i, j: (i, 0))],
        out_shape=[jax.ShapeDtypeStruct((L, PROJ_WIDTH), BF16),
                   jax.ShapeDtypeStruct((L, LANES), F32)],
        scratch_shapes=[pltpu.VMEM((TM_IN, D_MODEL), BF16)],
        compiler_params=pltpu.CompilerParams(dimension_semantics=("arbitrary", "arbitrary"),
                                             vmem_limit_bytes=56 * MIB),
        name="adaln_in_projection",
    )(h, norm_w.reshape(1, D_MODEL), mod, mod, w_in_t, w_in_t, w_in_t)


def _ssd_init(tail, state, expand):
    tail[...] = jnp.zeros_like(tail)
    state[...] = jnp.zeros_like(state)
    head_of_lane = jnp.right_shift(lax.broadcasted_iota(jnp.int32, (LANES, SSD_D_INNER), 1), HEAD_DIM_LOG2)
    row = lax.broadcasted_iota(jnp.int32, (LANES, SSD_D_INNER), 0)
    expand[...] = jnp.where(head_of_lane == row, 1.0, 0.0).astype(BF16)


def _ssd_chunk(z_ref, x_ref, b_ref, c_ref, dtraw_ref, cw_ref, cb_ref, dtb_ref, alog_ref, dskip_ref, nw_ref,
               y_ref, tail, state, expand):
    int_iota = lambda shape, dim: lax.broadcasted_iota(jnp.int32, shape, dim)

    cur = jnp.concatenate([x_ref[...], b_ref[...], c_ref[...]], axis=1).astype(F32)
    first_sublane = int_iota((SUBLANES, SSD_CONV_DIM), 0) == 0
    wrapped = [jnp.where(first_sublane,
                         pltpu.roll(tail[k * SUBLANES:(k + 1) * SUBLANES, :], 1, 0),
                         pltpu.roll(cur[CHUNK - CONV_TAIL + k * SUBLANES:CHUNK - CONV_TAIL + (k + 1) * SUBLANES, :], 1, 0))
               for k in range(SSD_CONV - 1)]
    acc = cb_ref[...] + cw_ref[SSD_CONV - 1:SSD_CONV, :] * cur
    for d in range(1, SSD_CONV):
        delayed = jnp.concatenate(wrapped[SSD_CONV - 1 - d:] + [cur[0:CHUNK - d * SUBLANES, :]], axis=0)
        acc = acc + cw_ref[SSD_CONV - 1 - d:SSD_CONV - d, :] * delayed
    tail[...] = cur[CHUNK - CONV_TAIL:, :]
    xbc = _silu(acc)
    xs = xbc[:, 0:SSD_D_INNER]
    xs_b = xs.astype(BF16)
    bmat = xbc[:, SSD_D_INNER:SSD_D_INNER + SSD_BC]
    cmat = xbc[:, SSD_D_INNER + SSD_BC:]

    pre = dtraw_ref[...] + dtb_ref[...]
    dt = jnp.maximum(pre, 0.0) + jnp.log1p(jnp.exp(-jnp.abs(pre)))
    d_a = dt * (-LOG2_E * jnp.exp(alog_ref[...]))
    causal = _time_iota((CHUNK, CHUNK), 0) >= _time_iota((CHUNK, CHUNK), 1)
    tril = jnp.where(causal, 1.0, 0.0).astype(BF16)
    acs = sum(_dot(tril, part) for part in _split3(d_a))
    acs_t = acs.T
    src_t = acs_t - jnp.log2(dt).T
    w_t = jnp.exp2(acs_t[:, CHUNK - 1:CHUNK] - src_t)
    end = jnp.broadcast_to(jnp.exp2(acs[CHUNK - 1:CHUNK, :]), (SUBLANES, LANES))
    end_row = sum(_dot(part, expand[...]) for part in _split3(end))[0:1, :]

    lane_blk = jnp.right_shift(int_iota((CHUNK, QUAD_W), 1), HEAD_DIM_LOG2)

    def block_diag(v):
        return jnp.concatenate([jnp.where(lane_blk == j, v, jnp.zeros_like(v)) for j in range(QUAD)], axis=0)

    def group(g, between_quads=lambda q: None):
        gs = slice(g * SSD_STATE, (g + 1) * SSD_STATE)
        b_g = bmat[:, gs]
        c_g = cmat[:, gs]
        cb_g = _dot_nt(c_g.astype(BF16), b_g.astype(BF16))
        b_gt = b_g.T
        y_parts = []
        for qi in range(HEADS_PER_GROUP // QUAD):
            q = g * (HEADS_PER_GROUP // QUAD) + qi
            between_quads(q)
            qs = slice(q * QUAD_W, (q + 1) * QUAD_W)
            lhs_diag, lhs_state, lhs_off = [], [], []
            for j in range(QUAD):
                hd = q * QUAD + j
                col = jnp.broadcast_to(acs[:, hd:hd + 1], (CHUNK, CHUNK))
                decay_dt = jnp.exp2(jnp.where(causal, col - src_t[hd:hd + 1, :], -jnp.inf))
                lhs_diag.append((cb_g * decay_dt).astype(BF16))
                lhs_state.append((b_gt * w_t[hd:hd + 1, :]).astype(BF16))
                lhs_off.append((c_g * jnp.exp2(col)).astype(BF16))
            x_blk = block_diag(xs_b[:, qs])
            s_prev = state[:, qs]
            s_blk = block_diag(s_prev.astype(BF16))
            y_q = _dot(jnp.concatenate(lhs_diag, axis=1), x_blk)
            y_q = y_q + _dot(jnp.concatenate(lhs_off, axis=1), s_blk)
            state[:, qs] = s_prev * end_row[:, qs] + _dot(jnp.concatenate(lhs_state, axis=1), x_blk)
            y_parts.append(y_q + xs[:, qs] * dskip_ref[:, qs])
        ws = slice(g * GROUP_W, (g + 1) * GROUP_W)
        u = jnp.concatenate(y_parts, axis=1) * _silu(z_ref[:, ws].astype(F32))
        u = u * lax.rsqrt(jnp.mean(u * u, axis=-1, keepdims=True) + EPS)
        y_ref[:, ws] = (u * nw_ref[:, ws]).astype(BF16)

    return group


def _log_gammas():
    return [float(np.log1p(-np.exp2(np.float32(-5.0 - hd)), dtype=np.float32)) for hd in range(RET_HEADS)]


def _ret_init(inv_ref, state, cos_in, sin_in, dmask, zeta, xi):
    state[...] = jnp.zeros_like(state)
    ri = _time_iota((CHUNK, CHUNK), 0).astype(F32)
    ci = _time_iota((CHUNK, CHUNK), 1).astype(F32)
    ang = ri * inv_ref[...]
    cos_in[...] = jnp.cos(ang)
    sin_in[...] = jnp.sin(ang)
    for hd, lg in enumerate(_log_gammas()):
        dmask[hd] = jnp.where(ri >= ci, jnp.exp((ri - ci) * lg), 0.0)
        zeta[hd] = jnp.exp((CHUNK - 1.0 - ri) * lg)
        xi[hd] = jnp.exp((ri + 1.0) * lg)


def _ret_chunk(chunk, q_ref, k_ref, v0_ref, v1_ref, g0_ref, g1_ref, inv_ref, nw_ref, y_ref, state, cos_in, sin_in,
               dmask, zeta, xi):
    heads_per_block = RET_QK_WIDTH // RET_V_DIM
    log_gammas = _log_gammas()
    half = RET_QK_DIM // 2

    base = jnp.broadcast_to((chunk * CHUNK).astype(F32) * inv_ref[...], (SUBLANES, RET_QK_DIM))
    cos_b = jnp.cos(base)[0:1, :]
    sin_b = jnp.sin(base)[0:1, :]
    cos_t = cos_in[...] * cos_b - sin_in[...] * sin_b
    sin_t = sin_in[...] * cos_b + cos_in[...] * sin_b
    sin_t = jnp.where(lax.broadcasted_iota(jnp.int32, (CHUNK, RET_QK_DIM), 1) < half, -sin_t, sin_t)
    k_scale = RET_QK_DIM ** -0.5
    cos_k = cos_t * k_scale
    sin_k = sin_t * k_scale

    def head(hd):
        ks = slice(hd * RET_QK_DIM, (hd + 1) * RET_QK_DIM)
        vs = slice(hd * RET_V_DIM, (hd + 1) * RET_V_DIM)
        v_ref, g_ref = (v0_ref, g0_ref) if hd < heads_per_block else (v1_ref, g1_ref)
        bs = slice((hd % heads_per_block) * RET_V_DIM, (hd % heads_per_block + 1) * RET_V_DIM)
        q_h = q_ref[:, ks].astype(F32)
        k_h = k_ref[:, ks].astype(F32)
        q_r = q_h * cos_t + pltpu.roll(q_h, half, 1) * sin_t
        k_r = k_h * cos_k + pltpu.roll(k_h, half, 1) * sin_k
        v_h = v_ref[:, bs]
        scores = _dot_nt(q_r.astype(BF16), k_r.astype(BF16)) * dmask[hd]
        s_prev = state[hd]
        o = _dot(scores.astype(BF16), v_h) + _dot((q_r * xi[hd]).astype(BF16), s_prev.astype(BF16))
        state[hd] = s_prev * float(np.exp(np.float32(CHUNK * log_gammas[hd]))) + _dot_tn(
            (k_r * zeta[hd]).astype(BF16), v_h)
        mu = jnp.mean(o, axis=-1, keepdims=True)
        d = o - mu
        y = d * lax.rsqrt(jnp.mean(d * d, axis=-1, keepdims=True) + EPS) * nw_ref[:, vs]
        y = (y * _silu(g_ref[:, bs].astype(F32))).astype(BF16)
        y_ref[:, SSD_D_INNER + hd * RET_V_DIM:SSD_D_INNER + (hd + 1) * RET_V_DIM] = y

    return head


def _mixer_kernel(z_ref, x_ref, b_ref, c_ref, dtraw_ref, cw_ref, cb_ref, dtb_ref, alog_ref, dskip_ref, snw_ref,
                  q_ref, k_ref, v0_ref, v1_ref, g0_ref, g1_ref, inv_ref, rnw_ref,
                  wout_hbm, h_ref, gate_ref, fw_ref, o_ref,
                  tail, ssd_state, expand, ret_state, cos_in, sin_in, dmask, zeta, xi, ybuf,
                  wout, wstage, wsem, *, n_steps, final):
    step = pl.program_id(0)

    def wout_copy(i, slot):
        return pltpu.make_async_copy(wout_hbm.at[pl.ds(i * W_OUT_ROWS, W_OUT_ROWS), :], wstage.at[slot],
                                     wsem.at[slot])

    @pl.when(step == 0)
    def _():
        n_stage = MIX_WIDTH // W_OUT_ROWS
        wout_copy(0, 0).start()

        def stage(i, carry):
            slot = lax.rem(i, 2)

            @pl.when(i + 1 < n_stage)
            def _():
                wout_copy(i + 1, 1 - slot).start()

            wout_copy(i, slot).wait()
            wout[pl.ds(pl.multiple_of(i * W_OUT_ROWS, W_OUT_ROWS), W_OUT_ROWS), :] = wstage[slot].astype(BF16)
            return carry

        lax.fori_loop(0, n_stage, stage, 0)
        _ssd_init(tail, ssd_state, expand)
        _ret_init(inv_ref, ret_state, cos_in, sin_in, dmask, zeta, xi)
        ybuf[...] = jnp.zeros_like(ybuf)

    slot = lax.rem(step, 2)
    first_chunk = jnp.minimum(step, n_steps - 1) * CHUNKS_PER_STEP
    y_old = ybuf[1 - slot]
    piece_w = D_MODEL // RET_HEADS
    pieces = []
    mixers = []
    for sub in range(CHUNKS_PER_STEP):
        rows = pl.ds(sub * CHUNK, CHUNK)
        sub_refs = lambda *refs: [r.at[rows] for r in refs]
        y_new = ybuf.at[slot, rows]
        z_c, x_c, b_c, c_c, dtraw_c = sub_refs(z_ref, x_ref, b_ref, c_ref, dtraw_ref)
        ssd_group = _ssd_chunk(z_c, x_c, b_c, c_c, dtraw_c, cw_ref, cb_ref, dtb_ref, alog_ref, dskip_ref, snw_ref,
                               y_new, tail, ssd_state, expand)
        q_c, k_c, v0_c, v1_c, g0_c, g1_c = sub_refs(q_ref, k_ref, v0_ref, v1_ref, g0_ref, g1_ref)
        ret_head = _ret_chunk(first_chunk + sub, q_c, k_c, v0_c, v1_c, g0_c, g1_c, inv_ref, rnw_ref,
                              y_new, ret_state, cos_in, sin_in, dmask, zeta, xi)
        mixers.append((ssd_group, ret_head))

    for ssd_group, ret_head in mixers:
        def between_quads(q, ret_head=ret_head):
            if q % CHUNKS_PER_STEP == 0:
                n = len(pieces)
                pieces.append(_dot(y_old, wout[:, n * piece_w:(n + 1) * piece_w]))
            ret_head(q)

        for g in range(SSD_GROUPS):
            ssd_group(g, between_quads)

    h = h_ref[...] + gate_ref[...] * _from_time_permuted(jnp.concatenate(pieces, axis=1))
    if final:
        h = h * lax.rsqrt(jnp.mean(h * h, axis=-1, keepdims=True) + EPS) * fw_ref[...]
    o_ref[...] = h


def _mix_and_project(proj, dt_raw, h, mod, conv_w, conv_b, dt_bias, a_log, d_skip, ssd_norm_w, ret_norm_w,
                     w_out, final_norm_w, final):
    L = h.shape[0]
    rows = CHUNKS_PER_STEP * CHUNK
    n_steps = L // rows
    half = RET_QK_DIM // 2
    inv = ROPE_BASE ** (-jnp.arange(half, dtype=F32) / half)
    inv = jnp.concatenate([inv, inv]).reshape(1, RET_QK_DIM)
    pad = lambda v: jnp.pad(v.reshape(1, SSD_HEADS), ((0, 0), (0, LANES - SSD_HEADS)))
    full = lambda shape: pl.BlockSpec(shape, lambda s: (0, 0))
    mixed = lambda width, col: pl.BlockSpec((rows, width), lambda s: (jnp.minimum(s, n_steps - 1), col))
    projected = lambda width, col: pl.BlockSpec((rows, width), lambda s: (jnp.maximum(s - 1, 0), col))
    table = pltpu.VMEM((RET_HEADS, CHUNK, CHUNK), F32)
    return pl.pallas_call(
        functools.partial(_mixer_kernel, n_steps=n_steps, final=final),
        grid=(n_steps + 1,),
        in_specs=[mixed(SSD_D_INNER, COL_Z), mixed(SSD_D_INNER, COL_X), mixed(SSD_BC, COL_B), mixed(SSD_BC, COL_C),
                  mixed(LANES, 0),
                  full((SSD_CONV, SSD_CONV_DIM)), full((1, SSD_CONV_DIM)), full((1, LANES)), full((1, LANES)),
                  full((1, SSD_D_INNER)), full((1, SSD_D_INNER)),
                  mixed(RET_QK_WIDTH, COL_Q), mixed(RET_QK_WIDTH, COL_K),
                  mixed(RET_QK_WIDTH, COL_V), mixed(RET_QK_WIDTH, COL_V + 1),
                  mixed(RET_QK_WIDTH, COL_G), mixed(RET_QK_WIDTH, COL_G + 1),
                  full((1, RET_QK_DIM)), full((1, RET_V_WIDTH)),
                  pl.BlockSpec(memory_space=pl.ANY),
                  projected(D_MODEL, 0),
                  pl.BlockSpec((1, D_MODEL), lambda s: (0, 2)),
                  full((1, D_MODEL))],
        out_specs=projected(D_MODEL, 0),
        out_shape=jax.ShapeDtypeStruct((L, D_MODEL), F32),
        scratch_shapes=[pltpu.VMEM((CONV_TAIL, SSD_CONV_DIM), F32),
                        pltpu.VMEM((SSD_STATE, SSD_D_INNER), F32),
                        pltpu.VMEM((LANES, SSD_D_INNER), BF16),
                        pltpu.VMEM((RET_HEADS, RET_QK_DIM, RET_V_DIM), F32),
                        pltpu.VMEM((CHUNK, RET_QK_DIM), F32), pltpu.VMEM((CHUNK, RET_QK_DIM), F32),
                        table, table, table,
                        pltpu.VMEM((2, rows, MIX_WIDTH), BF16),
                        pltpu.VMEM((MIX_WIDTH, D_MODEL), BF16),
                        pltpu.VMEM((2, W_OUT_ROWS, D_MODEL), F32),
                        pltpu.SemaphoreType.DMA((2,))],
        compiler_params=pltpu.CompilerParams(dimension_semantics=("arbitrary",),
                                             vmem_limit_bytes=60 * MIB),
        name="mixers_out_projection",
    )(proj, proj, proj, proj, dt_raw, conv_w, conv_b.reshape(1, SSD_CONV_DIM), pad(dt_bias), pad(a_log),
      jnp.repeat(d_skip, SSD_HEAD_DIM).reshape(1, SSD_D_INNER), ssd_norm_w.reshape(1, SSD_D_INNER),
      proj, proj, proj, proj, proj, proj, inv, ret_norm_w.reshape(1, RET_V_WIDTH),
      w_out, h, mod, final_norm_w.reshape(1, D_MODEL))


def kernel(x, c, w_ada, b_ada, norm_w, w_in, conv_w, conv_b, dt_bias, a_log, d_skip, ssd_norm_w, ret_norm_w,
           w_out, final_norm_w):
    bsz, L, d_model = x.shape
    assert bsz == 1 and d_model == D_MODEL and L % TM_IN == 0 and w_in.shape[-1] == IN_WIDTH
    depth = w_in.shape[0]
    h = x.reshape(L, D_MODEL)
    for layer in range(depth):
        mod = _modulation(c, w_ada[layer], b_ada[layer])
        proj, dt_raw = _in_projection(h, norm_w[layer], mod, jnp.swapaxes(w_in[layer], 0, 1))
        h = _mix_and_project(proj, dt_raw, h, mod, conv_w[layer], conv_b[layer], dt_bias[layer], a_log[layer],
                             d_skip[layer], ssd_norm_w[layer], ret_norm_w[layer], w_out[layer],
                             final_norm_w, final=layer == depth - 1)
    return h.reshape(bsz, L, D_MODEL)
```

```python
import functools
import math

import numpy as np
import jax
import jax.numpy as jnp
from jax import lax
from jax.experimental import pallas as pl
from jax.experimental.pallas import tpu as pltpu

D_MODEL = 2048
SSD_D_INNER = D_MODEL
SSD_HEAD_DIM = 64
SSD_HEADS = SSD_D_INNER // SSD_HEAD_DIM
HEAD_DIM_LOG2 = SSD_HEAD_DIM.bit_length() - 1
SSD_GROUPS = 4
SSD_STATE = 128
SSD_CONV = 4
SSD_BC = SSD_GROUPS * SSD_STATE
SSD_CONV_DIM = SSD_D_INNER + 2 * SSD_BC
RET_HEADS = 8
RET_QK_DIM = 128
RET_V_DIM = 256
RET_QK_WIDTH = RET_HEADS * RET_QK_DIM
RET_V_WIDTH = RET_HEADS * RET_V_DIM
MIX_WIDTH = SSD_D_INNER + RET_V_WIDTH
CHUNK = 128
ROPE_BASE = 10000.0
EPS = 1e-6
LOG2_E = math.log2(math.e)

OFF_Z = SSD_D_INNER
OFF_XBC = OFF_Z + SSD_CONV_DIM
OFF_DT = OFF_XBC + SSD_HEADS
OFF_Q = OFF_DT + RET_QK_WIDTH
OFF_K = OFF_Q + RET_QK_WIDTH
OFF_V = OFF_K + RET_V_WIDTH
IN_WIDTH = OFF_V + RET_V_WIDTH

PROJ_WIDTH = IN_WIDTH - SSD_HEADS
COL_Z, COL_X = 0, 1
COL_B, COL_C = 8, 9
COL_Q, COL_K, COL_V, COL_G = 5, 6, 7, 9

LANES = 128
SUBLANES = 8
QUAD = 4
QUAD_W = QUAD * SSD_HEAD_DIM
HEADS_PER_GROUP = SSD_HEADS // SSD_GROUPS
GROUP_W = SSD_D_INNER // SSD_GROUPS
CONV_TAIL = 16
CHUNKS_PER_STEP = 2
W_OUT_ROWS = 256

TM_IN, TN_IN = 1024, 1024
TILE_CONV0, TILE_ROT0, TILE_V0, TILE_G0 = (c // TN_IN for c in (OFF_Z, OFF_XBC, OFF_XBC + 2 * RET_QK_WIDTH,
                                                                 OFF_XBC + 2 * RET_QK_WIDTH + RET_V_WIDTH))
N_COL_TILES = PROJ_WIDTH // TN_IN
assert OFF_Z % TN_IN == 0 and OFF_XBC % TN_IN == 0 and RET_QK_WIDTH == TN_IN and RET_V_WIDTH % TN_IN == 0
assert PROJ_WIDTH % TN_IN == 0 and TM_IN % CHUNK == 0
MIB = 1024 * 1024

F32 = jnp.float32
BF16 = jnp.bfloat16


def _silu(v):
    return v / (1.0 + jnp.exp(-v))


def _dot(a, b):
    return jnp.dot(a, b, preferred_element_type=F32)


def _dot_nt(a, b):
    return lax.dot_general(a, b, (((1,), (1,)), ((), ())), preferred_element_type=F32)


def _dot_tn(a, b):
    return lax.dot_general(a, b, (((0,), (0,)), ((), ())), preferred_element_type=F32)


def _split3(v):
    hi = v.astype(BF16)
    r1 = v - hi.astype(F32)
    mid = r1.astype(BF16)
    lo = (r1 - mid.astype(F32)).astype(BF16)
    return hi, mid, lo


def _log_gammas():
    return [float(np.log1p(-np.exp2(np.float32(-5.0 - hd)), dtype=np.float32)) for hd in range(RET_HEADS)]


def _mod_kernel(c_ref, w_ref, b_ref, o_ref):
    cond = jnp.broadcast_to(_silu(c_ref[...]), (SUBLANES, D_MODEL))
    o_ref[...] = _dot(cond, w_ref[...])[0:1, :] + b_ref[...]


def _modulation(c, w_ada, b_ada):
    n = w_ada.shape[1]
    tn = 512
    return pl.pallas_call(
        _mod_kernel,
        grid=(n // tn,),
        in_specs=[pl.BlockSpec((1, D_MODEL), lambda j: (0, 0)),
                  pl.BlockSpec((D_MODEL, tn), lambda j: (0, j)),
                  pl.BlockSpec((1, tn), lambda j: (0, j))],
        out_specs=pl.BlockSpec((1, tn), lambda j: (0, j)),
        out_shape=jax.ShapeDtypeStruct((1, n), F32),
        compiler_params=pltpu.CompilerParams(dimension_semantics=("arbitrary",),
                                             vmem_limit_bytes=24 * MIB),
        name="adaln_modulation",
    )(c, w_ada, b_ada.reshape(1, n))


def _inproj_kernel(x_ref, nw_ref, shift_ref, scale_ref, wa_ref, wb_ref, wd_ref, inv_ref,
                   o_ref, dt_ref, u_ref, prev, cos_in, sin_in, cos_t, sin_t, decay, *, n_tiles):
    t = pl.program_id(0)
    tile = jnp.minimum(t, n_tiles - 1)
    i = tile // N_COL_TILES
    j = tile - i * N_COL_TILES
    je = lax.rem(jnp.maximum(t - 1, 0), N_COL_TILES)
    half = RET_QK_DIM // 2

    @pl.when(t == 0)
    def _():
        prev[...] = jnp.zeros_like(prev)
        ang = lax.broadcasted_iota(jnp.int32, (TM_IN, RET_QK_DIM), 0).astype(F32) * inv_ref[...]
        cos_in[...] = jnp.cos(ang)
        sin_in[...] = jnp.sin(ang)
        pos = lax.broadcasted_iota(jnp.int32, (CHUNK, RET_QK_DIM), 0).astype(F32) + 1.0
        for hd, lg in enumerate(_log_gammas()):
            decay[0, hd] = jnp.exp(pos * lg)
            decay[1, hd] = jnp.exp(pos * -lg) * RET_QK_DIM ** -0.5

    @pl.when(jnp.logical_and(j == 0, t < n_tiles))
    def _():
        x = x_ref[...]
        y = x * lax.rsqrt(jnp.mean(x * x, axis=-1, keepdims=True) + EPS) * nw_ref[...]
        u = (y * (1.0 + scale_ref[...]) + shift_ref[...]).astype(BF16)
        u_ref[...] = u
        is_dt = lax.broadcasted_iota(jnp.int32, (LANES, D_MODEL), 0) < SSD_HEADS
        dt_ref[...] = _dot_nt(u, jnp.where(is_dt, wd_ref[...], 0.0).astype(BF16))

        base = jnp.broadcast_to((i * TM_IN).astype(F32) * inv_ref[...], (SUBLANES, RET_QK_DIM))
        cos_b = jnp.cos(base)[0:1, :]
        sin_b = jnp.sin(base)[0:1, :]
        cos_t[...] = cos_in[...] * cos_b - sin_in[...] * sin_b
        sin_pos = sin_in[...] * cos_b + cos_in[...] * sin_b
        first_half = lax.broadcasted_iota(jnp.int32, (TM_IN, RET_QK_DIM), 1) < half
        sin_t[...] = jnp.where(first_half, -sin_pos, sin_pos)

    def project():
        after_dt = j >= TILE_ROT0
        skip = pl.multiple_of(jnp.where(after_dt, SSD_HEADS, 0), SSD_HEADS)
        head = wa_ref[pl.ds(skip, TN_IN - SSD_HEADS), :]
        tail = jnp.where(after_dt, wb_ref[...], wa_ref[TN_IN - SSD_HEADS:, :])
        w = jnp.concatenate([head, tail], axis=0).astype(BF16)
        return _dot_nt(u_ref[...], w)

    @pl.when(jnp.logical_or(je < TILE_CONV0, je >= TILE_G0))
    def _():
        acc = project()
        o_ref[...] = _silu(prev[...]).astype(BF16)
        prev[...] = acc

    @pl.when(jnp.logical_or(jnp.logical_and(je >= TILE_CONV0, je < TILE_ROT0),
                            jnp.logical_and(je >= TILE_V0, je < TILE_G0)))
    def _():
        acc = project()
        o_ref[...] = prev[...].astype(BF16)
        prev[...] = acc

    @pl.when(jnp.logical_and(je >= TILE_ROT0, je < TILE_V0))
    def _():
        acc = project()
        side = je - TILE_ROT0
        for hd in range(RET_HEADS):
            cols = slice(hd * RET_QK_DIM, (hd + 1) * RET_QK_DIM)
            scale = decay[side, hd]
            for blk in range(TM_IN // CHUNK):
                rows = slice(blk * CHUNK, (blk + 1) * CHUNK)
                a = prev[rows, cols]
                rot = a * cos_t[rows, :] + pltpu.roll(a, half, 1) * sin_t[rows, :]
                o_ref[rows, cols] = (rot * scale).astype(BF16)
        prev[...] = acc


def _in_projection(h, norm_w, mod, w_in_t, inv):
    L = h.shape[0]
    n_tiles = (L // TM_IN) * N_COL_TILES
    const = lambda t: (0, 0)
    mul_row = lambda t: jnp.minimum(t, n_tiles - 1) // N_COL_TILES
    mul_col = lambda t: lax.rem(jnp.minimum(t, n_tiles - 1), N_COL_TILES)
    epi = lambda t: jnp.maximum(t - 1, 0)
    table = pltpu.VMEM((TM_IN, RET_QK_DIM), F32)
    return pl.pallas_call(
        functools.partial(_inproj_kernel, n_tiles=n_tiles),
        grid=(n_tiles + 1,),
        in_specs=[pl.BlockSpec((TM_IN, D_MODEL), lambda t: (mul_row(t), 0)),
                  pl.BlockSpec((1, D_MODEL), const),
                  pl.BlockSpec((1, D_MODEL), lambda t: (0, 0)),
                  pl.BlockSpec((1, D_MODEL), lambda t: (0, 1)),
                  pl.BlockSpec((TN_IN, D_MODEL), lambda t: (mul_col(t), 0)),
                  pl.BlockSpec((SSD_HEADS, D_MODEL), lambda t: ((mul_col(t) + 1) * (TN_IN // SSD_HEADS), 0)),
                  pl.BlockSpec((LANES, D_MODEL), lambda t: (OFF_XBC // LANES, 0)),
                  pl.BlockSpec((1, RET_QK_DIM), const)],
        out_specs=[pl.BlockSpec((TM_IN, TN_IN), lambda t: (epi(t) // N_COL_TILES, lax.rem(epi(t), N_COL_TILES))),
                   pl.BlockSpec((TM_IN, LANES), lambda t: (mul_row(t), 0))],
        out_shape=[jax.ShapeDtypeStruct((L, PROJ_WIDTH), BF16),
                   jax.ShapeDtypeStruct((L, LANES), F32)],
        scratch_shapes=[pltpu.VMEM((TM_IN, D_MODEL), BF16),
                        pltpu.VMEM((TM_IN, TN_IN), F32),
                        table, table, table, table,
                        pltpu.VMEM((2, RET_HEADS, CHUNK, RET_QK_DIM), F32)],
        compiler_params=pltpu.CompilerParams(dimension_semantics=("arbitrary",),
                                             vmem_limit_bytes=60 * MIB),
        name="adaln_in_projection",
    )(h, norm_w.reshape(1, D_MODEL), mod, mod, w_in_t, w_in_t, w_in_t, inv)


def _ssd_init(xbuf, shifts, state, expand):
    xbuf[0:CONV_TAIL, :] = jnp.zeros((CONV_TAIL, SSD_CONV_DIM), BF16)
    row = lax.broadcasted_iota(jnp.int32, shifts.shape, 0)
    col = lax.broadcasted_iota(jnp.int32, shifts.shape, 1)
    delay = jnp.right_shift(row, CHUNK.bit_length() - 1) + 1
    t = jnp.bitwise_and(row, CHUNK - 1)
    shifts[...] = jnp.where(col == CONV_TAIL + t - delay, 1.0, 0.0).astype(BF16)
    state[...] = jnp.zeros_like(state)
    head_of_lane = jnp.right_shift(lax.broadcasted_iota(jnp.int32, (LANES, SSD_D_INNER), 1), HEAD_DIM_LOG2)
    row = lax.broadcasted_iota(jnp.int32, (LANES, SSD_D_INNER), 0)
    expand[...] = jnp.where(head_of_lane == row, 1.0, 0.0).astype(BF16)


def _ssd_chunk(z_ref, x_ref, b_ref, c_ref, dtraw_ref, cw_ref, cb_ref, dtb_ref, alog_ref, dskip_ref, nw_ref,
               y_ref, xbuf, shifts, state, expand):
    int_iota = lambda shape, dim: lax.broadcasted_iota(jnp.int32, shape, dim)

    xbuf[CONV_TAIL:, 0:SSD_D_INNER] = x_ref[...]
    xbuf[CONV_TAIL:, SSD_D_INNER:SSD_D_INNER + SSD_BC] = b_ref[...]
    xbuf[CONV_TAIL:, SSD_D_INNER + SSD_BC:] = c_ref[...]
    delayed = _dot(shifts[...], xbuf[...])
    acc = cb_ref[...] + cw_ref[SSD_CONV - 1:SSD_CONV, :] * xbuf[CONV_TAIL:, :].astype(F32)
    for k in range(SSD_CONV - 1):
        tap = SSD_CONV - 2 - k
        acc = acc + cw_ref[tap:tap + 1, :] * delayed[k * CHUNK:(k + 1) * CHUNK, :]
    xbuf[0:CONV_TAIL, :] = xbuf[CHUNK:CHUNK + CONV_TAIL, :]
    xbc = _silu(acc)
    xs = xbc[:, 0:SSD_D_INNER]
    xs_b = xs.astype(BF16)
    bmat = xbc[:, SSD_D_INNER:SSD_D_INNER + SSD_BC]
    cmat = xbc[:, SSD_D_INNER + SSD_BC:]

    pre = dtraw_ref[...] + dtb_ref[...]
    dt = jnp.maximum(pre, 0.0) + jnp.log1p(jnp.exp(-jnp.abs(pre)))
    d_a = dt * (-LOG2_E * jnp.exp(alog_ref[...]))
    causal = int_iota((CHUNK, CHUNK), 0) >= int_iota((CHUNK, CHUNK), 1)
    tril = jnp.where(causal, 1.0, 0.0).astype(BF16)
    acs = sum(_dot(tril, part) for part in _split3(d_a))
    acs_t = acs.T
    src_t = acs_t - jnp.log2(dt).T
    w_t = jnp.exp2(acs_t[:, CHUNK - 1:CHUNK] - src_t)
    end = jnp.broadcast_to(jnp.exp2(acs[CHUNK - 1:CHUNK, :]), (SUBLANES, LANES))
    end_row = sum(_dot(part, expand[...]) for part in _split3(end))[0:1, :]

    lane_blk = jnp.right_shift(int_iota((CHUNK, QUAD_W), 1), HEAD_DIM_LOG2)

    def block_diag(v):
        return jnp.concatenate([jnp.where(lane_blk == j, v, jnp.zeros_like(v)) for j in range(QUAD)], axis=0)

    def group(g, between_quads=lambda q: None):
        gs = slice(g * SSD_STATE, (g + 1) * SSD_STATE)
        b_g = bmat[:, gs]
        c_g = cmat[:, gs]
        cb_g = _dot_nt(c_g.astype(BF16), b_g.astype(BF16))
        b_gt = b_g.T
        y_parts = []
        for qi in range(HEADS_PER_GROUP // QUAD):
            q = g * (HEADS_PER_GROUP // QUAD) + qi
            between_quads(q)
            qs = slice(q * QUAD_W, (q + 1) * QUAD_W)
            lhs_diag, lhs_state, lhs_off = [], [], []
            for j in range(QUAD):
                hd = q * QUAD + j
                col = jnp.broadcast_to(acs[:, hd:hd + 1], (CHUNK, CHUNK))
                decay_dt = jnp.exp2(jnp.where(causal, col - src_t[hd:hd + 1, :], -jnp.inf))
                lhs_diag.append((cb_g * decay_dt).astype(BF16))
                lhs_state.append((b_gt * w_t[hd:hd + 1, :]).astype(BF16))
                lhs_off.append((c_g * jnp.exp2(col)).astype(BF16))
            x_blk = block_diag(xs_b[:, qs])
            s_prev = state[:, qs]
            s_blk = block_diag(s_prev.astype(BF16))
            y_q = _dot(jnp.concatenate(lhs_diag, axis=1), x_blk)
            y_q = y_q + _dot(jnp.concatenate(lhs_off, axis=1), s_blk)
            state[:, qs] = s_prev * end_row[:, qs] + _dot(jnp.concatenate(lhs_state, axis=1), x_blk)
            y_parts.append(y_q + xs[:, qs] * dskip_ref[:, qs])
        ws = slice(g * GROUP_W, (g + 1) * GROUP_W)
        u = jnp.concatenate(y_parts, axis=1) * z_ref[:, ws].astype(F32)
        u = u * lax.rsqrt(jnp.mean(u * u, axis=-1, keepdims=True) + EPS)
        y_ref[:, ws] = (u * nw_ref[:, ws]).astype(BF16)

    return group


def _ret_chunk(q_ref, k_ref, v0_ref, v1_ref, g0_ref, g1_ref, nw_ref, y_ref, state):
    heads_per_block = RET_QK_WIDTH // RET_V_DIM
    log_gammas = _log_gammas()
    causal = (lax.broadcasted_iota(jnp.int32, (CHUNK, CHUNK), 0)
              >= lax.broadcasted_iota(jnp.int32, (CHUNK, CHUNK), 1))

    def head(hd):
        ks = slice(hd * RET_QK_DIM, (hd + 1) * RET_QK_DIM)
        vs = slice(hd * RET_V_DIM, (hd + 1) * RET_V_DIM)
        v_ref, g_ref = (v0_ref, g0_ref) if hd < heads_per_block else (v1_ref, g1_ref)
        bs = slice((hd % heads_per_block) * RET_V_DIM, (hd % heads_per_block + 1) * RET_V_DIM)
        q_h = q_ref[:, ks]
        k_h = k_ref[:, ks]
        v_h = v_ref[:, bs]
        scores = jnp.where(causal, _dot_nt(q_h, k_h), 0.0)
        s_prev = state[hd]
        o = _dot(scores.astype(BF16), v_h) + _dot(q_h, s_prev.astype(BF16))
        state[hd] = (s_prev + _dot_tn(k_h, v_h)) * float(np.exp(np.float32(CHUNK * log_gammas[hd])))
        mu = jnp.mean(o, axis=-1, keepdims=True)
        d = o - mu
        y = d * lax.rsqrt(jnp.mean(d * d, axis=-1, keepdims=True) + EPS) * nw_ref[:, vs]
        y = (y * g_ref[:, bs].astype(F32)).astype(BF16)
        y_ref[:, SSD_D_INNER + hd * RET_V_DIM:SSD_D_INNER + (hd + 1) * RET_V_DIM] = y

    return head


def _mixer_kernel(z_ref, x_ref, b_ref, c_ref, dtraw_ref, cw_ref, cb_ref, dtb_ref, alog_ref, dskip_ref, snw_ref,
                  q_ref, k_ref, v0_ref, v1_ref, g0_ref, g1_ref, rnw_ref,
                  wout_hbm, h_ref, gate_ref, fw_ref, o_ref,
                  xbuf, shifts, ssd_state, expand, ret_state, ybuf, wout, wstage, wsem, *, final):
    step = pl.program_id(0)

    def wout_copy(i, slot):
        return pltpu.make_async_copy(wout_hbm.at[pl.ds(i * W_OUT_ROWS, W_OUT_ROWS), :], wstage.at[slot],
                                     wsem.at[slot])

    @pl.when(step == 0)
    def _():
        n_stage = MIX_WIDTH // W_OUT_ROWS
        wout_copy(0, 0).start()

        def stage(i, carry):
            slot = lax.rem(i, 2)

            @pl.when(i + 1 < n_stage)
            def _():
                wout_copy(i + 1, 1 - slot).start()

            wout_copy(i, slot).wait()
            wout[pl.ds(pl.multiple_of(i * W_OUT_ROWS, W_OUT_ROWS), W_OUT_ROWS), :] = wstage[slot].astype(BF16)
            return carry

        lax.fori_loop(0, n_stage, stage, 0)
        _ssd_init(xbuf, shifts, ssd_state, expand)
        ret_state[...] = jnp.zeros_like(ret_state)
        ybuf[...] = jnp.zeros_like(ybuf)

    slot = lax.rem(step, 2)
    y_old = ybuf[1 - slot]
    piece_w = D_MODEL // RET_HEADS
    pieces = []
    for sub in range(CHUNKS_PER_STEP):
        rows = pl.ds(sub * CHUNK, CHUNK)
        sub_refs = lambda *refs: [r.at[rows] for r in refs]
        y_new = ybuf.at[slot, rows]
        z_c, x_c, b_c, c_c, dtraw_c = sub_refs(z_ref, x_ref, b_ref, c_ref, dtraw_ref)
        ssd_group = _ssd_chunk(z_c, x_c, b_c, c_c, dtraw_c, cw_ref, cb_ref, dtb_ref, alog_ref, dskip_ref, snw_ref,
                               y_new, xbuf, shifts, ssd_state, expand)
        q_c, k_c, v0_c, v1_c, g0_c, g1_c = sub_refs(q_ref, k_ref, v0_ref, v1_ref, g0_ref, g1_ref)
        ret_head = _ret_chunk(q_c, k_c, v0_c, v1_c, g0_c, g1_c, rnw_ref, y_new, ret_state)

        def between_quads(q):
            if q % CHUNKS_PER_STEP == 0:
                n = len(pieces)
                pieces.append(_dot(y_old, wout[:, n * piece_w:(n + 1) * piece_w]))
            ret_head(q)

        for g in range(SSD_GROUPS):
            ssd_group(g, between_quads)

    h = h_ref[...] + gate_ref[...] * jnp.concatenate(pieces, axis=1)
    if final:
        h = h * lax.rsqrt(jnp.mean(h * h, axis=-1, keepdims=True) + EPS) * fw_ref[...]
    o_ref[...] = h


def _mix_and_project(proj, dt_raw, h, mod, conv_w, conv_b, dt_bias, a_log, d_skip, ssd_norm_w, ret_norm_w, w_out,
                     final_norm_w, final):
    L = h.shape[0]
    rows = CHUNKS_PER_STEP * CHUNK
    n_steps = L // rows
    pad = lambda v: jnp.pad(v.reshape(1, SSD_HEADS), ((0, 0), (0, LANES - SSD_HEADS)))
    full = lambda shape: pl.BlockSpec(shape, lambda s: (0, 0))
    mixed = lambda width, col: pl.BlockSpec((rows, width), lambda s: (jnp.minimum(s, n_steps - 1), col))
    projected = lambda width, col: pl.BlockSpec((rows, width), lambda s: (jnp.maximum(s - 1, 0), col))
    return pl.pallas_call(
        functools.partial(_mixer_kernel, final=final),
        grid=(n_steps + 1,),
        in_specs=[mixed(SSD_D_INNER, COL_Z), mixed(SSD_D_INNER, COL_X), mixed(SSD_BC, COL_B), mixed(SSD_BC, COL_C),
                  mixed(LANES, 0),
                  full((SSD_CONV, SSD_CONV_DIM)), full((1, SSD_CONV_DIM)), full((1, LANES)), full((1, LANES)),
                  full((1, SSD_D_INNER)), full((1, SSD_D_INNER)),
                  mixed(RET_QK_WIDTH, COL_Q), mixed(RET_QK_WIDTH, COL_K),
                  mixed(RET_QK_WIDTH, COL_V), mixed(RET_QK_WIDTH, COL_V + 1),
                  mixed(RET_QK_WIDTH, COL_G), mixed(RET_QK_WIDTH, COL_G + 1),
                  full((1, RET_V_WIDTH)),
                  pl.BlockSpec(memory_space=pl.ANY),
                  projected(D_MODEL, 0),
                  pl.BlockSpec((1, D_MODEL), lambda s: (0, 2)),
                  full((1, D_MODEL))],
        out_specs=projected(D_MODEL, 0),
        out_shape=jax.ShapeDtypeStruct((L, D_MODEL), F32),
        scratch_shapes=[pltpu.VMEM((CONV_TAIL + CHUNK, SSD_CONV_DIM), BF16),
                        pltpu.VMEM(((SSD_CONV - 1) * CHUNK, CONV_TAIL + CHUNK), BF16),
                        pltpu.VMEM((SSD_STATE, SSD_D_INNER), F32),
                        pltpu.VMEM((LANES, SSD_D_INNER), BF16),
                        pltpu.VMEM((RET_HEADS, RET_QK_DIM, RET_V_DIM), F32),
                        pltpu.VMEM((2, rows, MIX_WIDTH), BF16),
                        pltpu.VMEM((MIX_WIDTH, D_MODEL), BF16),
                        pltpu.VMEM((2, W_OUT_ROWS, D_MODEL), F32),
                        pltpu.SemaphoreType.DMA((2,))],
        compiler_params=pltpu.CompilerParams(dimension_semantics=("arbitrary",),
                                             vmem_limit_bytes=60 * MIB),
        name="mixers_out_projection",
    )(proj, proj, proj, proj, dt_raw, conv_w, conv_b.reshape(1, SSD_CONV_DIM), pad(dt_bias), pad(a_log),
      jnp.repeat(d_skip, SSD_HEAD_DIM).reshape(1, SSD_D_INNER), ssd_norm_w.reshape(1, SSD_D_INNER),
      proj, proj, proj, proj, proj, proj, ret_norm_w.reshape(1, RET_V_WIDTH),
      w_out, h, mod, final_norm_w.reshape(1, D_MODEL))


def kernel(x, c, w_ada, b_ada, norm_w, w_in, conv_w, conv_b, dt_bias, a_log, d_skip, ssd_norm_w, ret_norm_w,
           w_out, final_norm_w):
    bsz, L, d_model = x.shape
    assert bsz == 1 and d_model == D_MODEL and L % TM_IN == 0 and w_in.shape[-1] == IN_WIDTH
    depth = w_in.shape[0]
    half = RET_QK_DIM // 2
    inv = ROPE_BASE ** (-jnp.arange(half, dtype=F32) / half)
    inv = jnp.concatenate([inv, inv]).reshape(1, RET_QK_DIM)
    h = x.reshape(L, D_MODEL)
    for layer in range(depth):
        mod = _modulation(c, w_ada[layer], b_ada[layer])
        proj, dt_raw = _in_projection(h, norm_w[layer], mod, jnp.swapaxes(w_in[layer], 0, 1), inv)
        h = _mix_and_project(proj, dt_raw, h, mod, conv_w[layer], conv_b[layer], dt_bias[layer], a_log[layer],
                             d_skip[layer], ssd_norm_w[layer], ret_norm_w[layer], w_out[layer], final_norm_w,
                             final=layer == depth - 1)
    return h.reshape(bsz, L, D_MODEL)
```

```python
import functools
import math

import numpy as np
import jax
import jax.numpy as jnp
from jax import lax
from jax.experimental import pallas as pl
from jax.experimental.pallas import tpu as pltpu

D_MODEL = 2048
SSD_D_INNER = D_MODEL
SSD_HEAD_DIM = 64
SSD_HEADS = SSD_D_INNER // SSD_HEAD_DIM
HEAD_DIM_LOG2 = SSD_HEAD_DIM.bit_length() - 1
SSD_GROUPS = 4
SSD_STATE = 128
SSD_CONV = 4
SSD_BC = SSD_GROUPS * SSD_STATE
SSD_CONV_DIM = SSD_D_INNER + 2 * SSD_BC
RET_HEADS = 8
RET_QK_DIM = 128
RET_V_DIM = 256
RET_QK_WIDTH = RET_HEADS * RET_QK_DIM
RET_V_WIDTH = RET_HEADS * RET_V_DIM
MIX_WIDTH = SSD_D_INNER + RET_V_WIDTH
CHUNK = 128
ROPE_BASE = 10000.0
EPS = 1e-6
LOG2_E = math.log2(math.e)

OFF_Z = SSD_D_INNER
OFF_XBC = OFF_Z + SSD_CONV_DIM
OFF_DT = OFF_XBC + SSD_HEADS
OFF_Q = OFF_DT + RET_QK_WIDTH
OFF_K = OFF_Q + RET_QK_WIDTH
OFF_V = OFF_K + RET_V_WIDTH
IN_WIDTH = OFF_V + RET_V_WIDTH

PROJ_WIDTH = IN_WIDTH - SSD_HEADS
COL_Z, COL_X = 0, 1
COL_B, COL_C = 8, 9
COL_Q, COL_K, COL_V, COL_G = 5, 6, 7, 9

LANES = 128
SUBLANES = 8
QUAD = 4
QUAD_W = QUAD * SSD_HEAD_DIM
HEADS_PER_GROUP = SSD_HEADS // SSD_GROUPS
GROUP_W = SSD_D_INNER // SSD_GROUPS
CONV_TAIL = 16
CHUNKS_PER_STEP = 2
W_OUT_ROWS = 256

TM_IN, TN_IN = 1024, 1024
TILE_CONV0, TILE_ROT0, TILE_V0, TILE_G0 = (c // TN_IN for c in (OFF_Z, OFF_XBC, OFF_XBC + 2 * RET_QK_WIDTH,
                                                                 OFF_XBC + 2 * RET_QK_WIDTH + RET_V_WIDTH))
N_COL_TILES = PROJ_WIDTH // TN_IN
assert OFF_Z % TN_IN == 0 and OFF_XBC % TN_IN == 0 and RET_QK_WIDTH == TN_IN and RET_V_WIDTH % TN_IN == 0
assert PROJ_WIDTH % TN_IN == 0 and TM_IN % CHUNK == 0 and N_COL_TILES - 1 >= TILE_G0
MIB = 1024 * 1024

F32 = jnp.float32
BF16 = jnp.bfloat16


def _silu(v):
    return v / (1.0 + jnp.exp(-v))


def _dot(a, b):
    return jnp.dot(a, b, preferred_element_type=F32)


def _dot_nt(a, b):
    return lax.dot_general(a, b, (((1,), (1,)), ((), ())), preferred_element_type=F32)


def _dot_tn(a, b):
    return lax.dot_general(a, b, (((0,), (0,)), ((), ())), preferred_element_type=F32)


def _split3(v):
    hi = v.astype(BF16)
    r1 = v - hi.astype(F32)
    mid = r1.astype(BF16)
    lo = (r1 - mid.astype(F32)).astype(BF16)
    return hi, mid, lo


def _log_gammas():
    return [float(np.log1p(-np.exp2(np.float32(-5.0 - hd)), dtype=np.float32)) for hd in range(RET_HEADS)]


def _mod_kernel(c_ref, w_ref, b_ref, o_ref):
    cond = jnp.broadcast_to(_silu(c_ref[...]), (SUBLANES, D_MODEL))
    o_ref[...] = _dot(cond, w_ref[...])[0:1, :] + b_ref[...]


def _modulation(c, w_ada, b_ada):
    n = w_ada.shape[1]
    tn = 512
    return pl.pallas_call(
        _mod_kernel,
        grid=(n // tn,),
        in_specs=[pl.BlockSpec((1, D_MODEL), lambda j: (0, 0)),
                  pl.BlockSpec((D_MODEL, tn), lambda j: (0, j)),
                  pl.BlockSpec((1, tn), lambda j: (0, j))],
        out_specs=pl.BlockSpec((1, tn), lambda j: (0, j)),
        out_shape=jax.ShapeDtypeStruct((1, n), F32),
        compiler_params=pltpu.CompilerParams(dimension_semantics=("arbitrary",),
                                             vmem_limit_bytes=24 * MIB),
        name="adaln_modulation",
    )(c, w_ada, b_ada.reshape(1, n))


def _inproj_kernel(x_ref, nw_ref, shift_ref, scale_ref, wa_ref, wb_ref, wd_ref, inv_ref,
                   o_ref, dt_ref, u_ref, prev, cos_in, sin_in, cos_t, sin_t, decay, *, n_tiles):
    t = pl.program_id(0)
    tile = jnp.minimum(t, n_tiles - 1)
    i = tile // N_COL_TILES
    j = tile - i * N_COL_TILES
    je = lax.rem(jnp.maximum(t - 1, 0), N_COL_TILES)
    half = RET_QK_DIM // 2

    @pl.when(t == 0)
    def _():
        prev[...] = jnp.zeros_like(prev)
        ang = lax.broadcasted_iota(jnp.int32, (TM_IN, RET_QK_DIM), 0).astype(F32) * inv_ref[...]
        cos_in[...] = jnp.cos(ang)
        sin_in[...] = jnp.sin(ang)
        pos = lax.broadcasted_iota(jnp.int32, (CHUNK, RET_QK_DIM), 0).astype(F32) + 1.0
        for hd, lg in enumerate(_log_gammas()):
            decay[0, hd] = jnp.exp(pos * lg)
            decay[1, hd] = jnp.exp(pos * -lg) * RET_QK_DIM ** -0.5

    @pl.when(jnp.logical_and(j == 0, t < n_tiles))
    def _():
        x = x_ref[...]
        y = x * lax.rsqrt(jnp.mean(x * x, axis=-1, keepdims=True) + EPS) * nw_ref[...]
        u = (y * (1.0 + scale_ref[...]) + shift_ref[...]).astype(BF16)
        u_ref[...] = u
        is_dt = lax.broadcasted_iota(jnp.int32, (LANES, D_MODEL), 0) < SSD_HEADS
        dt_ref[...] = _dot_nt(u, jnp.where(is_dt, wd_ref[...], 0.0).astype(BF16))

        base = jnp.broadcast_to((i * TM_IN).astype(F32) * inv_ref[...], (SUBLANES, RET_QK_DIM))
        cos_b = jnp.cos(base)[0:1, :]
        sin_b = jnp.sin(base)[0:1, :]
        cos_t[...] = cos_in[...] * cos_b - sin_in[...] * sin_b
        sin_pos = sin_in[...] * cos_b + cos_in[...] * sin_b
        first_half = lax.broadcasted_iota(jnp.int32, (TM_IN, RET_QK_DIM), 1) < half
        sin_t[...] = jnp.where(first_half, -sin_pos, sin_pos)

    def project():
        after_dt = j >= TILE_ROT0
        skip = pl.multiple_of(jnp.where(after_dt, SSD_HEADS, 0), SSD_HEADS)
        head = wa_ref[pl.ds(skip, TN_IN - SSD_HEADS), :]
        tail = jnp.where(after_dt, wb_ref[...], wa_ref[TN_IN - SSD_HEADS:, :])
        w = jnp.concatenate([head, tail], axis=0).astype(BF16)
        return _dot_nt(u_ref[...], w)

    @pl.when(jnp.logical_and(jnp.logical_or(je < TILE_CONV0, je >= TILE_G0), t < n_tiles))
    def _():
        acc = project()
        o_ref[...] = _silu(prev[...]).astype(BF16)
        prev[...] = acc

    @pl.when(t == n_tiles)
    def _():
        o_ref[...] = _silu(prev[...]).astype(BF16)

    @pl.when(jnp.logical_or(jnp.logical_and(je >= TILE_CONV0, je < TILE_ROT0),
                            jnp.logical_and(je >= TILE_V0, je < TILE_G0)))
    def _():
        acc = project()
        o_ref[...] = prev[...].astype(BF16)
        prev[...] = acc

    @pl.when(jnp.logical_and(je >= TILE_ROT0, je < TILE_V0))
    def _():
        acc = project()
        side = je - TILE_ROT0
        for hd in range(RET_HEADS):
            cols = slice(hd * RET_QK_DIM, (hd + 1) * RET_QK_DIM)
            scale = decay[side, hd]
            for blk in range(TM_IN // CHUNK):
                rows = slice(blk * CHUNK, (blk + 1) * CHUNK)
                a = prev[rows, cols]
                rot = a * cos_t[rows, :] + pltpu.roll(a, half, 1) * sin_t[rows, :]
                o_ref[rows, cols] = (rot * scale).astype(BF16)
        prev[...] = acc


def _in_projection(h, norm_w, mod, w_in_t, inv):
    L = h.shape[0]
    n_tiles = (L // TM_IN) * N_COL_TILES
    const = lambda t: (0, 0)
    mul_row = lambda t: jnp.minimum(t, n_tiles - 1) // N_COL_TILES
    mul_col = lambda t: lax.rem(jnp.minimum(t, n_tiles - 1), N_COL_TILES)
    epi = lambda t: jnp.maximum(t - 1, 0)
    table = pltpu.VMEM((TM_IN, RET_QK_DIM), F32)
    return pl.pallas_call(
        functools.partial(_inproj_kernel, n_tiles=n_tiles),
        grid=(n_tiles + 1,),
        in_specs=[pl.BlockSpec((TM_IN, D_MODEL), lambda t: (mul_row(t), 0)),
                  pl.BlockSpec((1, D_MODEL), const),
                  pl.BlockSpec((1, D_MODEL), lambda t: (0, 0)),
                  pl.BlockSpec((1, D_MODEL), lambda t: (0, 1)),
                  pl.BlockSpec((TN_IN, D_MODEL), lambda t: (mul_col(t), 0)),
                  pl.BlockSpec((SSD_HEADS, D_MODEL), lambda t: ((mul_col(t) + 1) * (TN_IN // SSD_HEADS), 0)),
                  pl.BlockSpec((LANES, D_MODEL), lambda t: (OFF_XBC // LANES, 0)),
                  pl.BlockSpec((1, RET_QK_DIM), const)],
        out_specs=[pl.BlockSpec((TM_IN, TN_IN), lambda t: (epi(t) // N_COL_TILES, lax.rem(epi(t), N_COL_TILES))),
                   pl.BlockSpec((TM_IN, LANES), lambda t: (mul_row(t), 0))],
        out_shape=[jax.ShapeDtypeStruct((L, PROJ_WIDTH), BF16),
                   jax.ShapeDtypeStruct((L, LANES), F32)],
        scratch_shapes=[pltpu.VMEM((TM_IN, D_MODEL), BF16),
                        pltpu.VMEM((TM_IN, TN_IN), F32),
                        table, table, table, table,
                        pltpu.VMEM((2, RET_HEADS, CHUNK, RET_QK_DIM), F32)],
        compiler_params=pltpu.CompilerParams(dimension_semantics=("arbitrary",),
                                             vmem_limit_bytes=60 * MIB),
        name="adaln_in_projection",
    )(h, norm_w.reshape(1, D_MODEL), mod, mod, w_in_t, w_in_t, w_in_t, inv)


def _ssd_init(xbuf, shifts, state, expand):
    xbuf[0:CONV_TAIL, :] = jnp.zeros((CONV_TAIL, SSD_CONV_DIM), BF16)
    row = lax.broadcasted_iota(jnp.int32, shifts.shape, 0)
    col = lax.broadcasted_iota(jnp.int32, shifts.shape, 1)
    delay = jnp.right_shift(row, CHUNK.bit_length() - 1) + 1
    t = jnp.bitwise_and(row, CHUNK - 1)
    shifts[...] = jnp.where(col == CONV_TAIL + t - delay, 1.0, 0.0).astype(BF16)
    state[...] = jnp.zeros_like(state)
    head_of_lane = jnp.right_shift(lax.broadcasted_iota(jnp.int32, (LANES, SSD_D_INNER), 1), HEAD_DIM_LOG2)
    row = lax.broadcasted_iota(jnp.int32, (LANES, SSD_D_INNER), 0)
    expand[...] = jnp.where(head_of_lane == row, 1.0, 0.0).astype(BF16)


def _ssd_chunk(z_ref, x_ref, b_ref, c_ref, dtraw_ref, cw_ref, cb_ref, dtb_ref, alog_ref, dskip_ref, nw_ref,
               y_ref, xbuf, shifts, state, expand):
    int_iota = lambda shape, dim: lax.broadcasted_iota(jnp.int32, shape, dim)

    xbuf[CONV_TAIL:, 0:SSD_D_INNER] = x_ref[...]
    xbuf[CONV_TAIL:, SSD_D_INNER:SSD_D_INNER + SSD_BC] = b_ref[...]
    xbuf[CONV_TAIL:, SSD_D_INNER + SSD_BC:] = c_ref[...]
    delayed = _dot(shifts[...], xbuf[...])
    acc = cb_ref[...] + cw_ref[SSD_CONV - 1:SSD_CONV, :] * xbuf[CONV_TAIL:, :].astype(F32)
    for k in range(SSD_CONV - 1):
        tap = SSD_CONV - 2 - k
        acc = acc + cw_ref[tap:tap + 1, :] * delayed[k * CHUNK:(k + 1) * CHUNK, :]
    xbuf[0:CONV_TAIL, :] = xbuf[CHUNK:CHUNK + CONV_TAIL, :]
    xbc = _silu(acc)
    xs = xbc[:, 0:SSD_D_INNER]
    xs_b = xs.astype(BF16)
    bmat = xbc[:, SSD_D_INNER:SSD_D_INNER + SSD_BC]
    cmat = xbc[:, SSD_D_INNER + SSD_BC:]

    pre = dtraw_ref[...] + dtb_ref[...]
    dt = jnp.maximum(pre, 0.0) + jnp.log1p(jnp.exp(-jnp.abs(pre)))
    d_a = dt * (-LOG2_E * jnp.exp(alog_ref[...]))
    causal = int_iota((CHUNK, CHUNK), 0) >= int_iota((CHUNK, CHUNK), 1)
    tril = jnp.where(causal, 1.0, 0.0).astype(BF16)
    acs = sum(_dot(tril, part) for part in _split3(d_a))
    acs_t = acs.T
    src_t = acs_t - jnp.log2(dt).T
    w_t = jnp.exp2(acs_t[:, CHUNK - 1:CHUNK] - src_t)
    end = jnp.broadcast_to(jnp.exp2(acs[CHUNK - 1:CHUNK, :]), (SUBLANES, LANES))
    end_row = sum(_dot(part, expand[...]) for part in _split3(end))[0:1, :]

    lane_blk = jnp.right_shift(int_iota((CHUNK, QUAD_W), 1), HEAD_DIM_LOG2)

    def block_diag(v):
        return jnp.concatenate([jnp.where(lane_blk == j, v, jnp.zeros_like(v)) for j in range(QUAD)], axis=0)

    def group(g, between_quads=lambda q: None):
        gs = slice(g * SSD_STATE, (g + 1) * SSD_STATE)
        b_g = bmat[:, gs]
        c_g = cmat[:, gs]
        cb_g = _dot_nt(c_g.astype(BF16), b_g.astype(BF16))
        b_gt = b_g.T
        y_parts = []
        for qi in range(HEADS_PER_GROUP // QUAD):
            q = g * (HEADS_PER_GROUP // QUAD) + qi
            between_quads(q)
            qs = slice(q * QUAD_W, (q + 1) * QUAD_W)
            lhs_diag, lhs_state, lhs_off = [], [], []
            for j in range(QUAD):
                hd = q * QUAD + j
                col = jnp.broadcast_to(acs[:, hd:hd + 1], (CHUNK, CHUNK))
                decay_dt = jnp.exp2(jnp.where(causal, col - src_t[hd:hd + 1, :], -jnp.inf))
                lhs_diag.append((cb_g * decay_dt).astype(BF16))
                lhs_state.append((b_gt * w_t[hd:hd + 1, :]).astype(BF16))
                lhs_off.append((c_g * jnp.exp2(col)).astype(BF16))
            x_blk = block_diag(xs_b[:, qs])
            s_prev = state[:, qs]
            s_blk = block_diag(s_prev.astype(BF16))
            y_q = _dot(jnp.concatenate(lhs_diag, axis=1), x_blk)
            y_q = y_q + _dot(jnp.concatenate(lhs_off, axis=1), s_blk)
            state[:, qs] = s_prev * end_row[:, qs] + _dot(jnp.concatenate(lhs_state, axis=1), x_blk)
            y_parts.append(y_q + xs[:, qs] * dskip_ref[:, qs])
        ws = slice(g * GROUP_W, (g + 1) * GROUP_W)
        u = jnp.concatenate(y_parts, axis=1) * z_ref[:, ws].astype(F32)
        u = u * lax.rsqrt(jnp.mean(u * u, axis=-1, keepdims=True) + EPS)
        y_ref[:, ws] = (u * nw_ref[:, ws]).astype(BF16)

    return group


def _ret_chunk(q_ref, k_ref, v0_ref, v1_ref, g0_ref, g1_ref, nw_ref, y_ref, state):
    heads_per_block = RET_QK_WIDTH // RET_V_DIM
    log_gammas = _log_gammas()
    causal = (lax.broadcasted_iota(jnp.int32, (CHUNK, CHUNK), 0)
              >= lax.broadcasted_iota(jnp.int32, (CHUNK, CHUNK), 1))

    def head(hd):
        ks = slice(hd * RET_QK_DIM, (hd + 1) * RET_QK_DIM)
        vs = slice(hd * RET_V_DIM, (hd + 1) * RET_V_DIM)
        v_ref, g_ref = (v0_ref, g0_ref) if hd < heads_per_block else (v1_ref, g1_ref)
        bs = slice((hd % heads_per_block) * RET_V_DIM, (hd % heads_per_block + 1) * RET_V_DIM)
        q_h = q_ref[:, ks]
        k_h = k_ref[:, ks]
        v_h = v_ref[:, bs]
        scores = jnp.where(causal, _dot_nt(q_h, k_h), 0.0)
        s_prev = state[hd]
        o = _dot(scores.astype(BF16), v_h) + _dot(q_h, s_prev.astype(BF16))
        state[hd] = (s_prev + _dot_tn(k_h, v_h)) * float(np.exp(np.float32(CHUNK * log_gammas[hd])))
        mu = jnp.mean(o, axis=-1, keepdims=True)
        d = o - mu
        y = d * lax.rsqrt(jnp.mean(d * d, axis=-1, keepdims=True) + EPS) * nw_ref[:, vs]
        y = (y * g_ref[:, bs].astype(F32)).astype(BF16)
        y_ref[:, SSD_D_INNER + hd * RET_V_DIM:SSD_D_INNER + (hd + 1) * RET_V_DIM] = y

    return head


def _mixer_kernel(z_ref, x_ref, b_ref, c_ref, dtraw_ref, cw_ref, cb_ref, dtb_ref, alog_ref, dskip_ref, snw_ref,
                  q_ref, k_ref, v0_ref, v1_ref, g0_ref, g1_ref, rnw_ref,
                  wout_hbm, h_ref, gate_ref, fw_ref, o_ref,
                  xbuf, shifts, ssd_state, expand, ret_state, ybuf, wout, wstage, wsem, *, final):
    step = pl.program_id(0)

    def wout_copy(i, slot):
        return pltpu.make_async_copy(wout_hbm.at[pl.ds(i * W_OUT_ROWS, W_OUT_ROWS), :], wstage.at[slot],
                                     wsem.at[slot])

    @pl.when(step == 0)
    def _():
        n_stage = MIX_WIDTH // W_OUT_ROWS
        wout_copy(0, 0).start()

        def stage(i, carry):
            slot = lax.rem(i, 2)

            @pl.when(i + 1 < n_stage)
            def _():
                wout_copy(i + 1, 1 - slot).start()

            wout_copy(i, slot).wait()
            wout[pl.ds(pl.multiple_of(i * W_OUT_ROWS, W_OUT_ROWS), W_OUT_ROWS), :] = wstage[slot].astype(BF16)
            return carry

        lax.fori_loop(0, n_stage, stage, 0)
        _ssd_init(xbuf, shifts, ssd_state, expand)
        ret_state[...] = jnp.zeros_like(ret_state)
        ybuf[...] = jnp.zeros_like(ybuf)

    slot = lax.rem(step, 2)
    y_old = ybuf[1 - slot]
    piece_w = D_MODEL // RET_HEADS
    pieces = []
    for sub in range(CHUNKS_PER_STEP):
        rows = pl.ds(sub * CHUNK, CHUNK)
        sub_refs = lambda *refs: [r.at[rows] for r in refs]
        y_new = ybuf.at[slot, rows]
        z_c, x_c, b_c, c_c, dtraw_c = sub_refs(z_ref, x_ref, b_ref, c_ref, dtraw_ref)
        ssd_group = _ssd_chunk(z_c, x_c, b_c, c_c, dtraw_c, cw_ref, cb_ref, dtb_ref, alog_ref, dskip_ref, snw_ref,
                               y_new, xbuf, shifts, ssd_state, expand)
        q_c, k_c, v0_c, v1_c, g0_c, g1_c = sub_refs(q_ref, k_ref, v0_ref, v1_ref, g0_ref, g1_ref)
        ret_head = _ret_chunk(q_c, k_c, v0_c, v1_c, g0_c, g1_c, rnw_ref, y_new, ret_state)

        def between_quads(q):
            if q % CHUNKS_PER_STEP == 0:
                n = len(pieces)
                pieces.append(_dot(y_old, wout[:, n * piece_w:(n + 1) * piece_w]))
            ret_head(q)

        for g in range(SSD_GROUPS):
            ssd_group(g, between_quads)

    h = h_ref[...] + gate_ref[...] * jnp.concatenate(pieces, axis=1)
    if final:
        h = h * lax.rsqrt(jnp.mean(h * h, axis=-1, keepdims=True) + EPS) * fw_ref[...]
    o_ref[...] = h


def _mix_and_project(proj, dt_raw, h, mod, conv_w, conv_b, dt_bias, a_log, d_skip, ssd_norm_w, ret_norm_w, w_out,
                     final_norm_w, final):
    L = h.shape[0]
    rows = CHUNKS_PER_STEP * CHUNK
    n_steps = L // rows
    pad = lambda v: jnp.pad(v.reshape(1, SSD_HEADS), ((0, 0), (0, LANES - SSD_HEADS)))
    full = lambda shape: pl.BlockSpec(shape, lambda s: (0, 0))
    mixed = lambda width, col: pl.BlockSpec((rows, width), lambda s: (jnp.minimum(s, n_steps - 1), col))
    projected = lambda width, col: pl.BlockSpec((rows, width), lambda s: (jnp.maximum(s - 1, 0), col))
    return pl.pallas_call(
        functools.partial(_mixer_kernel, final=final),
        grid=(n_steps + 1,),
        in_specs=[mixed(SSD_D_INNER, COL_Z), mixed(SSD_D_INNER, COL_X), mixed(SSD_BC, COL_B), mixed(SSD_BC, COL_C),
                  mixed(LANES, 0),
                  full((SSD_CONV, SSD_CONV_DIM)), full((1, SSD_CONV_DIM)), full((1, LANES)), full((1, LANES)),
                  full((1, SSD_D_INNER)), full((1, SSD_D_INNER)),
                  mixed(RET_QK_WIDTH, COL_Q), mixed(RET_QK_WIDTH, COL_K),
                  mixed(RET_QK_WIDTH, COL_V), mixed(RET_QK_WIDTH, COL_V + 1),
                  mixed(RET_QK_WIDTH, COL_G), mixed(RET_QK_WIDTH, COL_G + 1),
                  full((1, RET_V_WIDTH)),
                  pl.BlockSpec(memory_space=pl.ANY),
                  projected(D_MODEL, 0),
                  pl.BlockSpec((1, D_MODEL), lambda s: (0, 2)),
                  full((1, D_MODEL))],
        out_specs=projected(D_MODEL, 0),
        out_shape=jax.ShapeDtypeStruct((L, D_MODEL), F32),
        scratch_shapes=[pltpu.VMEM((CONV_TAIL + CHUNK, SSD_CONV_DIM), BF16),
                        pltpu.VMEM(((SSD_CONV - 1) * CHUNK, CONV_TAIL + CHUNK), BF16),
                        pltpu.VMEM((SSD_STATE, SSD_D_INNER), F32),
                        pltpu.VMEM((LANES, SSD_D_INNER), BF16),
                        pltpu.VMEM((RET_HEADS, RET_QK_DIM, RET_V_DIM), F32),
                        pltpu.VMEM((2, rows, MIX_WIDTH), BF16),
                        pltpu.VMEM((MIX_WIDTH, D_MODEL), BF16),
                        pltpu.VMEM((2, W_OUT_ROWS, D_MODEL), F32),
                        pltpu.SemaphoreType.DMA((2,))],
        compiler_params=pltpu.CompilerParams(dimension_semantics=("arbitrary",),
                                             vmem_limit_bytes=60 * MIB),
        name="mixers_out_projection",
    )(proj, proj, proj, proj, dt_raw, conv_w, conv_b.reshape(1, SSD_CONV_DIM), pad(dt_bias), pad(a_log),
      jnp.repeat(d_skip, SSD_HEAD_DIM).reshape(1, SSD_D_INNER), ssd_norm_w.reshape(1, SSD_D_INNER),
      proj, proj, proj, proj, proj, proj, ret_norm_w.reshape(1, RET_V_WIDTH),
      w_out, h, mod, final_norm_w.reshape(1, D_MODEL))


def kernel(x, c, w_ada, b_ada, norm_w, w_in, conv_w, conv_b, dt_bias, a_log, d_skip, ssd_norm_w, ret_norm_w,
           w_out, final_norm_w):
    bsz, L, d_model = x.shape
    assert bsz == 1 and d_model == D_MODEL and L % TM_IN == 0 and w_in.shape[-1] == IN_WIDTH
    depth = w_in.shape[0]
    half = RET_QK_DIM // 2
    inv = ROPE_BASE ** (-jnp.arange(half, dtype=F32) / half)
    inv = jnp.concatenate([inv, inv]).reshape(1, RET_QK_DIM)
    h = x.reshape(L, D_MODEL)
    for layer in range(depth):
        mod = _modulation(c, w_ada[layer], b_ada[layer])
        proj, dt_raw = _in_projection(h, norm_w[layer], mod, jnp.swapaxes(w_in[layer], 0, 1), inv)
        h = _mix_and_project(proj, dt_raw, h, mod, conv_w[layer], conv_b[layer], dt_bias[layer], a_log[layer],
                             d_skip[layer], ssd_norm_w[layer], ret_norm_w[layer], w_out[layer], final_norm_w,
                             final=layer == depth - 1)
    return h.reshape(bsz, L, D_MODEL)
```

```python
import functools
import math

import numpy as np
import jax
import jax.numpy as jnp
from jax import lax
from jax.experimental import pallas as pl
from jax.experimental.pallas import tpu as pltpu

D_MODEL = 2048
SSD_D_INNER = D_MODEL
SSD_HEAD_DIM = 64
SSD_HEADS = SSD_D_INNER // SSD_HEAD_DIM
HEAD_DIM_LOG2 = SSD_HEAD_DIM.bit_length() - 1
SSD_GROUPS = 4
SSD_STATE = 128
SSD_CONV = 4
SSD_BC = SSD_GROUPS * SSD_STATE
SSD_CONV_DIM = SSD_D_INNER + 2 * SSD_BC
RET_HEADS = 8
RET_QK_DIM = 128
RET_V_DIM = 256
RET_QK_WIDTH = RET_HEADS * RET_QK_DIM
RET_V_WIDTH = RET_HEADS * RET_V_DIM
MIX_WIDTH = SSD_D_INNER + RET_V_WIDTH
CHUNK = 128
ROPE_BASE = 10000.0
EPS = 1e-6
LOG2_E = math.log2(math.e)

OFF_Z = SSD_D_INNER
OFF_XBC = OFF_Z + SSD_CONV_DIM
OFF_DT = OFF_XBC + SSD_HEADS
OFF_Q = OFF_DT + RET_QK_WIDTH
OFF_K = OFF_Q + RET_QK_WIDTH
OFF_V = OFF_K + RET_V_WIDTH
IN_WIDTH = OFF_V + RET_V_WIDTH

PROJ_WIDTH = IN_WIDTH - SSD_HEADS
COL_Z, COL_X = 0, 1
COL_B, COL_C = 8, 9
COL_Q, COL_K, COL_V, COL_G = 5, 6, 7, 9

LANES = 128
SUBLANES = 8
QUAD = 4
QUAD_W = QUAD * SSD_HEAD_DIM
HEADS_PER_GROUP = SSD_HEADS // SSD_GROUPS
GROUP_W = SSD_D_INNER // SSD_GROUPS
CONV_TAIL = SUBLANES
CHUNKS_PER_STEP = 2
W_OUT_ROWS = 256

TM_IN, TN_IN = 1024, 1024
TILE_CONV0, TILE_ROT0, TILE_V0, TILE_G0 = (c // TN_IN for c in (OFF_Z, OFF_XBC, OFF_XBC + 2 * RET_QK_WIDTH,
                                                                 OFF_XBC + 2 * RET_QK_WIDTH + RET_V_WIDTH))
N_COL_TILES = PROJ_WIDTH // TN_IN
assert OFF_Z % TN_IN == 0 and OFF_XBC % TN_IN == 0 and RET_QK_WIDTH == TN_IN and RET_V_WIDTH % TN_IN == 0
assert PROJ_WIDTH % TN_IN == 0 and TM_IN % CHUNK == 0 and N_COL_TILES - 1 >= TILE_G0
MIB = 1024 * 1024

F32 = jnp.float32
BF16 = jnp.bfloat16


def _silu(v):
    return v / (1.0 + jnp.exp(-v))


def _dot(a, b):
    return jnp.dot(a, b, preferred_element_type=F32)


def _dot_nt(a, b):
    return lax.dot_general(a, b, (((1,), (1,)), ((), ())), preferred_element_type=F32)


def _dot_tn(a, b):
    return lax.dot_general(a, b, (((0,), (0,)), ((), ())), preferred_element_type=F32)


def _split3(v):
    hi = v.astype(BF16)
    r1 = v - hi.astype(F32)
    mid = r1.astype(BF16)
    lo = (r1 - mid.astype(F32)).astype(BF16)
    return hi, mid, lo


def _log_gammas():
    return [float(np.log1p(-np.exp2(np.float32(-5.0 - hd)), dtype=np.float32)) for hd in range(RET_HEADS)]


def _mod_kernel(c_ref, w_ref, b_ref, o_ref):
    cond = jnp.broadcast_to(_silu(c_ref[...]), (SUBLANES, D_MODEL))
    o_ref[...] = _dot(cond, w_ref[...])[0:1, :] + b_ref[...]


def _modulation(c, w_ada, b_ada):
    n = w_ada.shape[1]
    tn = 512
    return pl.pallas_call(
        _mod_kernel,
        grid=(n // tn,),
        in_specs=[pl.BlockSpec((1, D_MODEL), lambda j: (0, 0)),
                  pl.BlockSpec((D_MODEL, tn), lambda j: (0, j)),
                  pl.BlockSpec((1, tn), lambda j: (0, j))],
        out_specs=pl.BlockSpec((1, tn), lambda j: (0, j)),
        out_shape=jax.ShapeDtypeStruct((1, n), F32),
        compiler_params=pltpu.CompilerParams(dimension_semantics=("arbitrary",),
                                             vmem_limit_bytes=24 * MIB),
        name="adaln_modulation",
    )(c, w_ada, b_ada.reshape(1, n))


def _inproj_kernel(x_ref, nw_ref, shift_ref, scale_ref, wa_ref, wb_ref, wd_ref, inv_ref,
                   o_ref, dt_ref, u_ref, prev, cos_in, sin_in, cos_t, sin_t, decay, *, n_tiles):
    t = pl.program_id(0)
    tile = jnp.minimum(t, n_tiles - 1)
    i = tile // N_COL_TILES
    j = tile - i * N_COL_TILES
    je = lax.rem(jnp.maximum(t - 1, 0), N_COL_TILES)
    half = RET_QK_DIM // 2

    @pl.when(t == 0)
    def _():
        prev[...] = jnp.zeros_like(prev)
        ang = lax.broadcasted_iota(jnp.int32, (TM_IN, RET_QK_DIM), 0).astype(F32) * inv_ref[...]
        cos_in[...] = jnp.cos(ang)
        sin_in[...] = jnp.sin(ang)
        pos = lax.broadcasted_iota(jnp.int32, (CHUNK, RET_QK_DIM), 0).astype(F32) + 1.0
        for hd, lg in enumerate(_log_gammas()):
            decay[0, hd] = jnp.exp(pos * lg)
            decay[1, hd] = jnp.exp(pos * -lg) * RET_QK_DIM ** -0.5

    @pl.when(jnp.logical_and(j == 0, t < n_tiles))
    def _():
        x = x_ref[...]
        y = x * lax.rsqrt(jnp.mean(x * x, axis=-1, keepdims=True) + EPS) * nw_ref[...]
        u = (y * (1.0 + scale_ref[...]) + shift_ref[...]).astype(BF16)
        u_ref[...] = u
        is_dt = lax.broadcasted_iota(jnp.int32, (LANES, D_MODEL), 0) < SSD_HEADS
        dt_ref[...] = _dot_nt(u, jnp.where(is_dt, wd_ref[...], 0.0).astype(BF16))

        base = jnp.broadcast_to((i * TM_IN).astype(F32) * inv_ref[...], (SUBLANES, RET_QK_DIM))
        cos_b = jnp.cos(base)[0:1, :]
        sin_b = jnp.sin(base)[0:1, :]
        cos_t[...] = cos_in[...] * cos_b - sin_in[...] * sin_b
        sin_pos = sin_in[...] * cos_b + cos_in[...] * sin_b
        first_half = lax.broadcasted_iota(jnp.int32, (TM_IN, RET_QK_DIM), 1) < half
        sin_t[...] = jnp.where(first_half, -sin_pos, sin_pos)

    def project():
        after_dt = j >= TILE_ROT0
        skip = pl.multiple_of(jnp.where(after_dt, SSD_HEADS, 0), SSD_HEADS)
        head = wa_ref[pl.ds(skip, TN_IN - SSD_HEADS), :]
        tail = jnp.where(after_dt, wb_ref[...], wa_ref[TN_IN - SSD_HEADS:, :])
        w = jnp.concatenate([head, tail], axis=0).astype(BF16)
        return _dot_nt(u_ref[...], w)

    @pl.when(jnp.logical_and(jnp.logical_or(je < TILE_CONV0, je >= TILE_G0), t < n_tiles))
    def _():
        acc = project()
        o_ref[...] = _silu(prev[...]).astype(BF16)
        prev[...] = acc

    @pl.when(t == n_tiles)
    def _():
        o_ref[...] = _silu(prev[...]).astype(BF16)

    @pl.when(jnp.logical_or(jnp.logical_and(je >= TILE_CONV0, je < TILE_ROT0),
                            jnp.logical_and(je >= TILE_V0, je < TILE_G0)))
    def _():
        acc = project()
        o_ref[...] = prev[...].astype(BF16)
        prev[...] = acc

    @pl.when(jnp.logical_and(je >= TILE_ROT0, je < TILE_V0))
    def _():
        acc = project()
        side = je - TILE_ROT0
        for hd in range(RET_HEADS):
            cols = slice(hd * RET_QK_DIM, (hd + 1) * RET_QK_DIM)
            scale = decay[side, hd]
            for blk in range(TM_IN // CHUNK):
                rows = slice(blk * CHUNK, (blk + 1) * CHUNK)
                a = prev[rows, cols]
                rot = a * cos_t[rows, :] + pltpu.roll(a, half, 1) * sin_t[rows, :]
                o_ref[rows, cols] = (rot * scale).astype(BF16)
        prev[...] = acc


def _in_projection(h, norm_w, mod, w_in_t, inv):
    L = h.shape[0]
    n_tiles = (L // TM_IN) * N_COL_TILES
    const = lambda t: (0, 0)
    mul_row = lambda t: jnp.minimum(t, n_tiles - 1) // N_COL_TILES
    mul_col = lambda t: lax.rem(jnp.minimum(t, n_tiles - 1), N_COL_TILES)
    epi = lambda t: jnp.maximum(t - 1, 0)
    table = pltpu.VMEM((TM_IN, RET_QK_DIM), F32)
    return pl.pallas_call(
        functools.partial(_inproj_kernel, n_tiles=n_tiles),
        grid=(n_tiles + 1,),
        in_specs=[pl.BlockSpec((TM_IN, D_MODEL), lambda t: (mul_row(t), 0)),
                  pl.BlockSpec((1, D_MODEL), const),
                  pl.BlockSpec((1, D_MODEL), lambda t: (0, 0)),
                  pl.BlockSpec((1, D_MODEL), lambda t: (0, 1)),
                  pl.BlockSpec((TN_IN, D_MODEL), lambda t: (mul_col(t), 0)),
                  pl.BlockSpec((SSD_HEADS, D_MODEL), lambda t: ((mul_col(t) + 1) * (TN_IN // SSD_HEADS), 0)),
                  pl.BlockSpec((LANES, D_MODEL), lambda t: (OFF_XBC // LANES, 0)),
                  pl.BlockSpec((1, RET_QK_DIM), const)],
        out_specs=[pl.BlockSpec((TM_IN, TN_IN), lambda t: (epi(t) // N_COL_TILES, lax.rem(epi(t), N_COL_TILES))),
                   pl.BlockSpec((TM_IN, LANES), lambda t: (mul_row(t), 0))],
        out_shape=[jax.ShapeDtypeStruct((L, PROJ_WIDTH), BF16),
                   jax.ShapeDtypeStruct((L, LANES), F32)],
        scratch_shapes=[pltpu.VMEM((TM_IN, D_MODEL), BF16),
                        pltpu.VMEM((TM_IN, TN_IN), F32),
                        table, table, table, table,
                        pltpu.VMEM((2, RET_HEADS, CHUNK, RET_QK_DIM), F32)],
        compiler_params=pltpu.CompilerParams(dimension_semantics=("arbitrary",),
                                             vmem_limit_bytes=60 * MIB),
        name="adaln_in_projection",
    )(h, norm_w.reshape(1, D_MODEL), mod, mod, w_in_t, w_in_t, w_in_t, inv)


def _ssd_init(xbuf, state, expand):
    xbuf[0:CONV_TAIL, :] = jnp.zeros((CONV_TAIL, SSD_CONV_DIM), F32)
    state[...] = jnp.zeros_like(state)
    head_of_lane = jnp.right_shift(lax.broadcasted_iota(jnp.int32, (LANES, SSD_D_INNER), 1), HEAD_DIM_LOG2)
    row = lax.broadcasted_iota(jnp.int32, (LANES, SSD_D_INNER), 0)
    expand[...] = jnp.where(head_of_lane == row, 1.0, 0.0).astype(BF16)


def _ssd_chunk(z_ref, x_ref, b_ref, c_ref, dtraw_ref, cw_ref, cb_ref, dtb_ref, alog_ref, dskip_ref, nw_ref,
               y_ref, xbuf, state, expand):
    int_iota = lambda shape, dim: lax.broadcasted_iota(jnp.int32, shape, dim)

    xbuf[CONV_TAIL:, 0:SSD_D_INNER] = x_ref[...].astype(F32)
    xbuf[CONV_TAIL:, SSD_D_INNER:SSD_D_INNER + SSD_BC] = b_ref[...].astype(F32)
    xbuf[CONV_TAIL:, SSD_D_INNER + SSD_BC:] = c_ref[...].astype(F32)
    acc = cb_ref[...] + cw_ref[SSD_CONV - 1:SSD_CONV, :] * xbuf[CONV_TAIL:, :]
    for d in range(1, SSD_CONV):
        acc = acc + cw_ref[SSD_CONV - 1 - d:SSD_CONV - d, :] * xbuf[CONV_TAIL - d:CONV_TAIL - d + CHUNK, :]
    xbuf[0:CONV_TAIL, :] = xbuf[CHUNK:CHUNK + CONV_TAIL, :]
    xbc = _silu(acc)
    xs = xbc[:, 0:SSD_D_INNER]
    xs_b = xs.astype(BF16)
    bmat = xbc[:, SSD_D_INNER:SSD_D_INNER + SSD_BC]
    cmat = xbc[:, SSD_D_INNER + SSD_BC:]

    pre = dtraw_ref[...] + dtb_ref[...]
    dt = jnp.maximum(pre, 0.0) + jnp.log1p(jnp.exp(-jnp.abs(pre)))
    d_a = dt * (-LOG2_E * jnp.exp(alog_ref[...]))
    causal = int_iota((CHUNK, CHUNK), 0) >= int_iota((CHUNK, CHUNK), 1)
    tril = jnp.where(causal, 1.0, 0.0).astype(BF16)
    acs = sum(_dot(tril, part) for part in _split3(d_a))
    acs_t = acs.T
    src_t = acs_t - jnp.log2(dt).T
    w_t = jnp.exp2(acs_t[:, CHUNK - 1:CHUNK] - src_t)
    end = jnp.broadcast_to(jnp.exp2(acs[CHUNK - 1:CHUNK, :]), (SUBLANES, LANES))
    end_row = sum(_dot(part, expand[...]) for part in _split3(end))[0:1, :]

    lane_blk = jnp.right_shift(int_iota((CHUNK, QUAD_W), 1), HEAD_DIM_LOG2)

    def block_diag(v):
        return jnp.concatenate([jnp.where(lane_blk == j, v, jnp.zeros_like(v)) for j in range(QUAD)], axis=0)

    def group(g, between_quads=lambda q: None):
        gs = slice(g * SSD_STATE, (g + 1) * SSD_STATE)
        b_g = bmat[:, gs]
        c_g = cmat[:, gs]
        cb_g = _dot_nt(c_g.astype(BF16), b_g.astype(BF16))
        b_gt = b_g.T
        y_parts = []
        for qi in range(HEADS_PER_GROUP // QUAD):
            q = g * (HEADS_PER_GROUP // QUAD) + qi
            between_quads(q)
            qs = slice(q * QUAD_W, (q + 1) * QUAD_W)
            lhs_diag, lhs_state, lhs_off = [], [], []
            for j in range(QUAD):
                hd = q * QUAD + j
                col = jnp.broadcast_to(acs[:, hd:hd + 1], (CHUNK, CHUNK))
                decay_dt = jnp.exp2(jnp.where(causal, col - src_t[hd:hd + 1, :], -jnp.inf))
                lhs_diag.append((cb_g * decay_dt).astype(BF16))
                lhs_state.append((b_gt * w_t[hd:hd + 1, :]).astype(BF16))
                lhs_off.append((c_g * jnp.exp2(col)).astype(BF16))
            x_blk = block_diag(xs_b[:, qs])
            s_prev = state[:, qs]
            s_blk = block_diag(s_prev.astype(BF16))
            y_q = _dot(jnp.concatenate(lhs_diag, axis=1), x_blk)
            y_q = y_q + _dot(jnp.concatenate(lhs_off, axis=1), s_blk)
            state[:, qs] = s_prev * end_row[:, qs] + _dot(jnp.concatenate(lhs_state, axis=1), x_blk)
            y_parts.append(y_q + xs[:, qs] * dskip_ref[:, qs])
        ws = slice(g * GROUP_W, (g + 1) * GROUP_W)
        u = jnp.concatenate(y_parts, axis=1) * z_ref[:, ws].astype(F32)
        u = u * lax.rsqrt(jnp.mean(u * u, axis=-1, keepdims=True) + EPS)
        y_ref[:, ws] = (u * nw_ref[:, ws]).astype(BF16)

    return group


def _ret_chunk(q_ref, k_ref, v0_ref, v1_ref, g0_ref, g1_ref, nw_ref, y_ref, state):
    heads_per_block = RET_QK_WIDTH // RET_V_DIM
    log_gammas = _log_gammas()
    causal = (lax.broadcasted_iota(jnp.int32, (CHUNK, CHUNK), 0)
              >= lax.broadcasted_iota(jnp.int32, (CHUNK, CHUNK), 1))

    def head(hd):
        ks = slice(hd * RET_QK_DIM, (hd + 1) * RET_QK_DIM)
        vs = slice(hd * RET_V_DIM, (hd + 1) * RET_V_DIM)
        v_ref, g_ref = (v0_ref, g0_ref) if hd < heads_per_block else (v1_ref, g1_ref)
        bs = slice((hd % heads_per_block) * RET_V_DIM, (hd % heads_per_block + 1) * RET_V_DIM)
        q_h = q_ref[:, ks]
        k_h = k_ref[:, ks]
        v_h = v_ref[:, bs]
        scores = jnp.where(causal, _dot_nt(q_h, k_h), 0.0)
        s_prev = state[hd]
        o = _dot(scores.astype(BF16), v_h) + _dot(q_h, s_prev.astype(BF16))
        state[hd] = (s_prev + _dot_tn(k_h, v_h)) * float(np.exp(np.float32(CHUNK * log_gammas[hd])))
        mu = jnp.mean(o, axis=-1, keepdims=True)
        d = o - mu
        y = d * lax.rsqrt(jnp.mean(d * d, axis=-1, keepdims=True) + EPS) * nw_ref[:, vs]
        y = (y * g_ref[:, bs].astype(F32)).astype(BF16)
        y_ref[:, SSD_D_INNER + hd * RET_V_DIM:SSD_D_INNER + (hd + 1) * RET_V_DIM] = y

    return head


def _mixer_kernel(z_ref, x_ref, b_ref, c_ref, dtraw_ref, cw_ref, cb_ref, dtb_ref, alog_ref, dskip_ref, snw_ref,
                  q_ref, k_ref, v0_ref, v1_ref, g0_ref, g1_ref, rnw_ref,
                  wout_hbm, h_ref, gate_ref, fw_ref, o_ref,
                  xbuf, ssd_state, expand, ret_state, ybuf, wout, wstage, wsem, *, final):
    step = pl.program_id(0)

    def wout_copy(i, slot):
        return pltpu.make_async_copy(wout_hbm.at[pl.ds(i * W_OUT_ROWS, W_OUT_ROWS), :], wstage.at[slot],
                                     wsem.at[slot])

    @pl.when(step == 0)
    def _():
        n_stage = MIX_WIDTH // W_OUT_ROWS
        wout_copy(0, 0).start()

        def stage(i, carry):
            slot = lax.rem(i, 2)

            @pl.when(i + 1 < n_stage)
            def _():
                wout_copy(i + 1, 1 - slot).start()

            wout_copy(i, slot).wait()
            wout[pl.ds(pl.multiple_of(i * W_OUT_ROWS, W_OUT_ROWS), W_OUT_ROWS), :] = wstage[slot].astype(BF16)
            return carry

        lax.fori_loop(0, n_stage, stage, 0)
        _ssd_init(xbuf, ssd_state, expand)
        ret_state[...] = jnp.zeros_like(ret_state)
        ybuf[...] = jnp.zeros_like(ybuf)

    slot = lax.rem(step, 2)
    y_old = ybuf[1 - slot]
    piece_w = D_MODEL // RET_HEADS
    pieces = []
    for sub in range(CHUNKS_PER_STEP):
        rows = pl.ds(sub * CHUNK, CHUNK)
        sub_refs = lambda *refs: [r.at[rows] for r in refs]
        y_new = ybuf.at[slot, rows]
        z_c, x_c, b_c, c_c, dtraw_c = sub_refs(z_ref, x_ref, b_ref, c_ref, dtraw_ref)
        ssd_group = _ssd_chunk(z_c, x_c, b_c, c_c, dtraw_c, cw_ref, cb_ref, dtb_ref, alog_ref, dskip_ref, snw_ref,
                               y_new, xbuf, ssd_state, expand)
        q_c, k_c, v0_c, v1_c, g0_c, g1_c = sub_refs(q_ref, k_ref, v0_ref, v1_ref, g0_ref, g1_ref)
        ret_head = _ret_chunk(q_c, k_c, v0_c, v1_c, g0_c, g1_c, rnw_ref, y_new, ret_state)

        def between_quads(q):
            if q % CHUNKS_PER_STEP == 0:
                n = len(pieces)
                pieces.append(_dot(y_old, wout[:, n * piece_w:(n + 1) * piece_w]))
            ret_head(q)

        for g in range(SSD_GROUPS):
            ssd_group(g, between_quads)

    h = h_ref[...] + gate_ref[...] * jnp.concatenate(pieces, axis=1)
    if final:
        h = h * lax.rsqrt(jnp.mean(h * h, axis=-1, keepdims=True) + EPS) * fw_ref[...]
    o_ref[...] = h


def _mix_and_project(proj, dt_raw, h, mod, conv_w, conv_b, dt_bias, a_log, d_skip, ssd_norm_w, ret_norm_w, w_out,
                     final_norm_w, final):
    L = h.shape[0]
    rows = CHUNKS_PER_STEP * CHUNK
    n_steps = L // rows
    pad = lambda v: jnp.pad(v.reshape(1, SSD_HEADS), ((0, 0), (0, LANES - SSD_HEADS)))
    full = lambda shape: pl.BlockSpec(shape, lambda s: (0, 0))
    mixed = lambda width, col: pl.BlockSpec((rows, width), lambda s: (jnp.minimum(s, n_steps - 1), col))
    projected = lambda width, col: pl.BlockSpec((rows, width), lambda s: (jnp.maximum(s - 1, 0), col))
    return pl.pallas_call(
        functools.partial(_mixer_kernel, final=final),
        grid=(n_steps + 1,),
        in_specs=[mixed(SSD_D_INNER, COL_Z), mixed(SSD_D_INNER, COL_X), mixed(SSD_BC, COL_B), mixed(SSD_BC, COL_C),
                  mixed(LANES, 0),
                  full((SSD_CONV, SSD_CONV_DIM)), full((1, SSD_CONV_DIM)), full((1, LANES)), full((1, LANES)),
                  full((1, SSD_D_INNER)), full((1, SSD_D_INNER)),
                  mixed(RET_QK_WIDTH, COL_Q), mixed(RET_QK_WIDTH, COL_K),
                  mixed(RET_QK_WIDTH, COL_V), mixed(RET_QK_WIDTH, COL_V + 1),
                  mixed(RET_QK_WIDTH, COL_G), mixed(RET_QK_WIDTH, COL_G + 1),
                  full((1, RET_V_WIDTH)),
                  pl.BlockSpec(memory_space=pl.ANY),
                  projected(D_MODEL, 0),
                  pl.BlockSpec((1, D_MODEL), lambda s: (0, 2)),
                  full((1, D_MODEL))],
        out_specs=projected(D_MODEL, 0),
        out_shape=jax.ShapeDtypeStruct((L, D_MODEL), F32),
        scratch_shapes=[pltpu.VMEM((CONV_TAIL + CHUNK, SSD_CONV_DIM), F32),
                        pltpu.VMEM((SSD_STATE, SSD_D_INNER), F32),
                        pltpu.VMEM((LANES, SSD_D_INNER), BF16),
                        pltpu.VMEM((RET_HEADS, RET_QK_DIM, RET_V_DIM), F32),
                        pltpu.VMEM((2, rows, MIX_WIDTH), BF16),
                        pltpu.VMEM((MIX_WIDTH, D_MODEL), BF16),
                        pltpu.VMEM((2, W_OUT_ROWS, D_MODEL), F32),
                        pltpu.SemaphoreType.DMA((2,))],
        compiler_params=pltpu.CompilerParams(dimension_semantics=("arbitrary",),
                                             vmem_limit_bytes=60 * MIB),
        name="mixers_out_projection",
    )(proj, proj, proj, proj, dt_raw, conv_w, conv_b.reshape(1, SSD_CONV_DIM), pad(dt_bias), pad(a_log),
      jnp.repeat(d_skip, SSD_HEAD_DIM).reshape(1, SSD_D_INNER), ssd_norm_w.reshape(1, SSD_D_INNER),
      proj, proj, proj, proj, proj, proj, ret_norm_w.reshape(1, RET_V_WIDTH),
      w_out, h, mod, final_norm_w.reshape(1, D_MODEL))


def kernel(x, c, w_ada, b_ada, norm_w, w_in, conv_w, conv_b, dt_bias, a_log, d_skip, ssd_norm_w, ret_norm_w,
           w_out, final_norm_w):
    bsz, L, d_model = x.shape
    assert bsz == 1 and d_model == D_MODEL and L % TM_IN == 0 and w_in.shape[-1] == IN_WIDTH
    depth = w_in.shape[0]
    half = RET_QK_DIM // 2
    inv = ROPE_BASE ** (-jnp.arange(half, dtype=F32) / half)
    inv = jnp.concatenate([inv, inv]).reshape(1, RET_QK_DIM)
    h = x.reshape(L, D_MODEL)
    for layer in range(depth):
        mod = _modulation(c, w_ada[layer], b_ada[layer])
        proj, dt_raw = _in_projection(h, norm_w[layer], mod, jnp.swapaxes(w_in[layer], 0, 1), inv)
        h = _mix_and_project(proj, dt_raw, h, mod, conv_w[layer], conv_b[layer], dt_bias[layer], a_log[layer],
                             d_skip[layer], ssd_norm_w[layer], ret_norm_w[layer], w_out[layer], final_norm_w,
                             final=layer == depth - 1)
    return h.reshape(bsz, L, D_MODEL)
```

```python
import functools
import math

import numpy as np
import jax
import jax.numpy as jnp
from jax import lax
from jax.experimental import pallas as pl
from jax.experimental.pallas import tpu as pltpu

D_MODEL = 2048
SSD_D_INNER = D_MODEL
SSD_HEAD_DIM = 64
SSD_HEADS = SSD_D_INNER // SSD_HEAD_DIM
HEAD_DIM_LOG2 = SSD_HEAD_DIM.bit_length() - 1
SSD_GROUPS = 4
SSD_STATE = 128
SSD_CONV = 4
SSD_BC = SSD_GROUPS * SSD_STATE
SSD_CONV_DIM = SSD_D_INNER + 2 * SSD_BC
RET_HEADS = 8
RET_QK_DIM = 128
RET_V_DIM = 256
RET_QK_WIDTH = RET_HEADS * RET_QK_DIM
RET_V_WIDTH = RET_HEADS * RET_V_DIM
MIX_WIDTH = SSD_D_INNER + RET_V_WIDTH
CHUNK = 128
ROPE_BASE = 10000.0
EPS = 1e-6
LOG2_E = math.log2(math.e)

OFF_Z = SSD_D_INNER
OFF_XBC = OFF_Z + SSD_CONV_DIM
OFF_DT = OFF_XBC + SSD_HEADS
OFF_Q = OFF_DT + RET_QK_WIDTH
OFF_K = OFF_Q + RET_QK_WIDTH
OFF_V = OFF_K + RET_V_WIDTH
IN_WIDTH = OFF_V + RET_V_WIDTH

PROJ_WIDTH = IN_WIDTH - SSD_HEADS
COL_Z, COL_X = 0, 1
COL_B, COL_C = 8, 9
COL_Q, COL_K, COL_V, COL_G = 5, 6, 7, 9

LANES = 128
SUBLANES = 8
QUAD = 4
QUAD_W = QUAD * SSD_HEAD_DIM
HEADS_PER_GROUP = SSD_HEADS // SSD_GROUPS
GROUP_W = SSD_D_INNER // SSD_GROUPS
CONV_TAIL = SUBLANES
CHUNKS_PER_STEP = 2
W_OUT_ROWS = 256

TM_IN, TN_IN = 1024, 1024
TILE_CONV0, TILE_ROT0, TILE_V0, TILE_G0 = (c // TN_IN for c in (OFF_Z, OFF_XBC, OFF_XBC + 2 * RET_QK_WIDTH,
                                                                 OFF_XBC + 2 * RET_QK_WIDTH + RET_V_WIDTH))
N_COL_TILES = PROJ_WIDTH // TN_IN
assert OFF_Z % TN_IN == 0 and OFF_XBC % TN_IN == 0 and RET_QK_WIDTH == TN_IN and RET_V_WIDTH % TN_IN == 0
assert PROJ_WIDTH % TN_IN == 0 and TM_IN % CHUNK == 0 and N_COL_TILES - 1 >= TILE_G0
MIB = 1024 * 1024

F32 = jnp.float32
BF16 = jnp.bfloat16


def _silu(v):
    return v / (1.0 + jnp.exp(-v))


def _dot(a, b):
    return jnp.dot(a, b, preferred_element_type=F32)


def _dot_nt(a, b):
    return lax.dot_general(a, b, (((1,), (1,)), ((), ())), preferred_element_type=F32)


def _dot_tn(a, b):
    return lax.dot_general(a, b, (((0,), (0,)), ((), ())), preferred_element_type=F32)


def _split3(v):
    hi = v.astype(BF16)
    r1 = v - hi.astype(F32)
    mid = r1.astype(BF16)
    lo = (r1 - mid.astype(F32)).astype(BF16)
    return hi, mid, lo


def _log_gammas():
    return [float(np.log1p(-np.exp2(np.float32(-5.0 - hd)), dtype=np.float32)) for hd in range(RET_HEADS)]


def _mod_kernel(c_ref, w_ref, b_ref, o_ref):
    cond = jnp.broadcast_to(_silu(c_ref[...]), (SUBLANES, D_MODEL))
    o_ref[...] = _dot(cond, w_ref[...])[0:1, :] + b_ref[...]


def _modulation(c, w_ada, b_ada):
    n = w_ada.shape[1]
    tn = 512
    return pl.pallas_call(
        _mod_kernel,
        grid=(n // tn,),
        in_specs=[pl.BlockSpec((1, D_MODEL), lambda j: (0, 0)),
                  pl.BlockSpec((D_MODEL, tn), lambda j: (0, j)),
                  pl.BlockSpec((1, tn), lambda j: (0, j))],
        out_specs=pl.BlockSpec((1, tn), lambda j: (0, j)),
        out_shape=jax.ShapeDtypeStruct((1, n), F32),
        compiler_params=pltpu.CompilerParams(dimension_semantics=("arbitrary",),
                                             vmem_limit_bytes=24 * MIB),
        name="adaln_modulation",
    )(c, w_ada, b_ada.reshape(1, n))


def _inproj_kernel(x_ref, nw_ref, shift_ref, scale_ref, wa_ref, wb_ref, wd_ref, inv_ref,
                   o_ref, dt_ref, u_ref, prev, cos_in, sin_in, cos_t, sin_t, decay, *, n_tiles):
    t = pl.program_id(0)
    tile = jnp.minimum(t, n_tiles - 1)
    i = tile // N_COL_TILES
    j = tile - i * N_COL_TILES
    je = lax.rem(jnp.maximum(t - 1, 0), N_COL_TILES)
    half = RET_QK_DIM // 2

    @pl.when(t == 0)
    def _():
        prev[...] = jnp.zeros_like(prev)
        ang = lax.broadcasted_iota(jnp.int32, (TM_IN, RET_QK_DIM), 0).astype(F32) * inv_ref[...]
        cos_in[...] = jnp.cos(ang)
        sin_in[...] = jnp.sin(ang)
        pos = lax.broadcasted_iota(jnp.int32, (CHUNK, RET_QK_DIM), 0).astype(F32) + 1.0
        for hd, lg in enumerate(_log_gammas()):
            decay[0, hd] = jnp.exp(pos * lg)
            decay[1, hd] = jnp.exp(pos * -lg) * RET_QK_DIM ** -0.5

    @pl.when(jnp.logical_and(j == 0, t < n_tiles))
    def _():
        x = x_ref[...]
        y = x * lax.rsqrt(jnp.mean(x * x, axis=-1, keepdims=True) + EPS) * nw_ref[...]
        u = (y * (1.0 + scale_ref[...]) + shift_ref[...]).astype(BF16)
        u_ref[...] = u
        is_dt = lax.broadcasted_iota(jnp.int32, (LANES, D_MODEL), 0) < SSD_HEADS
        dt_ref[...] = _dot_nt(u, jnp.where(is_dt, wd_ref[...], 0.0).astype(BF16))

        base = jnp.broadcast_to((i * TM_IN).astype(F32) * inv_ref[...], (SUBLANES, RET_QK_DIM))
        cos_b = jnp.cos(base)[0:1, :]
        sin_b = jnp.sin(base)[0:1, :]
        cos_t[...] = cos_in[...] * cos_b - sin_in[...] * sin_b
        sin_pos = sin_in[...] * cos_b + cos_in[...] * sin_b
        first_half = lax.broadcasted_iota(jnp.int32, (TM_IN, RET_QK_DIM), 1) < half
        sin_t[...] = jnp.where(first_half, -sin_pos, sin_pos)

    def project():
        after_dt = j >= TILE_ROT0
        skip = pl.multiple_of(jnp.where(after_dt, SSD_HEADS, 0), SSD_HEADS)
        head = wa_ref[pl.ds(skip, TN_IN - SSD_HEADS), :]
        tail = jnp.where(after_dt, wb_ref[...], wa_ref[TN_IN - SSD_HEADS:, :])
        w = jnp.concatenate([head, tail], axis=0).astype(BF16)
        return _dot_nt(u_ref[...], w)

    @pl.when(jnp.logical_and(jnp.logical_or(je < TILE_CONV0, je >= TILE_G0), t < n_tiles))
    def _():
        acc = project()
        o_ref[...] = _silu(prev[...]).astype(BF16)
        prev[...] = acc

    @pl.when(t == n_tiles)
    def _():
        o_ref[...] = _silu(prev[...]).astype(BF16)

    @pl.when(jnp.logical_or(jnp.logical_and(je >= TILE_CONV0, je < TILE_ROT0),
                            jnp.logical_and(je >= TILE_V0, je < TILE_G0)))
    def _():
        acc = project()
        o_ref[...] = prev[...].astype(BF16)
        prev[...] = acc

    @pl.when(jnp.logical_and(je >= TILE_ROT0, je < TILE_V0))
    def _():
        acc = project()
        side = je - TILE_ROT0
        for hd in range(RET_HEADS):
            cols = slice(hd * RET_QK_DIM, (hd + 1) * RET_QK_DIM)
            scale = decay[side, hd]
            for blk in range(TM_IN // CHUNK):
                rows = slice(blk * CHUNK, (blk + 1) * CHUNK)
                a = prev[rows, cols]
                rot = a * cos_t[rows, :] + pltpu.roll(a, half, 1) * sin_t[rows, :]
                o_ref[rows, cols] = (rot * scale).astype(BF16)
        prev[...] = acc


def _in_projection(h, norm_w, mod, w_in_t, inv):
    L = h.shape[0]
    n_tiles = (L // TM_IN) * N_COL_TILES
    const = lambda t: (0, 0)
    mul_row = lambda t: jnp.minimum(t, n_tiles - 1) // N_COL_TILES
    mul_col = lambda t: lax.rem(jnp.minimum(t, n_tiles - 1), N_COL_TILES)
    epi = lambda t: jnp.maximum(t - 1, 0)
    table = pltpu.VMEM((TM_IN, RET_QK_DIM), F32)
    return pl.pallas_call(
        functools.partial(_inproj_kernel, n_tiles=n_tiles),
        grid=(n_tiles + 1,),
        in_specs=[pl.BlockSpec((TM_IN, D_MODEL), lambda t: (mul_row(t), 0)),
                  pl.BlockSpec((1, D_MODEL), const),
                  pl.BlockSpec((1, D_MODEL), lambda t: (0, 0)),
                  pl.BlockSpec((1, D_MODEL), lambda t: (0, 1)),
                  pl.BlockSpec((TN_IN, D_MODEL), lambda t: (mul_col(t), 0)),
                  pl.BlockSpec((SSD_HEADS, D_MODEL), lambda t: ((mul_col(t) + 1) * (TN_IN // SSD_HEADS), 0)),
                  pl.BlockSpec((LANES, D_MODEL), lambda t: (OFF_XBC // LANES, 0)),
                  pl.BlockSpec((1, RET_QK_DIM), const)],
        out_specs=[pl.BlockSpec((TM_IN, TN_IN), lambda t: (epi(t) // N_COL_TILES, lax.rem(epi(t), N_COL_TILES))),
                   pl.BlockSpec((TM_IN, LANES), lambda t: (mul_row(t), 0))],
        out_shape=[jax.ShapeDtypeStruct((L, PROJ_WIDTH), BF16),
                   jax.ShapeDtypeStruct((L, LANES), F32)],
        scratch_shapes=[pltpu.VMEM((TM_IN, D_MODEL), BF16),
                        pltpu.VMEM((TM_IN, TN_IN), F32),
                        table, table, table, table,
                        pltpu.VMEM((2, RET_HEADS, CHUNK, RET_QK_DIM), F32)],
        compiler_params=pltpu.CompilerParams(dimension_semantics=("arbitrary",),
                                             vmem_limit_bytes=60 * MIB),
        name="adaln_in_projection",
    )(h, norm_w.reshape(1, D_MODEL), mod, mod, w_in_t, w_in_t, w_in_t, inv)


def _ssd_init(xbuf, state, expand):
    xbuf[...] = jnp.zeros_like(xbuf)
    state[...] = jnp.zeros_like(state)
    head_of_lane = jnp.right_shift(lax.broadcasted_iota(jnp.int32, (LANES, SSD_D_INNER), 1), HEAD_DIM_LOG2)
    row = lax.broadcasted_iota(jnp.int32, (LANES, SSD_D_INNER), 0)
    expand[...] = jnp.where(head_of_lane == row, 1.0, 0.0).astype(BF16)


def _ssd_chunk(z_ref, x_ref, b_ref, c_ref, dtraw_ref, cw_ref, cb_ref, dtb_ref, alog_ref, dskip_ref, nw_ref,
               y_ref, xbuf, state, expand):
    int_iota = lambda shape, dim: lax.broadcasted_iota(jnp.int32, shape, dim)

    xbuf[0:CONV_TAIL, :] = xbuf[CHUNK:CHUNK + CONV_TAIL, :]
    xbuf[CONV_TAIL:, 0:SSD_D_INNER] = x_ref[...].astype(F32)
    xbuf[CONV_TAIL:, SSD_D_INNER:SSD_D_INNER + SSD_BC] = b_ref[...].astype(F32)
    xbuf[CONV_TAIL:, SSD_D_INNER + SSD_BC:] = c_ref[...].astype(F32)

    def conv_silu(cols):
        acc = cb_ref[:, cols] + cw_ref[SSD_CONV - 1:SSD_CONV, cols] * xbuf[CONV_TAIL:, cols]
        for d in range(1, SSD_CONV):
            acc = acc + cw_ref[SSD_CONV - 1 - d:SSD_CONV - d, cols] * xbuf[CONV_TAIL - d:CONV_TAIL - d + CHUNK, cols]
        return _silu(acc)

    pre = dtraw_ref[...] + dtb_ref[...]
    dt = jnp.maximum(pre, 0.0) + jnp.log1p(jnp.exp(-jnp.abs(pre)))
    d_a = dt * (-LOG2_E * jnp.exp(alog_ref[...]))
    causal = int_iota((CHUNK, CHUNK), 0) >= int_iota((CHUNK, CHUNK), 1)
    tril = jnp.where(causal, 1.0, 0.0).astype(BF16)
    acs = sum(_dot(tril, part) for part in _split3(d_a))
    acs_t = acs.T
    src_t = acs_t - jnp.log2(dt).T
    w_t = jnp.exp2(acs_t[:, CHUNK - 1:CHUNK] - src_t)
    end = jnp.broadcast_to(jnp.exp2(acs[CHUNK - 1:CHUNK, :]), (SUBLANES, LANES))
    end_row = sum(_dot(part, expand[...]) for part in _split3(end))[0:1, :]

    lane_blk = jnp.right_shift(int_iota((CHUNK, QUAD_W), 1), HEAD_DIM_LOG2)

    def block_diag(v):
        return jnp.concatenate([jnp.where(lane_blk == j, v, jnp.zeros_like(v)) for j in range(QUAD)], axis=0)

    def group(g, between_quads=lambda q: None):
        gs = slice(g * SSD_STATE, (g + 1) * SSD_STATE)
        b_g = conv_silu(slice(SSD_D_INNER + g * SSD_STATE, SSD_D_INNER + (g + 1) * SSD_STATE))
        c_g = conv_silu(slice(SSD_D_INNER + SSD_BC + g * SSD_STATE, SSD_D_INNER + SSD_BC + (g + 1) * SSD_STATE))
        cb_g = _dot_nt(c_g.astype(BF16), b_g.astype(BF16))
        b_gt = b_g.T
        y_parts = []
        for qi in range(HEADS_PER_GROUP // QUAD):
            q = g * (HEADS_PER_GROUP // QUAD) + qi
            between_quads(q)
            qs = slice(q * QUAD_W, (q + 1) * QUAD_W)
            lhs_diag, lhs_state, lhs_off = [], [], []
            for j in range(QUAD):
                hd = q * QUAD + j
                col = jnp.broadcast_to(acs[:, hd:hd + 1], (CHUNK, CHUNK))
                decay_dt = jnp.exp2(jnp.where(causal, col - src_t[hd:hd + 1, :], -jnp.inf))
                lhs_diag.append((cb_g * decay_dt).astype(BF16))
                lhs_state.append((b_gt * w_t[hd:hd + 1, :]).astype(BF16))
                lhs_off.append((c_g * jnp.exp2(col)).astype(BF16))
            xs = conv_silu(qs)
            x_blk = block_diag(xs.astype(BF16))
            s_prev = state[:, qs]
            s_blk = block_diag(s_prev.astype(BF16))
            y_q = _dot(jnp.concatenate(lhs_diag, axis=1), x_blk)
            y_q = y_q + _dot(jnp.concatenate(lhs_off, axis=1), s_blk)
            state[:, qs] = s_prev * end_row[:, qs] + _dot(jnp.concatenate(lhs_state, axis=1), x_blk)
            y_parts.append(y_q + xs * dskip_ref[:, qs])
        ws = slice(g * GROUP_W, (g + 1) * GROUP_W)
        u = jnp.concatenate(y_parts, axis=1) * z_ref[:, ws].astype(F32)
        u = u * lax.rsqrt(jnp.mean(u * u, axis=-1, keepdims=True) + EPS)
        y_ref[:, ws] = (u * nw_ref[:, ws]).astype(BF16)

    return group


def _ret_chunk(q_ref, k_ref, v0_ref, v1_ref, g0_ref, g1_ref, nw_ref, y_ref, state):
    heads_per_block = RET_QK_WIDTH // RET_V_DIM
    log_gammas = _log_gammas()
    causal = (lax.broadcasted_iota(jnp.int32, (CHUNK, CHUNK), 0)
              >= lax.broadcasted_iota(jnp.int32, (CHUNK, CHUNK), 1))

    def head(hd):
        ks = slice(hd * RET_QK_DIM, (hd + 1) * RET_QK_DIM)
        vs = slice(hd * RET_V_DIM, (hd + 1) * RET_V_DIM)
        v_ref, g_ref = (v0_ref, g0_ref) if hd < heads_per_block else (v1_ref, g1_ref)
        bs = slice((hd % heads_per_block) * RET_V_DIM, (hd % heads_per_block + 1) * RET_V_DIM)
        q_h = q_ref[:, ks]
        k_h = k_ref[:, ks]
        v_h = v_ref[:, bs]
        scores = jnp.where(causal, _dot_nt(q_h, k_h), 0.0)
        s_prev = state[hd]
        o = _dot(scores.astype(BF16), v_h) + _dot(q_h, s_prev.astype(BF16))
        state[hd] = (s_prev + _dot_tn(k_h, v_h)) * float(np.exp(np.float32(CHUNK * log_gammas[hd])))
        mu = jnp.mean(o, axis=-1, keepdims=True)
        d = o - mu
        y = d * lax.rsqrt(jnp.mean(d * d, axis=-1, keepdims=True) + EPS) * nw_ref[:, vs]
        y = (y * g_ref[:, bs].astype(F32)).astype(BF16)
        y_ref[:, SSD_D_INNER + hd * RET_V_DIM:SSD_D_INNER + (hd + 1) * RET_V_DIM] = y

    return head


def _mixer_kernel(z_ref, x_ref, b_ref, c_ref, dtraw_ref, cw_ref, cb_ref, dtb_ref, alog_ref, dskip_ref, snw_ref,
                  q_ref, k_ref, v0_ref, v1_ref, g0_ref, g1_ref, rnw_ref,
                  wout_hbm, h_ref, gate_ref, fw_ref, o_ref,
                  xbuf, ssd_state, expand, ret_state, ybuf, wout, wstage, wsem, *, final):
    step = pl.program_id(0)

    def wout_copy(i, slot):
        return pltpu.make_async_copy(wout_hbm.at[pl.ds(i * W_OUT_ROWS, W_OUT_ROWS), :], wstage.at[slot],
                                     wsem.at[slot])

    @pl.when(step == 0)
    def _():
        n_stage = MIX_WIDTH // W_OUT_ROWS
        wout_copy(0, 0).start()

        def stage(i, carry):
            slot = lax.rem(i, 2)

            @pl.when(i + 1 < n_stage)
            def _():
                wout_copy(i + 1, 1 - slot).start()

            wout_copy(i, slot).wait()
            wout[pl.ds(pl.multiple_of(i * W_OUT_ROWS, W_OUT_ROWS), W_OUT_ROWS), :] = wstage[slot].astype(BF16)
            return carry

        lax.fori_loop(0, n_stage, stage, 0)
        _ssd_init(xbuf, ssd_state, expand)
        ret_state[...] = jnp.zeros_like(ret_state)
        ybuf[...] = jnp.zeros_like(ybuf)

    slot = lax.rem(step, 2)
    y_old = ybuf[1 - slot]
    piece_w = D_MODEL // RET_HEADS
    pieces = []
    for sub in range(CHUNKS_PER_STEP):
        rows = pl.ds(sub * CHUNK, CHUNK)
        sub_refs = lambda *refs: [r.at[rows] for r in refs]
        y_new = ybuf.at[slot, rows]
        z_c, x_c, b_c, c_c, dtraw_c = sub_refs(z_ref, x_ref, b_ref, c_ref, dtraw_ref)
        ssd_group = _ssd_chunk(z_c, x_c, b_c, c_c, dtraw_c, cw_ref, cb_ref, dtb_ref, alog_ref, dskip_ref, snw_ref,
                               y_new, xbuf, ssd_state, expand)
        q_c, k_c, v0_c, v1_c, g0_c, g1_c = sub_refs(q_ref, k_ref, v0_ref, v1_ref, g0_ref, g1_ref)
        ret_head = _ret_chunk(q_c, k_c, v0_c, v1_c, g0_c, g1_c, rnw_ref, y_new, ret_state)

        def between_quads(q):
            if q % CHUNKS_PER_STEP == 0:
                n = len(pieces)
                pieces.append(_dot(y_old, wout[:, n * piece_w:(n + 1) * piece_w]))
            ret_head(q)

        for g in range(SSD_GROUPS):
            ssd_group(g, between_quads)

    h = h_ref[...] + gate_ref[...] * jnp.concatenate(pieces, axis=1)
    if final:
        h = h * lax.rsqrt(jnp.mean(h * h, axis=-1, keepdims=True) + EPS) * fw_ref[...]
    o_ref[...] = h


def _mix_and_project(proj, dt_raw, h, mod, conv_w, conv_b, dt_bias, a_log, d_skip, ssd_norm_w, ret_norm_w, w_out,
                     final_norm_w, final):
    L = h.shape[0]
    rows = CHUNKS_PER_STEP * CHUNK
    n_steps = L // rows
    pad = lambda v: jnp.pad(v.reshape(1, SSD_HEADS), ((0, 0), (0, LANES - SSD_HEADS)))
    full = lambda shape: pl.BlockSpec(shape, lambda s: (0, 0))
    mixed = lambda width, col: pl.BlockSpec((rows, width), lambda s: (jnp.minimum(s, n_steps - 1), col))
    projected = lambda width, col: pl.BlockSpec((rows, width), lambda s: (jnp.maximum(s - 1, 0), col))
    return pl.pallas_call(
        functools.partial(_mixer_kernel, final=final),
        grid=(n_steps + 1,),
        in_specs=[mixed(SSD_D_INNER, COL_Z), mixed(SSD_D_INNER, COL_X), mixed(SSD_BC, COL_B), mixed(SSD_BC, COL_C),
                  mixed(LANES, 0),
                  full((SSD_CONV, SSD_CONV_DIM)), full((1, SSD_CONV_DIM)), full((1, LANES)), full((1, LANES)),
                  full((1, SSD_D_INNER)), full((1, SSD_D_INNER)),
                  mixed(RET_QK_WIDTH, COL_Q), mixed(RET_QK_WIDTH, COL_K),
                  mixed(RET_QK_WIDTH, COL_V), mixed(RET_QK_WIDTH, COL_V + 1),
                  mixed(RET_QK_WIDTH, COL_G), mixed(RET_QK_WIDTH, COL_G + 1),
                  full((1, RET_V_WIDTH)),
                  pl.BlockSpec(memory_space=pl.ANY),
                  projected(D_MODEL, 0),
                  pl.BlockSpec((1, D_MODEL), lambda s: (0, 2)),
                  full((1, D_MODEL))],
        out_specs=projected(D_MODEL, 0),
        out_shape=jax.ShapeDtypeStruct((L, D_MODEL), F32),
        scratch_shapes=[pltpu.VMEM((CONV_TAIL + CHUNK, SSD_CONV_DIM), F32),
                        pltpu.VMEM((SSD_STATE, SSD_D_INNER), F32),
                        pltpu.VMEM((LANES, SSD_D_INNER), BF16),
                        pltpu.VMEM((RET_HEADS, RET_QK_DIM, RET_V_DIM), F32),
                        pltpu.VMEM((2, rows, MIX_WIDTH), BF16),
                        pltpu.VMEM((MIX_WIDTH, D_MODEL), BF16),
                        pltpu.VMEM((2, W_OUT_ROWS, D_MODEL), F32),
                        pltpu.SemaphoreType.DMA((2,))],
        compiler_params=pltpu.CompilerParams(dimension_semantics=("arbitrary",),
                                             vmem_limit_bytes=60 * MIB),
        name="mixers_out_projection",
    )(proj, proj, proj, proj, dt_raw, conv_w, conv_b.reshape(1, SSD_CONV_DIM), pad(dt_bias), pad(a_log),
      jnp.repeat(d_skip, SSD_HEAD_DIM).reshape(1, SSD_D_INNER), ssd_norm_w.reshape(1, SSD_D_INNER),
      proj, proj, proj, proj, proj, proj, ret_norm_w.reshape(1, RET_V_WIDTH),
      w_out, h, mod, final_norm_w.reshape(1, D_MODEL))


def kernel(x, c, w_ada, b_ada, norm_w, w_in, conv_w, conv_b, dt_bias, a_log, d_skip, ssd_norm_w, ret_norm_w,
           w_out, final_norm_w):
    bsz, L, d_model = x.shape
    assert bsz == 1 and d_model == D_MODEL and L % TM_IN == 0 and w_in.shape[-1] == IN_WIDTH
    depth = w_in.shape[0]
    half = RET_QK_DIM // 2
    inv = ROPE_BASE ** (-jnp.arange(half, dtype=F32) / half)
    inv = jnp.concatenate([inv, inv]).reshape(1, RET_QK_DIM)
    h = x.reshape(L, D_MODEL)
    for layer in range(depth):
        mod = _modulation(c, w_ada[layer], b_ada[layer])
        proj, dt_raw = _in_projection(h, norm_w[layer], mod, jnp.swapaxes(w_in[layer], 0, 1), inv)
        h = _mix_and_project(proj, dt_raw, h, mod, conv_w[layer], conv_b[layer], dt_bias[layer], a_log[layer],
                             d_skip[layer], ssd_norm_w[layer], ret_norm_w[layer], w_out[layer], final_norm_w,
                             final=layer == depth - 1)
    return h.reshape(bsz, L, D_MODEL)
```

```python
import functools
import math

import numpy as np
import jax
import jax.numpy as jnp
from jax import lax
from jax.experimental import pallas as pl
from jax.experimental.pallas import tpu as pltpu

D_MODEL = 2048
SSD_D_INNER = D_MODEL
SSD_HEAD_DIM = 64
SSD_HEADS = SSD_D_INNER // SSD_HEAD_DIM
HEAD_DIM_LOG2 = SSD_HEAD_DIM.bit_length() - 1
SSD_GROUPS = 4
SSD_STATE = 128
SSD_CONV = 4
SSD_BC = SSD_GROUPS * SSD_STATE
SSD_CONV_DIM = SSD_D_INNER + 2 * SSD_BC
RET_HEADS = 8
RET_QK_DIM = 128
RET_V_DIM = 256
RET_QK_WIDTH = RET_HEADS * RET_QK_DIM
RET_V_WIDTH = RET_HEADS * RET_V_DIM
MIX_WIDTH = SSD_D_INNER + RET_V_WIDTH
CHUNK = 128
ROPE_BASE = 10000.0
EPS = 1e-6
LOG2_E = math.log2(math.e)

OFF_Z = SSD_D_INNER
OFF_XBC = OFF_Z + SSD_CONV_DIM
OFF_DT = OFF_XBC + SSD_HEADS
OFF_Q = OFF_DT + RET_QK_WIDTH
OFF_K = OFF_Q + RET_QK_WIDTH
OFF_V = OFF_K + RET_V_WIDTH
IN_WIDTH = OFF_V + RET_V_WIDTH

PROJ_WIDTH = IN_WIDTH - SSD_HEADS
COL_Z, COL_X = 0, 1
COL_B, COL_C = 8, 9
COL_Q, COL_K, COL_V, COL_G = 5, 6, 7, 9

LANES = 128
SUBLANES = 8
QUAD = 4
QUAD_W = QUAD * SSD_HEAD_DIM
HEADS_PER_GROUP = SSD_HEADS // SSD_GROUPS
GROUP_W = SSD_D_INNER // SSD_GROUPS
CONV_TAIL = 16
CHUNKS_PER_STEP = 2
W_OUT_ROWS = 256

TM_IN, TN_IN = 1024, 1024
TILE_CONV0, TILE_ROT0, TILE_V0, TILE_G0 = (c // TN_IN for c in (OFF_Z, OFF_XBC, OFF_XBC + 2 * RET_QK_WIDTH,
                                                                 OFF_XBC + 2 * RET_QK_WIDTH + RET_V_WIDTH))
N_COL_TILES = PROJ_WIDTH // TN_IN
assert OFF_Z % TN_IN == 0 and OFF_XBC % TN_IN == 0 and RET_QK_WIDTH == TN_IN and RET_V_WIDTH % TN_IN == 0
assert PROJ_WIDTH % TN_IN == 0 and TM_IN % CHUNK == 0 and N_COL_TILES - 1 >= TILE_G0
MIB = 1024 * 1024

F32 = jnp.float32
BF16 = jnp.bfloat16


def _silu(v):
    return v / (1.0 + jnp.exp(-v))


def _dot(a, b):
    return jnp.dot(a, b, preferred_element_type=F32)


def _dot_nt(a, b):
    return lax.dot_general(a, b, (((1,), (1,)), ((), ())), preferred_element_type=F32)


def _dot_tn(a, b):
    return lax.dot_general(a, b, (((0,), (0,)), ((), ())), preferred_element_type=F32)


def _split3(v):
    hi = v.astype(BF16)
    r1 = v - hi.astype(F32)
    mid = r1.astype(BF16)
    lo = (r1 - mid.astype(F32)).astype(BF16)
    return hi, mid, lo


def _log_gammas():
    return [float(np.log1p(-np.exp2(np.float32(-5.0 - hd)), dtype=np.float32)) for hd in range(RET_HEADS)]


def _mod_kernel(c_ref, w_ref, b_ref, o_ref):
    cond = jnp.broadcast_to(_silu(c_ref[...]), (SUBLANES, D_MODEL))
    o_ref[...] = _dot(cond, w_ref[...])[0:1, :] + b_ref[...]


def _modulation(c, w_ada, b_ada):
    n = w_ada.shape[1]
    tn = 512
    return pl.pallas_call(
        _mod_kernel,
        grid=(n // tn,),
        in_specs=[pl.BlockSpec((1, D_MODEL), lambda j: (0, 0)),
                  pl.BlockSpec((D_MODEL, tn), lambda j: (0, j)),
                  pl.BlockSpec((1, tn), lambda j: (0, j))],
        out_specs=pl.BlockSpec((1, tn), lambda j: (0, j)),
        out_shape=jax.ShapeDtypeStruct((1, n), F32),
        compiler_params=pltpu.CompilerParams(dimension_semantics=("arbitrary",),
                                             vmem_limit_bytes=24 * MIB),
        name="adaln_modulation",
    )(c, w_ada, b_ada.reshape(1, n))


def _inproj_kernel(x_ref, nw_ref, shift_ref, scale_ref, wa_ref, wb_ref, wd_ref, inv_ref,
                   o_ref, dt_ref, u_ref, prev, cos_in, sin_in, cos_t, sin_t, decay, *, n_tiles):
    t = pl.program_id(0)
    tile = jnp.minimum(t, n_tiles - 1)
    i = tile // N_COL_TILES
    j = tile - i * N_COL_TILES
    je = lax.rem(jnp.maximum(t - 1, 0), N_COL_TILES)
    half = RET_QK_DIM // 2

    @pl.when(t == 0)
    def _():
        prev[...] = jnp.zeros_like(prev)
        ang = lax.broadcasted_iota(jnp.int32, (TM_IN, RET_QK_DIM), 0).astype(F32) * inv_ref[...]
        cos_in[...] = jnp.cos(ang)
        sin_in[...] = jnp.sin(ang)
        pos = lax.broadcasted_iota(jnp.int32, (CHUNK, RET_QK_DIM), 0).astype(F32) + 1.0
        for hd, lg in enumerate(_log_gammas()):
            decay[0, hd] = jnp.exp(pos * lg)
            decay[1, hd] = jnp.exp(pos * -lg) * RET_QK_DIM ** -0.5

    @pl.when(jnp.logical_and(j == 0, t < n_tiles))
    def _():
        x = x_ref[...]
        gain = nw_ref[...] * (1.0 + scale_ref[...])
        u = (x * lax.rsqrt(jnp.mean(x * x, axis=-1, keepdims=True) + EPS) * gain + shift_ref[...]).astype(BF16)
        u_ref[...] = u
        is_dt = lax.broadcasted_iota(jnp.int32, (LANES, D_MODEL), 0) < SSD_HEADS
        dt_ref[...] = _dot_nt(u, jnp.where(is_dt, wd_ref[...], 0.0).astype(BF16))

        base = jnp.broadcast_to((i * TM_IN).astype(F32) * inv_ref[...], (SUBLANES, RET_QK_DIM))
        cos_b = jnp.cos(base)[0:1, :]
        sin_b = jnp.sin(base)[0:1, :]
        cos_t[...] = cos_in[...] * cos_b - sin_in[...] * sin_b
        sin_pos = sin_in[...] * cos_b + cos_in[...] * sin_b
        first_half = lax.broadcasted_iota(jnp.int32, (TM_IN, RET_QK_DIM), 1) < half
        sin_t[...] = jnp.where(first_half, -sin_pos, sin_pos)

    def project():
        after_dt = j >= TILE_ROT0
        skip = pl.multiple_of(jnp.where(after_dt, SSD_HEADS, 0), SSD_HEADS)
        head = wa_ref[pl.ds(skip, TN_IN - SSD_HEADS), :]
        tail = jnp.where(after_dt, wb_ref[...], wa_ref[TN_IN - SSD_HEADS:, :])
        w = jnp.concatenate([head, tail], axis=0).astype(BF16)
        return _dot_nt(u_ref[...], w)

    @pl.when(jnp.logical_and(jnp.logical_or(je < TILE_CONV0, je >= TILE_G0), t < n_tiles))
    def _():
        acc = project()
        o_ref[...] = _silu(prev[...]).astype(BF16)
        prev[...] = acc

    @pl.when(t == n_tiles)
    def _():
        o_ref[...] = _silu(prev[...]).astype(BF16)

    @pl.when(jnp.logical_or(jnp.logical_and(je >= TILE_CONV0, je < TILE_ROT0),
                            jnp.logical_and(je >= TILE_V0, je < TILE_G0)))
    def _():
        acc = project()
        o_ref[...] = prev[...].astype(BF16)
        prev[...] = acc

    @pl.when(jnp.logical_and(je >= TILE_ROT0, je < TILE_V0))
    def _():
        acc = project()
        side = je - TILE_ROT0
        for hd in range(RET_HEADS):
            cols = slice(hd * RET_QK_DIM, (hd + 1) * RET_QK_DIM)
            scale = decay[side, hd]
            for blk in range(TM_IN // CHUNK):
                rows = slice(blk * CHUNK, (blk + 1) * CHUNK)
                a = prev[rows, cols]
                rot = a * cos_t[rows, :] + pltpu.roll(a, half, 1) * sin_t[rows, :]
                o_ref[rows, cols] = (rot * scale).astype(BF16)
        prev[...] = acc


def _in_projection(h, norm_w, mod, w_in_t, inv):
    L = h.shape[0]
    n_tiles = (L // TM_IN) * N_COL_TILES
    const = lambda t: (0, 0)
    mul_row = lambda t: jnp.minimum(t, n_tiles - 1) // N_COL_TILES
    mul_col = lambda t: lax.rem(jnp.minimum(t, n_tiles - 1), N_COL_TILES)
    epi = lambda t: jnp.maximum(t - 1, 0)
    table = pltpu.VMEM((TM_IN, RET_QK_DIM), F32)
    return pl.pallas_call(
        functools.partial(_inproj_kernel, n_tiles=n_tiles),
        grid=(n_tiles + 1,),
        in_specs=[pl.BlockSpec((TM_IN, D_MODEL), lambda t: (mul_row(t), 0)),
                  pl.BlockSpec((1, D_MODEL), const),
                  pl.BlockSpec((1, D_MODEL), lambda t: (0, 0)),
                  pl.BlockSpec((1, D_MODEL), lambda t: (0, 1)),
                  pl.BlockSpec((TN_IN, D_MODEL), lambda t: (mul_col(t), 0)),
                  pl.BlockSpec((SSD_HEADS, D_MODEL), lambda t: ((mul_col(t) + 1) * (TN_IN // SSD_HEADS), 0)),
                  pl.BlockSpec((LANES, D_MODEL), lambda t: (OFF_XBC // LANES, 0)),
                  pl.BlockSpec((1, RET_QK_DIM), const)],
        out_specs=[pl.BlockSpec((TM_IN, TN_IN), lambda t: (epi(t) // N_COL_TILES, lax.rem(epi(t), N_COL_TILES))),
                   pl.BlockSpec((TM_IN, LANES), lambda t: (mul_row(t), 0))],
        out_shape=[jax.ShapeDtypeStruct((L, PROJ_WIDTH), BF16),
                   jax.ShapeDtypeStruct((L, LANES), F32)],
        scratch_shapes=[pltpu.VMEM((TM_IN, D_MODEL), BF16),
                        pltpu.VMEM((TM_IN, TN_IN), F32),
                        table, table, table, table,
                        pltpu.VMEM((2, RET_HEADS, CHUNK, RET_QK_DIM), F32)],
        compiler_params=pltpu.CompilerParams(dimension_semantics=("arbitrary",),
                                             vmem_limit_bytes=60 * MIB),
        name="adaln_in_projection",
    )(h, norm_w.reshape(1, D_MODEL), mod, mod, w_in_t, w_in_t, w_in_t, inv)


def _ssd_init(xbuf, shifts, state, expand):
    xbuf[0:CONV_TAIL, :] = jnp.zeros((CONV_TAIL, SSD_CONV_DIM), BF16)
    row = lax.broadcasted_iota(jnp.int32, shifts.shape, 0)
    col = lax.broadcasted_iota(jnp.int32, shifts.shape, 1)
    delay = jnp.right_shift(row, CHUNK.bit_length() - 1) + 1
    t = jnp.bitwise_and(row, CHUNK - 1)
    shifts[...] = jnp.where(col == CONV_TAIL + t - delay, 1.0, 0.0).astype(BF16)
    state[...] = jnp.zeros_like(state)
    head_of_lane = jnp.right_shift(lax.broadcasted_iota(jnp.int32, (LANES, SSD_D_INNER), 1), HEAD_DIM_LOG2)
    row = lax.broadcasted_iota(jnp.int32, (LANES, SSD_D_INNER), 0)
    expand[...] = jnp.where(head_of_lane == row, 1.0, 0.0).astype(BF16)


def _ssd_chunk(z_ref, x_ref, b_ref, c_ref, dtraw_ref, cw_ref, cb_ref, dtb_ref, alog_ref, dskip_ref, nw_ref,
               y_ref, xbuf, shifts, state, expand):
    int_iota = lambda shape, dim: lax.broadcasted_iota(jnp.int32, shape, dim)

    xbuf[CONV_TAIL:, 0:SSD_D_INNER] = x_ref[...]
    xbuf[CONV_TAIL:, SSD_D_INNER:SSD_D_INNER + SSD_BC] = b_ref[...]
    xbuf[CONV_TAIL:, SSD_D_INNER + SSD_BC:] = c_ref[...]
    delayed = _dot(shifts[...], xbuf[...])
    acc = cb_ref[...] + cw_ref[SSD_CONV - 1:SSD_CONV, :] * xbuf[CONV_TAIL:, :].astype(F32)
    for k in range(SSD_CONV - 1):
        tap = SSD_CONV - 2 - k
        acc = acc + cw_ref[tap:tap + 1, :] * delayed[k * CHUNK:(k + 1) * CHUNK, :]
    xbuf[0:CONV_TAIL, :] = xbuf[CHUNK:CHUNK + CONV_TAIL, :]
    xbc = _silu(acc)
    xs = xbc[:, 0:SSD_D_INNER]
    xs_b = xs.astype(BF16)
    bmat = xbc[:, SSD_D_INNER:SSD_D_INNER + SSD_BC]
    cmat = xbc[:, SSD_D_INNER + SSD_BC:]

    pre = dtraw_ref[...] + dtb_ref[...]
    dt = jnp.maximum(pre, 0.0) + jnp.log1p(jnp.exp(-jnp.abs(pre)))
    d_a = dt * (-LOG2_E * jnp.exp(alog_ref[...]))
    causal = int_iota((CHUNK, CHUNK), 0) >= int_iota((CHUNK, CHUNK), 1)
    tril = jnp.where(causal, 1.0, 0.0).astype(BF16)
    acs = sum(_dot(tril, part) for part in _split3(d_a))
    acs_t = acs.T
    src_t = acs_t - jnp.log2(dt).T
    w_t = jnp.exp2(acs_t[:, CHUNK - 1:CHUNK] - src_t)
    end = jnp.broadcast_to(jnp.exp2(acs[CHUNK - 1:CHUNK, :]), (SUBLANES, LANES))
    end_row = sum(_dot(part, expand[...]) for part in _split3(end))[0:1, :]

    lane_blk = jnp.right_shift(int_iota((CHUNK, QUAD_W), 1), HEAD_DIM_LOG2)

    def block_diag(v):
        return jnp.concatenate([jnp.where(lane_blk == j, v, jnp.zeros_like(v)) for j in range(QUAD)], axis=0)

    def group(g, between_quads=lambda q: None):
        gs = slice(g * SSD_STATE, (g + 1) * SSD_STATE)
        b_g = bmat[:, gs]
        c_g = cmat[:, gs]
        cb_g = _dot_nt(c_g.astype(BF16), b_g.astype(BF16))
        b_gt = b_g.T
        y_parts = []
        for qi in range(HEADS_PER_GROUP // QUAD):
            q = g * (HEADS_PER_GROUP // QUAD) + qi
            between_quads(q)
            qs = slice(q * QUAD_W, (q + 1) * QUAD_W)
            lhs_diag, lhs_state, lhs_off = [], [], []
            for j in range(QUAD):
                hd = q * QUAD + j
                col = jnp.broadcast_to(acs[:, hd:hd + 1], (CHUNK, CHUNK))
                decay_dt = jnp.exp2(jnp.where(causal, col - src_t[hd:hd + 1, :], -jnp.inf))
                lhs_diag.append((cb_g * decay_dt).astype(BF16))
                lhs_state.append((b_gt * w_t[hd:hd + 1, :]).astype(BF16))
                lhs_off.append((c_g * jnp.exp2(col)).astype(BF16))
            x_blk = block_diag(xs_b[:, qs])
            s_prev = state[:, qs]
            s_blk = block_diag(s_prev.astype(BF16))
            y_q = _dot(jnp.concatenate(lhs_diag, axis=1), x_blk)
            y_q = y_q + _dot(jnp.concatenate(lhs_off, axis=1), s_blk)
            state[:, qs] = s_prev * end_row[:, qs] + _dot(jnp.concatenate(lhs_state, axis=1), x_blk)
            y_parts.append(y_q + xs[:, qs] * dskip_ref[:, qs])
        ws = slice(g * GROUP_W, (g + 1) * GROUP_W)
        u = jnp.concatenate(y_parts, axis=1) * z_ref[:, ws].astype(F32)
        u = u * lax.rsqrt(jnp.mean(u * u, axis=-1, keepdims=True) + EPS)
        y_ref[:, ws] = (u * nw_ref[:, ws]).astype(BF16)

    return group


def _ret_chunk(q_ref, k_ref, v0_ref, v1_ref, g0_ref, g1_ref, nw_ref, y_ref, state):
    heads_per_block = RET_QK_WIDTH // RET_V_DIM
    log_gammas = _log_gammas()
    causal = (lax.broadcasted_iota(jnp.int32, (CHUNK, CHUNK), 0)
              >= lax.broadcasted_iota(jnp.int32, (CHUNK, CHUNK), 1))

    def head(hd):
        ks = slice(hd * RET_QK_DIM, (hd + 1) * RET_QK_DIM)
        vs = slice(hd * RET_V_DIM, (hd + 1) * RET_V_DIM)
        v_ref, g_ref = (v0_ref, g0_ref) if hd < heads_per_block else (v1_ref, g1_ref)
        bs = slice((hd % heads_per_block) * RET_V_DIM, (hd % heads_per_block + 1) * RET_V_DIM)
        q_h = q_ref[:, ks]
        k_h = k_ref[:, ks]
        v_h = v_ref[:, bs]
        scores = jnp.where(causal, _dot_nt(q_h, k_h), 0.0)
        s_prev = state[hd]
        o = _dot(scores.astype(BF16), v_h) + _dot(q_h, s_prev.astype(BF16))
        state[hd] = (s_prev + _dot_tn(k_h, v_h)) * float(np.exp(np.float32(CHUNK * log_gammas[hd])))
        mu = jnp.mean(o, axis=-1, keepdims=True)
        d = o - mu
        y = d * lax.rsqrt(jnp.mean(d * d, axis=-1, keepdims=True) + EPS) * nw_ref[:, vs]
        y = (y * g_ref[:, bs].astype(F32)).astype(BF16)
        y_ref[:, SSD_D_INNER + hd * RET_V_DIM:SSD_D_INNER + (hd + 1) * RET_V_DIM] = y

    return head


def _mixer_kernel(z_ref, x_ref, b_ref, c_ref, dtraw_ref, cw_ref, cb_ref, dtb_ref, alog_ref, dskip_ref, snw_ref,
                  q_ref, k_ref, v0_ref, v1_ref, g0_ref, g1_ref, rnw_ref,
                  wout_hbm, h_ref, gate_ref, fw_ref, o_ref,
                  xbuf, shifts, ssd_state, expand, ret_state, ybuf, wout, wstage, wsem, *, final):
    step = pl.program_id(0)

    def wout_copy(i, slot):
        return pltpu.make_async_copy(wout_hbm.at[pl.ds(i * W_OUT_ROWS, W_OUT_ROWS), :], wstage.at[slot],
                                     wsem.at[slot])

    @pl.when(step == 0)
    def _():
        n_stage = MIX_WIDTH // W_OUT_ROWS
        wout_copy(0, 0).start()

        def stage(i, carry):
            slot = lax.rem(i, 2)

            @pl.when(i + 1 < n_stage)
            def _():
                wout_copy(i + 1, 1 - slot).start()

            wout_copy(i, slot).wait()
            wout[pl.ds(pl.multiple_of(i * W_OUT_ROWS, W_OUT_ROWS), W_OUT_ROWS), :] = wstage[slot].astype(BF16)
            return carry

        lax.fori_loop(0, n_stage, stage, 0)
        _ssd_init(xbuf, shifts, ssd_state, expand)
        ret_state[...] = jnp.zeros_like(ret_state)
        ybuf[...] = jnp.zeros_like(ybuf)

    slot = lax.rem(step, 2)
    y_old = ybuf[1 - slot]
    piece_w = D_MODEL // RET_HEADS
    pieces = []
    for sub in range(CHUNKS_PER_STEP):
        rows = pl.ds(sub * CHUNK, CHUNK)
        sub_refs = lambda *refs: [r.at[rows] for r in refs]
        y_new = ybuf.at[slot, rows]
        z_c, x_c, b_c, c_c, dtraw_c = sub_refs(z_ref, x_ref, b_ref, c_ref, dtraw_ref)
        ssd_group = _ssd_chunk(z_c, x_c, b_c, c_c, dtraw_c, cw_ref, cb_ref, dtb_ref, alog_ref, dskip_ref, snw_ref,
                               y_new, xbuf, shifts, ssd_state, expand)
        q_c, k_c, v0_c, v1_c, g0_c, g1_c = sub_refs(q_ref, k_ref, v0_ref, v1_ref, g0_ref, g1_ref)
        ret_head = _ret_chunk(q_c, k_c, v0_c, v1_c, g0_c, g1_c, rnw_ref, y_new, ret_state)

        def between_quads(q):
            if q % CHUNKS_PER_STEP == 0:
                n = len(pieces)
                pieces.append(_dot(y_old, wout[:, n * piece_w:(n + 1) * piece_w]))
            ret_head(q)

        for g in range(SSD_GROUPS):
            ssd_group(g, between_quads)

    h = h_ref[...] + gate_ref[...] * jnp.concatenate(pieces, axis=1)
    if final:
        h = h * lax.rsqrt(jnp.mean(h * h, axis=-1, keepdims=True) + EPS) * fw_ref[...]
    o_ref[...] = h


def _mix_and_project(proj, dt_raw, h, mod, conv_w, conv_b, dt_bias, a_log, d_skip, ssd_norm_w, ret_norm_w, w_out,
                     final_norm_w, final):
    L = h.shape[0]
    rows = CHUNKS_PER_STEP * CHUNK
    n_steps = L // rows
    pad = lambda v: jnp.pad(v.reshape(1, SSD_HEADS), ((0, 0), (0, LANES - SSD_HEADS)))
    full = lambda shape: pl.BlockSpec(shape, lambda s: (0, 0))
    mixed = lambda width, col: pl.BlockSpec((rows, width), lambda s: (jnp.minimum(s, n_steps - 1), col))
    projected = lambda width, col: pl.BlockSpec((rows, width), lambda s: (jnp.maximum(s - 1, 0), col))
    return pl.pallas_call(
        functools.partial(_mixer_kernel, final=final),
        grid=(n_steps + 1,),
        in_specs=[mixed(SSD_D_INNER, COL_Z), mixed(SSD_D_INNER, COL_X), mixed(SSD_BC, COL_B), mixed(SSD_BC, COL_C),
                  mixed(LANES, 0),
                  full((SSD_CONV, SSD_CONV_DIM)), full((1, SSD_CONV_DIM)), full((1, LANES)), full((1, LANES)),
                  full((1, SSD_D_INNER)), full((1, SSD_D_INNER)),
                  mixed(RET_QK_WIDTH, COL_Q), mixed(RET_QK_WIDTH, COL_K),
                  mixed(RET_QK_WIDTH, COL_V), mixed(RET_QK_WIDTH, COL_V + 1),
                  mixed(RET_QK_WIDTH, COL_G), mixed(RET_QK_WIDTH, COL_G + 1),
                  full((1, RET_V_WIDTH)),
                  pl.BlockSpec(memory_space=pl.ANY),
                  projected(D_MODEL, 0),
                  pl.BlockSpec((1, D_MODEL), lambda s: (0, 2)),
                  full((1, D_MODEL))],
        out_specs=projected(D_MODEL, 0),
        out_shape=jax.ShapeDtypeStruct((L, D_MODEL), F32),
        scratch_shapes=[pltpu.VMEM((CONV_TAIL + CHUNK, SSD_CONV_DIM), BF16),
                        pltpu.VMEM(((SSD_CONV - 1) * CHUNK, CONV_TAIL + CHUNK), BF16),
                        pltpu.VMEM((SSD_STATE, SSD_D_INNER), F32),
                        pltpu.VMEM((LANES, SSD_D_INNER), BF16),
                        pltpu.VMEM((RET_HEADS, RET_QK_DIM, RET_V_DIM), F32),
                        pltpu.VMEM((2, rows, MIX_WIDTH), BF16),
                        pltpu.VMEM((MIX_WIDTH, D_MODEL), BF16),
                        pltpu.VMEM((2, W_OUT_ROWS, D_MODEL), F32),
                        pltpu.SemaphoreType.DMA((2,))],
        compiler_params=pltpu.CompilerParams(dimension_semantics=("arbitrary",),
                                             vmem_limit_bytes=60 * MIB),
        name="mixers_out_projection",
    )(proj, proj, proj, proj, dt_raw, conv_w, conv_b.reshape(1, SSD_CONV_DIM), pad(dt_bias), pad(a_log),
      jnp.repeat(d_skip, SSD_HEAD_DIM).reshape(1, SSD_D_INNER), ssd_norm_w.reshape(1, SSD_D_INNER),
      proj, proj, proj, proj, proj, proj, ret_norm_w.reshape(1, RET_V_WIDTH),
      w_out, h, mod, final_norm_w.reshape(1, D_MODEL))


def kernel(x, c, w_ada, b_ada, norm_w, w_in, conv_w, conv_b, dt_bias, a_log, d_skip, ssd_norm_w, ret_norm_w,
           w_out, final_norm_w):
    bsz, L, d_model = x.shape
    assert bsz == 1 and d_model == D_MODEL and L % TM_IN == 0 and w_in.shape[-1] == IN_WIDTH
    depth = w_in.shape[0]
    half = RET_QK_DIM // 2
    inv = ROPE_BASE ** (-jnp.arange(half, dtype=F32) / half)
    inv = jnp.concatenate([inv, inv]).reshape(1, RET_QK_DIM)
    h = x.reshape(L, D_MODEL)
    for layer in range(depth):
        mod = _modulation(c, w_ada[layer], b_ada[layer])
        proj, dt_raw = _in_projection(h, norm_w[layer], mod, jnp.swapaxes(w_in[layer], 0, 1), inv)
        h = _mix_and_project(proj, dt_raw, h, mod, conv_w[layer], conv_b[layer], dt_bias[layer], a_log[layer],
                             d_skip[layer], ssd_norm_w[layer], ret_norm_w[layer], w_out[layer], final_norm_w,
                             final=layer == depth - 1)
    return h.reshape(bsz, L, D_MODEL)
```

```python
import functools
import math

import numpy as np
import jax
import jax.numpy as jnp
from jax import lax
from jax.experimental import pallas as pl
from jax.experimental.pallas import tpu as pltpu

D_MODEL = 2048
SSD_D_INNER = D_MODEL
SSD_HEAD_DIM = 64
SSD_HEADS = SSD_D_INNER // SSD_HEAD_DIM
HEAD_DIM_LOG2 = SSD_HEAD_DIM.bit_length() - 1
SSD_GROUPS = 4
SSD_STATE = 128
SSD_CONV = 4
SSD_BC = SSD_GROUPS * SSD_STATE
SSD_CONV_DIM = SSD_D_INNER + 2 * SSD_BC
RET_HEADS = 8
RET_QK_DIM = 128
RET_V_DIM = 256
RET_QK_WIDTH = RET_HEADS * RET_QK_DIM
RET_V_WIDTH = RET_HEADS * RET_V_DIM
MIX_WIDTH = SSD_D_INNER + RET_V_WIDTH
CHUNK = 128
ROPE_BASE = 10000.0
EPS = 1e-6
LOG2_E = math.log2(math.e)

OFF_Z = SSD_D_INNER
OFF_XBC = OFF_Z + SSD_CONV_DIM
OFF_DT = OFF_XBC + SSD_HEADS
OFF_Q = OFF_DT + RET_QK_WIDTH
OFF_K = OFF_Q + RET_QK_WIDTH
OFF_V = OFF_K + RET_V_WIDTH
IN_WIDTH = OFF_V + RET_V_WIDTH

PROJ_WIDTH = IN_WIDTH - SSD_HEADS
COL_Z, COL_X = 0, 1
COL_B, COL_C = 8, 9
COL_Q, COL_K, COL_V, COL_G = 5, 6, 7, 9

LANES = 128
SUBLANES = 8
BF16_SUBLANES = 16
QUAD = 4
QUAD_W = QUAD * SSD_HEAD_DIM
HEADS_PER_GROUP = SSD_HEADS // SSD_GROUPS
GROUP_W = SSD_D_INNER // SSD_GROUPS
CONV_TAIL = BF16_SUBLANES
CHUNKS_PER_STEP = 2

TM_IN, TN_IN = 1024, 1024
TILE_CONV0, TILE_ROT0, TILE_V0, TILE_G0 = (c // TN_IN for c in (OFF_Z, OFF_XBC, OFF_XBC + 2 * RET_QK_WIDTH,
                                                                 OFF_XBC + 2 * RET_QK_WIDTH + RET_V_WIDTH))
N_COL_TILES = PROJ_WIDTH // TN_IN
assert OFF_Z % TN_IN == 0 and OFF_XBC % TN_IN == 0 and RET_QK_WIDTH == TN_IN and RET_V_WIDTH % TN_IN == 0
assert PROJ_WIDTH % TN_IN == 0 and TM_IN % CHUNK == 0 and N_COL_TILES - 1 >= TILE_G0
MIB = 1024 * 1024

F32 = jnp.float32
BF16 = jnp.bfloat16


def _silu(v):
    return v / (1.0 + jnp.exp(-v))


def _dot(a, b):
    return jnp.dot(a, b, preferred_element_type=F32)


def _dot_nt(a, b):
    return lax.dot_general(a, b, (((1,), (1,)), ((), ())), preferred_element_type=F32)


def _dot_tn(a, b):
    return lax.dot_general(a, b, (((0,), (0,)), ((), ())), preferred_element_type=F32)


def _split3(v):
    hi = v.astype(BF16)
    r1 = v - hi.astype(F32)
    mid = r1.astype(BF16)
    lo = (r1 - mid.astype(F32)).astype(BF16)
    return hi, mid, lo


def _log_gammas():
    return [float(np.log1p(-np.exp2(np.float32(-5.0 - hd)), dtype=np.float32)) for hd in range(RET_HEADS)]


def _mod_kernel(c_ref, w_ref, b_ref, o_ref):
    cond = jnp.broadcast_to(_silu(c_ref[...]), (SUBLANES, D_MODEL))
    o_ref[...] = _dot(cond, w_ref[...])[0:1, :] + b_ref[...]


def _modulation(c, w_ada, b_ada):
    n = w_ada.shape[1]
    tn = 512
    return pl.pallas_call(
        _mod_kernel,
        grid=(n // tn,),
        in_specs=[pl.BlockSpec((1, D_MODEL), lambda j: (0, 0)),
                  pl.BlockSpec((D_MODEL, tn), lambda j: (0, j)),
                  pl.BlockSpec((1, tn), lambda j: (0, j))],
        out_specs=pl.BlockSpec((1, tn), lambda j: (0, j)),
        out_shape=jax.ShapeDtypeStruct((1, n), F32),
        compiler_params=pltpu.CompilerParams(dimension_semantics=("arbitrary",),
                                             vmem_limit_bytes=24 * MIB),
        name="adaln_modulation",
    )(c, w_ada, b_ada.reshape(1, n))


def _inproj_kernel(x_ref, nw_ref, shift_ref, scale_ref, wa_ref, wb_ref, wd_ref, inv_ref, wout_ref,
                   o_ref, dt_ref, wout_bf_ref, u_ref, prev, cos_in, sin_in, cos_t, sin_t, decay, *, n_tiles):
    t = pl.program_id(0)
    tile = jnp.minimum(t, n_tiles - 1)
    i = tile // N_COL_TILES
    j = tile - i * N_COL_TILES
    je = lax.rem(jnp.maximum(t - 1, 0), N_COL_TILES)
    half = RET_QK_DIM // 2

    @pl.when(t == 0)
    def _():
        prev[...] = jnp.zeros_like(prev)
        ang = lax.broadcasted_iota(jnp.int32, (TM_IN, RET_QK_DIM), 0).astype(F32) * inv_ref[...]
        cos_in[...] = jnp.cos(ang)
        sin_in[...] = jnp.sin(ang)
        pos = lax.broadcasted_iota(jnp.int32, (CHUNK, RET_QK_DIM), 0).astype(F32) + 1.0
        for hd, lg in enumerate(_log_gammas()):
            decay[0, hd] = jnp.exp(pos * lg)
            decay[1, hd] = jnp.exp(pos * -lg) * RET_QK_DIM ** -0.5

    @pl.when(jnp.logical_and(j == 0, t < n_tiles))
    def _():
        x = x_ref[...]
        gain = nw_ref[...] * (1.0 + scale_ref[...])
        u = (x * lax.rsqrt(jnp.mean(x * x, axis=-1, keepdims=True) + EPS) * gain + shift_ref[...]).astype(BF16)
        u_ref[...] = u
        is_dt = lax.broadcasted_iota(jnp.int32, (LANES, D_MODEL), 0) < SSD_HEADS
        dt_ref[...] = _dot_nt(u, jnp.where(is_dt, wd_ref[...], 0.0).astype(BF16))

        base = jnp.broadcast_to((i * TM_IN).astype(F32) * inv_ref[...], (SUBLANES, RET_QK_DIM))
        cos_b = jnp.cos(base)[0:1, :]
        sin_b = jnp.sin(base)[0:1, :]
        cos_t[...] = cos_in[...] * cos_b - sin_in[...] * sin_b
        sin_pos = sin_in[...] * cos_b + cos_in[...] * sin_b
        first_half = lax.broadcasted_iota(jnp.int32, (TM_IN, RET_QK_DIM), 1) < half
        sin_t[...] = jnp.where(first_half, -sin_pos, sin_pos)

    def project():
        after_dt = j >= TILE_ROT0
        skip = pl.multiple_of(jnp.where(after_dt, SSD_HEADS, 0), SSD_HEADS)
        head = wa_ref[pl.ds(skip, TN_IN - SSD_HEADS), :]
        tail = jnp.where(after_dt, wb_ref[...], wa_ref[TN_IN - SSD_HEADS:, :])
        w = jnp.concatenate([head, tail], axis=0).astype(BF16)
        wout_bf_ref[...] = wout_ref[...].astype(BF16)
        return _dot_nt(u_ref[...], w)

    @pl.when(jnp.logical_and(jnp.logical_or(je < TILE_CONV0, je >= TILE_G0), t < n_tiles))
    def _():
        acc = project()
        o_ref[...] = _silu(prev[...]).astype(BF16)
        prev[...] = acc

    @pl.when(t == n_tiles)
    def _():
        o_ref[...] = _silu(prev[...]).astype(BF16)

    @pl.when(jnp.logical_or(jnp.logical_and(je >= TILE_CONV0, je < TILE_ROT0),
                            jnp.logical_and(je >= TILE_V0, je < TILE_G0)))
    def _():
        acc = project()
        o_ref[...] = prev[...].astype(BF16)
        prev[...] = acc

    @pl.when(jnp.logical_and(je >= TILE_ROT0, je < TILE_V0))
    def _():
        acc = project()
        side = je - TILE_ROT0
        for hd in range(RET_HEADS):
            cols = slice(hd * RET_QK_DIM, (hd + 1) * RET_QK_DIM)
            scale = decay[side, hd]
            for blk in range(TM_IN // CHUNK):
                rows = slice(blk * CHUNK, (blk + 1) * CHUNK)
                a = prev[rows, cols]
                rot = a * cos_t[rows, :] + pltpu.roll(a, half, 1) * sin_t[rows, :]
                o_ref[rows, cols] = (rot * scale).astype(BF16)
        prev[...] = acc


def _in_projection(h, norm_w, mod, w_in_t, inv, w_out):
    L = h.shape[0]
    n_tiles = (L // TM_IN) * N_COL_TILES
    n_cast = min(1 << (n_tiles.bit_length() - 1), MIX_WIDTH // BF16_SUBLANES)
    cast_block = pl.BlockSpec((MIX_WIDTH // n_cast, D_MODEL), lambda t: (jnp.minimum(t, n_cast - 1), 0))
    const = lambda t: (0, 0)
    mul_row = lambda t: jnp.minimum(t, n_tiles - 1) // N_COL_TILES
    mul_col = lambda t: lax.rem(jnp.minimum(t, n_tiles - 1), N_COL_TILES)
    epi = lambda t: jnp.maximum(t - 1, 0)
    table = pltpu.VMEM((TM_IN, RET_QK_DIM), F32)
    return pl.pallas_call(
        functools.partial(_inproj_kernel, n_tiles=n_tiles),
        grid=(n_tiles + 1,),
        in_specs=[pl.BlockSpec((TM_IN, D_MODEL), lambda t: (mul_row(t), 0)),
                  pl.BlockSpec((1, D_MODEL), const),
                  pl.BlockSpec((1, D_MODEL), lambda t: (0, 0)),
                  pl.BlockSpec((1, D_MODEL), lambda t: (0, 1)),
                  pl.BlockSpec((TN_IN, D_MODEL), lambda t: (mul_col(t), 0)),
                  pl.BlockSpec((SSD_HEADS, D_MODEL), lambda t: ((mul_col(t) + 1) * (TN_IN // SSD_HEADS), 0)),
                  pl.BlockSpec((LANES, D_MODEL), lambda t: (OFF_XBC // LANES, 0)),
                  pl.BlockSpec((1, RET_QK_DIM), const),
                  cast_block],
        out_specs=[pl.BlockSpec((TM_IN, TN_IN), lambda t: (epi(t) // N_COL_TILES, lax.rem(epi(t), N_COL_TILES))),
                   pl.BlockSpec((TM_IN, LANES), lambda t: (mul_row(t), 0)),
                   cast_block],
        out_shape=[jax.ShapeDtypeStruct((L, PROJ_WIDTH), BF16),
                   jax.ShapeDtypeStruct((L, LANES), F32),
                   jax.ShapeDtypeStruct((MIX_WIDTH, D_MODEL), BF16)],
        scratch_shapes=[pltpu.VMEM((TM_IN, D_MODEL), BF16),
                        pltpu.VMEM((TM_IN, TN_IN), F32),
                        table, table, table, table,
                        pltpu.VMEM((2, RET_HEADS, CHUNK, RET_QK_DIM), F32)],
        compiler_params=pltpu.CompilerParams(dimension_semantics=("arbitrary",),
                                             vmem_limit_bytes=60 * MIB),
        name="adaln_in_projection",
    )(h, norm_w.reshape(1, D_MODEL), mod, mod, w_in_t, w_in_t, w_in_t, inv, w_out)


def _ssd_init(xbuf, shifts, state, expand):
    xbuf[0:CONV_TAIL, :] = jnp.zeros((CONV_TAIL, SSD_CONV_DIM), BF16)
    row = lax.broadcasted_iota(jnp.int32, shifts.shape, 0)
    col = lax.broadcasted_iota(jnp.int32, shifts.shape, 1)
    delay = jnp.right_shift(row, CHUNK.bit_length() - 1) + 1
    t = jnp.bitwise_and(row, CHUNK - 1)
    shifts[...] = jnp.where(col == CONV_TAIL + t - delay, 1.0, 0.0).astype(BF16)
    state[...] = jnp.zeros_like(state)
    head_of_lane = jnp.right_shift(lax.broadcasted_iota(jnp.int32, (LANES, SSD_D_INNER), 1), HEAD_DIM_LOG2)
    row = lax.broadcasted_iota(jnp.int32, (LANES, SSD_D_INNER), 0)
    expand[...] = jnp.where(head_of_lane == row, 1.0, 0.0).astype(BF16)


def _ssd_chunk(z_ref, x_ref, b_ref, c_ref, dtraw_ref, cw_ref, cb_ref, dtb_ref, alog_ref, dskip_ref, nw_ref,
               y_ref, xbuf, shifts, state, expand):
    int_iota = lambda shape, dim: lax.broadcasted_iota(jnp.int32, shape, dim)

    xbuf[CONV_TAIL:, 0:SSD_D_INNER] = x_ref[...]
    xbuf[CONV_TAIL:, SSD_D_INNER:SSD_D_INNER + SSD_BC] = b_ref[...]
    xbuf[CONV_TAIL:, SSD_D_INNER + SSD_BC:] = c_ref[...]
    delayed = _dot(shifts[...], xbuf[...])
    acc = cb_ref[...] + cw_ref[SSD_CONV - 1:SSD_CONV, :] * xbuf[CONV_TAIL:, :].astype(F32)
    for k in range(SSD_CONV - 1):
        tap = SSD_CONV - 2 - k
        acc = acc + cw_ref[tap:tap + 1, :] * delayed[k * CHUNK:(k + 1) * CHUNK, :]
    xbuf[0:CONV_TAIL, :] = xbuf[CHUNK:CHUNK + CONV_TAIL, :]
    xbc = _silu(acc)
    xs = xbc[:, 0:SSD_D_INNER]
    xs_b = xs.astype(BF16)
    bmat = xbc[:, SSD_D_INNER:SSD_D_INNER + SSD_BC]
    cmat = xbc[:, SSD_D_INNER + SSD_BC:]

    pre = dtraw_ref[...] + dtb_ref[...]
    dt = jnp.maximum(pre, 0.0) + jnp.log1p(jnp.exp(-jnp.abs(pre)))
    d_a = dt * (-LOG2_E * jnp.exp(alog_ref[...]))
    causal = int_iota((CHUNK, CHUNK), 0) >= int_iota((CHUNK, CHUNK), 1)
    tril = jnp.where(causal, 1.0, 0.0).astype(BF16)
    acs = sum(_dot(tril, part) for part in _split3(d_a))
    acs_t = acs.T
    src_t = acs_t - jnp.log2(dt).T
    w_t = jnp.exp2(acs_t[:, CHUNK - 1:CHUNK] - src_t)
    end = jnp.broadcast_to(jnp.exp2(acs[CHUNK - 1:CHUNK, :]), (SUBLANES, LANES))
    end_row = sum(_dot(part, expand[...]) for part in _split3(end))[0:1, :]

    lane_blk = jnp.right_shift(int_iota((CHUNK, QUAD_W), 1), HEAD_DIM_LOG2)

    def block_diag(v):
        return jnp.concatenate([jnp.where(lane_blk == j, v, jnp.zeros_like(v)) for j in range(QUAD)], axis=0)

    def group(g, between_quads=lambda q: None):
        gs = slice(g * SSD_STATE, (g + 1) * SSD_STATE)
        b_g = bmat[:, gs]
        c_g = cmat[:, gs]
        cb_g = _dot_nt(c_g.astype(BF16), b_g.astype(BF16))
        b_gt = b_g.T
        y_parts = []
        for qi in range(HEADS_PER_GROUP // QUAD):
            q = g * (HEADS_PER_GROUP // QUAD) + qi
            between_quads(q)
            qs = slice(q * QUAD_W, (q + 1) * QUAD_W)
            lhs_diag, lhs_state, lhs_off = [], [], []
            for j in range(QUAD):
                hd = q * QUAD + j
                col = jnp.broadcast_to(acs[:, hd:hd + 1], (CHUNK, CHUNK))
                decay_dt = jnp.exp2(jnp.where(causal, col - src_t[hd:hd + 1, :], -jnp.inf))
                lhs_diag.append((cb_g * decay_dt).astype(BF16))
                lhs_state.append((b_gt * w_t[hd:hd + 1, :]).astype(BF16))
                lhs_off.append((c_g * jnp.exp2(col)).astype(BF16))
            x_blk = block_diag(xs_b[:, qs])
            s_prev = state[:, qs]
            s_blk = block_diag(s_prev.astype(BF16))
            y_q = _dot(jnp.concatenate(lhs_diag, axis=1), x_blk)
            y_q = y_q + _dot(jnp.concatenate(lhs_off, axis=1), s_blk)
            state[:, qs] = s_prev * end_row[:, qs] + _dot(jnp.concatenate(lhs_state, axis=1), x_blk)
            y_parts.append(y_q + xs[:, qs] * dskip_ref[:, qs])
        ws = slice(g * GROUP_W, (g + 1) * GROUP_W)
        u = jnp.concatenate(y_parts, axis=1) * z_ref[:, ws].astype(F32)
        u = u * lax.rsqrt(jnp.mean(u * u, axis=-1, keepdims=True) + EPS)
        y_ref[:, ws] = (u * nw_ref[:, ws]).astype(BF16)

    return group


def _ret_chunk(q_ref, k_ref, v0_ref, v1_ref, g0_ref, g1_ref, nw_ref, y_ref, state):
    heads_per_block = RET_QK_WIDTH // RET_V_DIM
    log_gammas = _log_gammas()
    causal = (lax.broadcasted_iota(jnp.int32, (CHUNK, CHUNK), 0)
              >= lax.broadcasted_iota(jnp.int32, (CHUNK, CHUNK), 1))

    def head(hd):
        ks = slice(hd * RET_QK_DIM, (hd + 1) * RET_QK_DIM)
        vs = slice(hd * RET_V_DIM, (hd + 1) * RET_V_DIM)
        v_ref, g_ref = (v0_ref, g0_ref) if hd < heads_per_block else (v1_ref, g1_ref)
        bs = slice((hd % heads_per_block) * RET_V_DIM, (hd % heads_per_block + 1) * RET_V_DIM)
        q_h = q_ref[:, ks]
        k_h = k_ref[:, ks]
        v_h = v_ref[:, bs]
        scores = jnp.where(causal, _dot_nt(q_h, k_h), 0.0)
        s_prev = state[hd]
        o = _dot(scores.astype(BF16), v_h) + _dot(q_h, s_prev.astype(BF16))
        state[hd] = (s_prev + _dot_tn(k_h, v_h)) * float(np.exp(np.float32(CHUNK * log_gammas[hd])))
        mu = jnp.mean(o, axis=-1, keepdims=True)
        d = o - mu
        y = d * lax.rsqrt(jnp.mean(d * d, axis=-1, keepdims=True) + EPS) * nw_ref[:, vs]
        y = (y * g_ref[:, bs].astype(F32)).astype(BF16)
        y_ref[:, SSD_D_INNER + hd * RET_V_DIM:SSD_D_INNER + (hd + 1) * RET_V_DIM] = y

    return head


def _mixer_kernel(z_ref, x_ref, b_ref, c_ref, dtraw_ref, cw_ref, cb_ref, dtb_ref, alog_ref, dskip_ref, snw_ref,
                  q_ref, k_ref, v0_ref, v1_ref, g0_ref, g1_ref, rnw_ref,
                  wout_hbm, h_ref, gate_ref, fw_ref, o_ref,
                  xbuf, shifts, ssd_state, expand, ret_state, ybuf, wout, wsem, *, final):
    step = pl.program_id(0)

    @pl.when(step == 0)
    def _():
        wout_copy = pltpu.make_async_copy(wout_hbm, wout, wsem.at[0])
        wout_copy.start()
        _ssd_init(xbuf, shifts, ssd_state, expand)
        ret_state[...] = jnp.zeros_like(ret_state)
        ybuf[...] = jnp.zeros_like(ybuf)
        wout_copy.wait()

    slot = lax.rem(step, 2)
    y_old = ybuf[1 - slot]
    piece_w = D_MODEL // RET_HEADS
    pieces = []
    for sub in range(CHUNKS_PER_STEP):
        rows = pl.ds(sub * CHUNK, CHUNK)
        sub_refs = lambda *refs: [r.at[rows] for r in refs]
        y_new = ybuf.at[slot, rows]
        z_c, x_c, b_c, c_c, dtraw_c = sub_refs(z_ref, x_ref, b_ref, c_ref, dtraw_ref)
        ssd_group = _ssd_chunk(z_c, x_c, b_c, c_c, dtraw_c, cw_ref, cb_ref, dtb_ref, alog_ref, dskip_ref, snw_ref,
                               y_new, xbuf, shifts, ssd_state, expand)
        q_c, k_c, v0_c, v1_c, g0_c, g1_c = sub_refs(q_ref, k_ref, v0_ref, v1_ref, g0_ref, g1_ref)
        ret_head = _ret_chunk(q_c, k_c, v0_c, v1_c, g0_c, g1_c, rnw_ref, y_new, ret_state)

        def between_quads(q):
            if q % CHUNKS_PER_STEP == 0:
                n = len(pieces)
                pieces.append(_dot(y_old, wout[:, n * piece_w:(n + 1) * piece_w]))
            ret_head(q)

        for g in range(SSD_GROUPS):
            ssd_group(g, between_quads)

    h = h_ref[...] + gate_ref[...] * jnp.concatenate(pieces, axis=1)
    if final:
        h = h * lax.rsqrt(jnp.mean(h * h, axis=-1, keepdims=True) + EPS) * fw_ref[...]
    o_ref[...] = h


def _mix_and_project(proj, dt_raw, h, mod, conv_w, conv_b, dt_bias, a_log, d_skip, ssd_norm_w, ret_norm_w, w_out,
                     final_norm_w, final):
    L = h.shape[0]
    rows = CHUNKS_PER_STEP * CHUNK
    n_steps = L // rows
    pad = lambda v: jnp.pad(v.reshape(1, SSD_HEADS), ((0, 0), (0, LANES - SSD_HEADS)))
    full = lambda shape: pl.BlockSpec(shape, lambda s: (0, 0))
    mixed = lambda width, col: pl.BlockSpec((rows, width), lambda s: (jnp.minimum(s, n_steps - 1), col))
    projected = lambda width, col: pl.BlockSpec((rows, width), lambda s: (jnp.maximum(s - 1, 0), col))
    return pl.pallas_call(
        functools.partial(_mixer_kernel, final=final),
        grid=(n_steps + 1,),
        in_specs=[mixed(SSD_D_INNER, COL_Z), mixed(SSD_D_INNER, COL_X), mixed(SSD_BC, COL_B), mixed(SSD_BC, COL_C),
                  mixed(LANES, 0),
                  full((SSD_CONV, SSD_CONV_DIM)), full((1, SSD_CONV_DIM)), full((1, LANES)), full((1, LANES)),
                  full((1, SSD_D_INNER)), full((1, SSD_D_INNER)),
                  mixed(RET_QK_WIDTH, COL_Q), mixed(RET_QK_WIDTH, COL_K),
                  mixed(RET_QK_WIDTH, COL_V), mixed(RET_QK_WIDTH, COL_V + 1),
                  mixed(RET_QK_WIDTH, COL_G), mixed(RET_QK_WIDTH, COL_G + 1),
                  full((1, RET_V_WIDTH)),
                  pl.BlockSpec(memory_space=pl.ANY),
                  projected(D_MODEL, 0),
                  pl.BlockSpec((1, D_MODEL), lambda s: (0, 2)),
                  full((1, D_MODEL))],
        out_specs=projected(D_MODEL, 0),
        out_shape=jax.ShapeDtypeStruct((L, D_MODEL), F32),
        scratch_shapes=[pltpu.VMEM((CONV_TAIL + CHUNK, SSD_CONV_DIM), BF16),
                        pltpu.VMEM(((SSD_CONV - 1) * CHUNK, CONV_TAIL + CHUNK), BF16),
                        pltpu.VMEM((SSD_STATE, SSD_D_INNER), F32),
                        pltpu.VMEM((LANES, SSD_D_INNER), BF16),
                        pltpu.VMEM((RET_HEADS, RET_QK_DIM, RET_V_DIM), F32),
                        pltpu.VMEM((2, rows, MIX_WIDTH), BF16),
                        pltpu.VMEM((MIX_WIDTH, D_MODEL), BF16),
                        pltpu.SemaphoreType.DMA((1,))],
        compiler_params=pltpu.CompilerParams(dimension_semantics=("arbitrary",),
                                             vmem_limit_bytes=60 * MIB),
        name="mixers_out_projection",
    )(proj, proj, proj, proj, dt_raw, conv_w, conv_b.reshape(1, SSD_CONV_DIM), pad(dt_bias), pad(a_log),
      jnp.repeat(d_skip, SSD_HEAD_DIM).reshape(1, SSD_D_INNER), ssd_norm_w.reshape(1, SSD_D_INNER),
      proj, proj, proj, proj, proj, proj, ret_norm_w.reshape(1, RET_V_WIDTH),
      w_out, h, mod, final_norm_w.reshape(1, D_MODEL))


def kernel(x, c, w_ada, b_ada, norm_w, w_in, conv_w, conv_b, dt_bias, a_log, d_skip, ssd_norm_w, ret_norm_w,
           w_out, final_norm_w):
    bsz, L, d_model = x.shape
    assert bsz == 1 and d_model == D_MODEL and L % TM_IN == 0 and w_in.shape[-1] == IN_WIDTH
    depth = w_in.shape[0]
    half = RET_QK_DIM // 2
    inv = ROPE_BASE ** (-jnp.arange(half, dtype=F32) / half)
    inv = jnp.concatenate([inv, inv]).reshape(1, RET_QK_DIM)
    h = x.reshape(L, D_MODEL)
    for layer in range(depth):
        mod = _modulation(c, w_ada[layer], b_ada[layer])
        proj, dt_raw, w_out_bf = _in_projection(h, norm_w[layer], mod, jnp.swapaxes(w_in[layer], 0, 1), inv,
                                                w_out[layer])
        h = _mix_and_project(proj, dt_raw, h, mod, conv_w[layer], conv_b[layer], dt_bias[layer], a_log[layer],
                             d_skip[layer], ssd_norm_w[layer], ret_norm_w[layer], w_out_bf, final_norm_w,
                             final=layer == depth - 1)
    return h.reshape(bsz, L, D_MODEL)
```

```python
import functools
import math

import numpy as np
import jax
import jax.numpy as jnp
from jax import lax
from jax.experimental import pallas as pl
from jax.experimental.pallas import tpu as pltpu

D_MODEL = 2048
SSD_D_INNER = D_MODEL
SSD_HEAD_DIM = 64
SSD_HEADS = SSD_D_INNER // SSD_HEAD_DIM
HEAD_DIM_LOG2 = SSD_HEAD_DIM.bit_length() - 1
SSD_GROUPS = 4
SSD_STATE = 128
SSD_CONV = 4
SSD_BC = SSD_GROUPS * SSD_STATE
SSD_CONV_DIM = SSD_D_INNER + 2 * SSD_BC
RET_HEADS = 8
RET_QK_DIM = 128
RET_V_DIM = 256
RET_QK_WIDTH = RET_HEADS * RET_QK_DIM
RET_V_WIDTH = RET_HEADS * RET_V_DIM
MIX_WIDTH = SSD_D_INNER + RET_V_WIDTH
CHUNK = 128
ROPE_BASE = 10000.0
EPS = 1e-6
LOG2_E = math.log2(math.e)

OFF_Z = SSD_D_INNER
OFF_XBC = OFF_Z + SSD_CONV_DIM
OFF_DT = OFF_XBC + SSD_HEADS
OFF_Q = OFF_DT + RET_QK_WIDTH
OFF_K = OFF_Q + RET_QK_WIDTH
OFF_V = OFF_K + RET_V_WIDTH
IN_WIDTH = OFF_V + RET_V_WIDTH

PROJ_WIDTH = IN_WIDTH - SSD_HEADS
COL_Z, COL_X = 0, 1
COL_B, COL_C = 8, 9
COL_Q, COL_K, COL_V, COL_G = 5, 6, 7, 9

LANES = 128
SUBLANES = 8
BF16_SUBLANES = 16
QUAD = 4
QUAD_W = QUAD * SSD_HEAD_DIM
HEADS_PER_GROUP = SSD_HEADS // SSD_GROUPS
GROUP_W = SSD_D_INNER // SSD_GROUPS
CONV_TAIL = BF16_SUBLANES
CHUNKS_PER_STEP = 2

MOD_TILE = 512
MOD_BUFFERS = 3
TM_IN, TN_IN = 1024, 1024
TILE_CONV0, TILE_ROT0, TILE_V0, TILE_G0 = (c // TN_IN for c in (OFF_Z, OFF_XBC, OFF_XBC + 2 * RET_QK_WIDTH,
                                                                 OFF_XBC + 2 * RET_QK_WIDTH + RET_V_WIDTH))
N_COL_TILES = PROJ_WIDTH // TN_IN
assert OFF_Z % TN_IN == 0 and OFF_XBC % TN_IN == 0 and RET_QK_WIDTH == TN_IN and RET_V_WIDTH % TN_IN == 0
assert PROJ_WIDTH % TN_IN == 0 and TM_IN % CHUNK == 0 and N_COL_TILES - 1 >= TILE_G0
MIB = 1024 * 1024

F32 = jnp.float32
BF16 = jnp.bfloat16


def _silu(v):
    return v / (1.0 + jnp.exp(-v))


def _dot(a, b):
    return jnp.dot(a, b, preferred_element_type=F32)


def _dot_nt(a, b):
    return lax.dot_general(a, b, (((1,), (1,)), ((), ())), preferred_element_type=F32)


def _dot_tn(a, b):
    return lax.dot_general(a, b, (((0,), (0,)), ((), ())), preferred_element_type=F32)


def _split3(v):
    hi = v.astype(BF16)
    r1 = v - hi.astype(F32)
    mid = r1.astype(BF16)
    lo = (r1 - mid.astype(F32)).astype(BF16)
    return hi, mid, lo


def _log_gammas():
    return [float(np.log1p(-np.exp2(np.float32(-5.0 - hd)), dtype=np.float32)) for hd in range(RET_HEADS)]


def _mod_kernel(c_ref, w_hbm, b_ref, o_ref, wbuf, sem):
    n_tiles = o_ref.shape[1] // MOD_TILE

    def fetch(j):
        slot = j % MOD_BUFFERS
        return pltpu.make_async_copy(w_hbm.at[:, pl.ds(j * MOD_TILE, MOD_TILE)], wbuf.at[slot], sem.at[slot])

    for j in range(min(MOD_BUFFERS - 1, n_tiles)):
        fetch(j).start()
    cond = jnp.broadcast_to(_silu(c_ref[...]), (SUBLANES, D_MODEL))
    for j in range(n_tiles):
        if j + MOD_BUFFERS - 1 < n_tiles:
            fetch(j + MOD_BUFFERS - 1).start()
        fetch(j).wait()
        cols = slice(j * MOD_TILE, (j + 1) * MOD_TILE)
        o_ref[:, cols] = _dot(cond, wbuf[j % MOD_BUFFERS])[0:1, :] + b_ref[:, cols]


def _modulation(c, w_ada, b_ada):
    n = w_ada.shape[1]
    assert n % MOD_TILE == 0
    vmem = pl.BlockSpec(memory_space=pltpu.VMEM)
    return pl.pallas_call(
        _mod_kernel,
        in_specs=[vmem, pl.BlockSpec(memory_space=pl.ANY), vmem],
        out_specs=vmem,
        out_shape=jax.ShapeDtypeStruct((1, n), F32),
        scratch_shapes=[pltpu.VMEM((MOD_BUFFERS, D_MODEL, MOD_TILE), F32),
                        pltpu.SemaphoreType.DMA((MOD_BUFFERS,))],
        compiler_params=pltpu.CompilerParams(vmem_limit_bytes=24 * MIB),
        name="adaln_modulation",
    )(c, w_ada, b_ada.reshape(1, n))


def _inproj_kernel(x_ref, nw_ref, shift_ref, scale_ref, wa_ref, wb_ref, wd_ref, inv_ref, wout_ref,
                   o_ref, dt_ref, wout_bf_ref, u_ref, prev, cos_in, sin_in, cos_t, sin_t, decay, *, n_tiles):
    t = pl.program_id(0)
    tile = jnp.minimum(t, n_tiles - 1)
    i = tile // N_COL_TILES
    j = tile - i * N_COL_TILES
    je = lax.rem(jnp.maximum(t - 1, 0), N_COL_TILES)
    half = RET_QK_DIM // 2

    @pl.when(t == 0)
    def _():
        prev[...] = jnp.zeros_like(prev)
        ang = lax.broadcasted_iota(jnp.int32, (TM_IN, RET_QK_DIM), 0).astype(F32) * inv_ref[...]
        cos_in[...] = jnp.cos(ang)
        sin_in[...] = jnp.sin(ang)
        pos = lax.broadcasted_iota(jnp.int32, (CHUNK, RET_QK_DIM), 0).astype(F32) + 1.0
        for hd, lg in enumerate(_log_gammas()):
            decay[0, hd] = jnp.exp(pos * lg)
            decay[1, hd] = jnp.exp(pos * -lg) * RET_QK_DIM ** -0.5

    @pl.when(jnp.logical_and(j == 0, t < n_tiles))
    def _():
        x = x_ref[...]
        gain = nw_ref[...] * (1.0 + scale_ref[...])
        u = (x * lax.rsqrt(jnp.mean(x * x, axis=-1, keepdims=True) + EPS) * gain + shift_ref[...]).astype(BF16)
        u_ref[...] = u
        is_dt = lax.broadcasted_iota(jnp.int32, (LANES, D_MODEL), 0) < SSD_HEADS
        dt_ref[...] = _dot_nt(u, jnp.where(is_dt, wd_ref[...], 0.0).astype(BF16))

        base = jnp.broadcast_to((i * TM_IN).astype(F32) * inv_ref[...], (SUBLANES, RET_QK_DIM))
        cos_b = jnp.cos(base)[0:1, :]
        sin_b = jnp.sin(base)[0:1, :]
        cos_t[...] = cos_in[...] * cos_b - sin_in[...] * sin_b
        sin_pos = sin_in[...] * cos_b + cos_in[...] * sin_b
        first_half = lax.broadcasted_iota(jnp.int32, (TM_IN, RET_QK_DIM), 1) < half
        sin_t[...] = jnp.where(first_half, -sin_pos, sin_pos)

    def project():
        after_dt = j >= TILE_ROT0
        skip = pl.multiple_of(jnp.where(after_dt, SSD_HEADS, 0), SSD_HEADS)
        head = wa_ref[pl.ds(skip, TN_IN - SSD_HEADS), :]
        tail = jnp.where(after_dt, wb_ref[...], wa_ref[TN_IN - SSD_HEADS:, :])
        w = jnp.concatenate([head, tail], axis=0).astype(BF16)
        wout_bf_ref[...] = wout_ref[...].astype(BF16)
        return _dot_nt(u_ref[...], w)

    @pl.when(jnp.logical_and(jnp.logical_or(je < TILE_CONV0, je >= TILE_G0), t < n_tiles))
    def _():
        acc = project()
        o_ref[...] = _silu(prev[...]).astype(BF16)
        prev[...] = acc

    @pl.when(t == n_tiles)
    def _():
        o_ref[...] = _silu(prev[...]).astype(BF16)

    @pl.when(jnp.logical_or(jnp.logical_and(je >= TILE_CONV0, je < TILE_ROT0),
                            jnp.logical_and(je >= TILE_V0, je < TILE_G0)))
    def _():
        acc = project()
        o_ref[...] = prev[...].astype(BF16)
        prev[...] = acc

    @pl.when(jnp.logical_and(je >= TILE_ROT0, je < TILE_V0))
    def _():
        acc = project()
        side = je - TILE_ROT0
        for hd in range(RET_HEADS):
            cols = slice(hd * RET_QK_DIM, (hd + 1) * RET_QK_DIM)
            scale = decay[side, hd]
            for blk in range(TM_IN // CHUNK):
                rows = slice(blk * CHUNK, (blk + 1) * CHUNK)
                a = prev[rows, cols]
                rot = a * cos_t[rows, :] + pltpu.roll(a, half, 1) * sin_t[rows, :]
                o_ref[rows, cols] = (rot * scale).astype(BF16)
        prev[...] = acc


def _in_projection(h, norm_w, mod, w_in_t, inv, w_out):
    L = h.shape[0]
    n_tiles = (L // TM_IN) * N_COL_TILES
    n_cast = min(1 << (n_tiles.bit_length() - 1), MIX_WIDTH // BF16_SUBLANES)
    cast_block = pl.BlockSpec((MIX_WIDTH // n_cast, D_MODEL), lambda t: (jnp.minimum(t, n_cast - 1), 0))
    const = lambda t: (0, 0)
    mul_row = lambda t: jnp.minimum(t, n_tiles - 1) // N_COL_TILES
    mul_col = lambda t: lax.rem(jnp.minimum(t, n_tiles - 1), N_COL_TILES)
    epi = lambda t: jnp.maximum(t - 1, 0)
    table = pltpu.VMEM((TM_IN, RET_QK_DIM), F32)
    return pl.pallas_call(
        functools.partial(_inproj_kernel, n_tiles=n_tiles),
        grid=(n_tiles + 1,),
        in_specs=[pl.BlockSpec((TM_IN, D_MODEL), lambda t: (mul_row(t), 0)),
                  pl.BlockSpec((1, D_MODEL), const),
                  pl.BlockSpec((1, D_MODEL), lambda t: (0, 0)),
                  pl.BlockSpec((1, D_MODEL), lambda t: (0, 1)),
                  pl.BlockSpec((TN_IN, D_MODEL), lambda t: (mul_col(t), 0)),
                  pl.BlockSpec((SSD_HEADS, D_MODEL), lambda t: ((mul_col(t) + 1) * (TN_IN // SSD_HEADS), 0)),
                  pl.BlockSpec((LANES, D_MODEL), lambda t: (OFF_XBC // LANES, 0)),
                  pl.BlockSpec((1, RET_QK_DIM), const),
                  cast_block],
        out_specs=[pl.BlockSpec((TM_IN, TN_IN), lambda t: (epi(t) // N_COL_TILES, lax.rem(epi(t), N_COL_TILES))),
                   pl.BlockSpec((TM_IN, LANES), lambda t: (mul_row(t), 0)),
                   cast_block],
        out_shape=[jax.ShapeDtypeStruct((L, PROJ_WIDTH), BF16),
                   jax.ShapeDtypeStruct((L, LANES), F32),
                   jax.ShapeDtypeStruct((MIX_WIDTH, D_MODEL), BF16)],
        scratch_shapes=[pltpu.VMEM((TM_IN, D_MODEL), BF16),
                        pltpu.VMEM((TM_IN, TN_IN), F32),
                        table, table, table, table,
                        pltpu.VMEM((2, RET_HEADS, CHUNK, RET_QK_DIM), F32)],
        compiler_params=pltpu.CompilerParams(dimension_semantics=("arbitrary",),
                                             vmem_limit_bytes=60 * MIB),
        name="adaln_in_projection",
    )(h, norm_w.reshape(1, D_MODEL), mod, mod, w_in_t, w_in_t, w_in_t, inv, w_out)


def _ssd_init(xbuf, shifts, state, expand):
    xbuf[0:CONV_TAIL, :] = jnp.zeros((CONV_TAIL, SSD_CONV_DIM), BF16)
    row = lax.broadcasted_iota(jnp.int32, shifts.shape, 0)
    col = lax.broadcasted_iota(jnp.int32, shifts.shape, 1)
    delay = jnp.right_shift(row, CHUNK.bit_length() - 1) + 1
    t = jnp.bitwise_and(row, CHUNK - 1)
    shifts[...] = jnp.where(col == CONV_TAIL + t - delay, 1.0, 0.0).astype(BF16)
    state[...] = jnp.zeros_like(state)
    head_of_lane = jnp.right_shift(lax.broadcasted_iota(jnp.int32, (LANES, SSD_D_INNER), 1), HEAD_DIM_LOG2)
    row = lax.broadcasted_iota(jnp.int32, (LANES, SSD_D_INNER), 0)
    expand[...] = jnp.where(head_of_lane == row, 1.0, 0.0).astype(BF16)


def _ssd_chunk(z_ref, x_ref, b_ref, c_ref, dtraw_ref, cw_ref, cb_ref, dtb_ref, alog_ref, dskip_ref, nw_ref,
               y_ref, xbuf, shifts, state, expand):
    int_iota = lambda shape, dim: lax.broadcasted_iota(jnp.int32, shape, dim)

    xbuf[CONV_TAIL:, 0:SSD_D_INNER] = x_ref[...]
    xbuf[CONV_TAIL:, SSD_D_INNER:SSD_D_INNER + SSD_BC] = b_ref[...]
    xbuf[CONV_TAIL:, SSD_D_INNER + SSD_BC:] = c_ref[...]
    delayed = _dot(shifts[...], xbuf[...])
    acc = cb_ref[...] + cw_ref[SSD_CONV - 1:SSD_CONV, :] * xbuf[CONV_TAIL:, :].astype(F32)
    for k in range(SSD_CONV - 1):
        tap = SSD_CONV - 2 - k
        acc = acc + cw_ref[tap:tap + 1, :] * delayed[k * CHUNK:(k + 1) * CHUNK, :]
    xbuf[0:CONV_TAIL, :] = xbuf[CHUNK:CHUNK + CONV_TAIL, :]
    xbc = _silu(acc)
    xs = xbc[:, 0:SSD_D_INNER]
    xs_b = xs.astype(BF16)
    bmat = xbc[:, SSD_D_INNER:SSD_D_INNER + SSD_BC]
    cmat = xbc[:, SSD_D_INNER + SSD_BC:]

    pre = dtraw_ref[...] + dtb_ref[...]
    dt = jnp.maximum(pre, 0.0) + jnp.log1p(jnp.exp(-jnp.abs(pre)))
    d_a = dt * (-LOG2_E * jnp.exp(alog_ref[...]))
    causal = int_iota((CHUNK, CHUNK), 0) >= int_iota((CHUNK, CHUNK), 1)
    tril = jnp.where(causal, 1.0, 0.0).astype(BF16)
    acs = sum(_dot(tril, part) for part in _split3(d_a))
    acs_t = acs.T
    src_t = acs_t - jnp.log2(dt).T
    w_t = jnp.exp2(acs_t[:, CHUNK - 1:CHUNK] - src_t)
    end = jnp.broadcast_to(jnp.exp2(acs[CHUNK - 1:CHUNK, :]), (SUBLANES, LANES))
    end_row = sum(_dot(part, expand[...]) for part in _split3(end))[0:1, :]

    lane_blk = jnp.right_shift(int_iota((CHUNK, QUAD_W), 1), HEAD_DIM_LOG2)

    def block_diag(v):
        return jnp.concatenate([jnp.where(lane_blk == j, v, jnp.zeros_like(v)) for j in range(QUAD)], axis=0)

    def group(g, between_quads=lambda q: None):
        gs = slice(g * SSD_STATE, (g + 1) * SSD_STATE)
        b_g = bmat[:, gs]
        c_g = cmat[:, gs]
        cb_g = _dot_nt(c_g.astype(BF16), b_g.astype(BF16))
        b_gt = b_g.T
        y_parts = []
        for qi in range(HEADS_PER_GROUP // QUAD):
            q = g * (HEADS_PER_GROUP // QUAD) + qi
            between_quads(q)
            qs = slice(q * QUAD_W, (q + 1) * QUAD_W)
            lhs_diag, lhs_state, lhs_off = [], [], []
            for j in range(QUAD):
                hd = q * QUAD + j
                col = jnp.broadcast_to(acs[:, hd:hd + 1], (CHUNK, CHUNK))
                decay_dt = jnp.exp2(jnp.where(causal, col - src_t[hd:hd + 1, :], -jnp.inf))
                lhs_diag.append((cb_g * decay_dt).astype(BF16))
                lhs_state.append((b_gt * w_t[hd:hd + 1, :]).astype(BF16))
                lhs_off.append((c_g * jnp.exp2(col)).astype(BF16))
            x_blk = block_diag(xs_b[:, qs])
            s_prev = state[:, qs]
            s_blk = block_diag(s_prev.astype(BF16))
            y_q = _dot(jnp.concatenate(lhs_diag, axis=1), x_blk)
            y_q = y_q + _dot(jnp.concatenate(lhs_off, axis=1), s_blk)
            state[:, qs] = s_prev * end_row[:, qs] + _dot(jnp.concatenate(lhs_state, axis=1), x_blk)
            y_parts.append(y_q + xs[:, qs] * dskip_ref[:, qs])
        ws = slice(g * GROUP_W, (g + 1) * GROUP_W)
        u = jnp.concatenate(y_parts, axis=1) * z_ref[:, ws].astype(F32)
        u = u * lax.rsqrt(jnp.mean(u * u, axis=-1, keepdims=True) + EPS)
        y_ref[:, ws] = (u * nw_ref[:, ws]).astype(BF16)

    return group


def _ret_chunk(q_ref, k_ref, v0_ref, v1_ref, g0_ref, g1_ref, nw_ref, y_ref, state):
    heads_per_block = RET_QK_WIDTH // RET_V_DIM
    log_gammas = _log_gammas()
    causal = (lax.broadcasted_iota(jnp.int32, (CHUNK, CHUNK), 0)
              >= lax.broadcasted_iota(jnp.int32, (CHUNK, CHUNK), 1))

    def head(hd):
        ks = slice(hd * RET_QK_DIM, (hd + 1) * RET_QK_DIM)
        vs = slice(hd * RET_V_DIM, (hd + 1) * RET_V_DIM)
        v_ref, g_ref = (v0_ref, g0_ref) if hd < heads_per_block else (v1_ref, g1_ref)
        bs = slice((hd % heads_per_block) * RET_V_DIM, (hd % heads_per_block + 1) * RET_V_DIM)
        q_h = q_ref[:, ks]
        k_h = k_ref[:, ks]
        v_h = v_ref[:, bs]
        scores = jnp.where(causal, _dot_nt(q_h, k_h), 0.0)
        s_prev = state[hd]
        o = _dot(scores.astype(BF16), v_h) + _dot(q_h, s_prev.astype(BF16))
        state[hd] = (s_prev + _dot_tn(k_h, v_h)) * float(np.exp(np.float32(CHUNK * log_gammas[hd])))
        mu = jnp.mean(o, axis=-1, keepdims=True)
        d = o - mu
        y = d * lax.rsqrt(jnp.mean(d * d, axis=-1, keepdims=True) + EPS) * nw_ref[:, vs]
        y = (y * g_ref[:, bs].astype(F32)).astype(BF16)
        y_ref[:, SSD_D_INNER + hd * RET_V_DIM:SSD_D_INNER + (hd + 1) * RET_V_DIM] = y

    return head


def _mixer_kernel(z_ref, x_ref, b_ref, c_ref, dtraw_ref, cw_ref, cb_ref, dtb_ref, alog_ref, dskip_ref, snw_ref,
                  q_ref, k_ref, v0_ref, v1_ref, g0_ref, g1_ref, rnw_ref,
                  wout_hbm, h_ref, gate_ref, fw_ref, o_ref,
                  xbuf, shifts, ssd_state, expand, ret_state, ybuf, wout, wsem, *, final):
    step = pl.program_id(0)

    @pl.when(step == 0)
    def _():
        wout_copy = pltpu.make_async_copy(wout_hbm, wout, wsem.at[0])
        wout_copy.start()
        _ssd_init(xbuf, shifts, ssd_state, expand)
        ret_state[...] = jnp.zeros_like(ret_state)
        ybuf[...] = jnp.zeros_like(ybuf)
        wout_copy.wait()

    slot = lax.rem(step, 2)
    y_old = ybuf[1 - slot]
    piece_w = D_MODEL // RET_HEADS
    pieces = []
    for sub in range(CHUNKS_PER_STEP):
        rows = pl.ds(sub * CHUNK, CHUNK)
        sub_refs = lambda *refs: [r.at[rows] for r in refs]
        y_new = ybuf.at[slot, rows]
        z_c, x_c, b_c, c_c, dtraw_c = sub_refs(z_ref, x_ref, b_ref, c_ref, dtraw_ref)
        ssd_group = _ssd_chunk(z_c, x_c, b_c, c_c, dtraw_c, cw_ref, cb_ref, dtb_ref, alog_ref, dskip_ref, snw_ref,
                               y_new, xbuf, shifts, ssd_state, expand)
        q_c, k_c, v0_c, v1_c, g0_c, g1_c = sub_refs(q_ref, k_ref, v0_ref, v1_ref, g0_ref, g1_ref)
        ret_head = _ret_chunk(q_c, k_c, v0_c, v1_c, g0_c, g1_c, rnw_ref, y_new, ret_state)

        def between_quads(q):
            if q % CHUNKS_PER_STEP == 0:
                n = len(pieces)
                pieces.append(_dot(y_old, wout[:, n * piece_w:(n + 1) * piece_w]))
            ret_head(q)

        for g in range(SSD_GROUPS):
            ssd_group(g, between_quads)

    h = h_ref[...] + gate_ref[...] * jnp.concatenate(pieces, axis=1)
    if final:
        h = h * lax.rsqrt(jnp.mean(h * h, axis=-1, keepdims=True) + EPS) * fw_ref[...]
    o_ref[...] = h


def _mix_and_project(proj, dt_raw, h, mod, conv_w, conv_b, dt_bias, a_log, d_skip, ssd_norm_w, ret_norm_w, w_out,
                     final_norm_w, final):
    L = h.shape[0]
    rows = CHUNKS_PER_STEP * CHUNK
    n_steps = L // rows
    pad = lambda v: jnp.pad(v.reshape(1, SSD_HEADS), ((0, 0), (0, LANES - SSD_HEADS)))
    full = lambda shape: pl.BlockSpec(shape, lambda s: (0, 0))
    mixed = lambda width, col: pl.BlockSpec((rows, width), lambda s: (jnp.minimum(s, n_steps - 1), col))
    projected = lambda width, col: pl.BlockSpec((rows, width), lambda s: (jnp.maximum(s - 1, 0), col))
    return pl.pallas_call(
        functools.partial(_mixer_kernel, final=final),
        grid=(n_steps + 1,),
        in_specs=[mixed(SSD_D_INNER, COL_Z), mixed(SSD_D_INNER, COL_X), mixed(SSD_BC, COL_B), mixed(SSD_BC, COL_C),
                  mixed(LANES, 0),
                  full((SSD_CONV, SSD_CONV_DIM)), full((1, SSD_CONV_DIM)), full((1, LANES)), full((1, LANES)),
                  full((1, SSD_D_INNER)), full((1, SSD_D_INNER)),
                  mixed(RET_QK_WIDTH, COL_Q), mixed(RET_QK_WIDTH, COL_K),
                  mixed(RET_QK_WIDTH, COL_V), mixed(RET_QK_WIDTH, COL_V + 1),
                  mixed(RET_QK_WIDTH, COL_G), mixed(RET_QK_WIDTH, COL_G + 1),
                  full((1, RET_V_WIDTH)),
                  pl.BlockSpec(memory_space=pl.ANY),
                  projected(D_MODEL, 0),
                  pl.BlockSpec((1, D_MODEL), lambda s: (0, 2)),
                  full((1, D_MODEL))],
        out_specs=projected(D_MODEL, 0),
        out_shape=jax.ShapeDtypeStruct((L, D_MODEL), F32),
        scratch_shapes=[pltpu.VMEM((CONV_TAIL + CHUNK, SSD_CONV_DIM), BF16),
                        pltpu.VMEM(((SSD_CONV - 1) * CHUNK, CONV_TAIL + CHUNK), BF16),
                        pltpu.VMEM((SSD_STATE, SSD_D_INNER), F32),
                        pltpu.VMEM((LANES, SSD_D_INNER), BF16),
                        pltpu.VMEM((RET_HEADS, RET_QK_DIM, RET_V_DIM), F32),
                        pltpu.VMEM((2, rows, MIX_WIDTH), BF16),
                        pltpu.VMEM((MIX_WIDTH, D_MODEL), BF16),
                        pltpu.SemaphoreType.DMA((1,))],
        compiler_params=pltpu.CompilerParams(dimension_semantics=("arbitrary",),
                                             vmem_limit_bytes=60 * MIB),
        name="mixers_out_projection",
    )(proj, proj, proj, proj, dt_raw, conv_w, conv_b.reshape(1, SSD_CONV_DIM), pad(dt_bias), pad(a_log),
      jnp.repeat(d_skip, SSD_HEAD_DIM).reshape(1, SSD_D_INNER), ssd_norm_w.reshape(1, SSD_D_INNER),
      proj, proj, proj, proj, proj, proj, ret_norm_w.reshape(1, RET_V_WIDTH),
      w_out, h, mod, final_norm_w.reshape(1, D_MODEL))


def kernel(x, c, w_ada, b_ada, norm_w, w_in, conv_w, conv_b, dt_bias, a_log, d_skip, ssd_norm_w, ret_norm_w,
           w_out, final_norm_w):
    bsz, L, d_model = x.shape
    assert bsz == 1 and d_model == D_MODEL and L % TM_IN == 0 and w_in.shape[-1] == IN_WIDTH
    depth = w_in.shape[0]
    half = RET_QK_DIM // 2
    inv = ROPE_BASE ** (-jnp.arange(half, dtype=F32) / half)
    inv = jnp.concatenate([inv, inv]).reshape(1, RET_QK_DIM)
    h = x.reshape(L, D_MODEL)
    for layer in range(depth):
        mod = _modulation(c, w_ada[layer], b_ada[layer])
        proj, dt_raw, w_out_bf = _in_projection(h, norm_w[layer], mod, jnp.swapaxes(w_in[layer], 0, 1), inv,
                                                w_out[layer])
        h = _mix_and_project(proj, dt_raw, h, mod, conv_w[layer], conv_b[layer], dt_bias[layer], a_log[layer],
                             d_skip[layer], ssd_norm_w[layer], ret_norm_w[layer], w_out_bf, final_norm_w,
                             final=layer == depth - 1)
    return h.reshape(bsz, L, D_MODEL)
```

```python
import functools
import math

import numpy as np
import jax
import jax.numpy as jnp
from jax import lax
from jax.experimental import pallas as pl
from jax.experimental.pallas import tpu as pltpu

D_MODEL = 2048
SSD_D_INNER = D_MODEL
SSD_HEAD_DIM = 64
SSD_HEADS = SSD_D_INNER // SSD_HEAD_DIM
HEAD_DIM_LOG2 = SSD_HEAD_DIM.bit_length() - 1
SSD_GROUPS = 4
SSD_STATE = 128
SSD_CONV = 4
SSD_BC = SSD_GROUPS * SSD_STATE
SSD_CONV_DIM = SSD_D_INNER + 2 * SSD_BC
RET_HEADS = 8
RET_QK_DIM = 128
RET_V_DIM = 256
RET_QK_WIDTH = RET_HEADS * RET_QK_DIM
RET_V_WIDTH = RET_HEADS * RET_V_DIM
MIX_WIDTH = SSD_D_INNER + RET_V_WIDTH
CHUNK = 128
ROPE_BASE = 10000.0
EPS = 1e-6
LOG2_E = math.log2(math.e)

OFF_Z = SSD_D_INNER
OFF_XBC = OFF_Z + SSD_CONV_DIM
OFF_DT = OFF_XBC + SSD_HEADS
OFF_Q = OFF_DT + RET_QK_WIDTH
OFF_K = OFF_Q + RET_QK_WIDTH
OFF_V = OFF_K + RET_V_WIDTH
IN_WIDTH = OFF_V + RET_V_WIDTH

PROJ_WIDTH = IN_WIDTH - SSD_HEADS
COL_Z, COL_X = 0, 1
COL_B, COL_C = 8, 9
COL_Q, COL_K, COL_V, COL_G = 5, 6, 7, 9

LANES = 128
SUBLANES = 8
BF16_SUBLANES = 16
QUAD = 4
QUAD_W = QUAD * SSD_HEAD_DIM
HEADS_PER_GROUP = SSD_HEADS // SSD_GROUPS
GROUP_W = SSD_D_INNER // SSD_GROUPS
CONV_TAIL = BF16_SUBLANES
CHUNKS_PER_STEP = 2

MOD_TILE = 512
MOD_BUFFERS = 3
TM_IN, TN_IN = 1024, 1024
TILE_CONV0, TILE_ROT0, TILE_V0, TILE_G0 = (c // TN_IN for c in (OFF_Z, OFF_XBC, OFF_XBC + 2 * RET_QK_WIDTH,
                                                                 OFF_XBC + 2 * RET_QK_WIDTH + RET_V_WIDTH))
N_COL_TILES = PROJ_WIDTH // TN_IN
assert OFF_Z % TN_IN == 0 and OFF_XBC % TN_IN == 0 and RET_QK_WIDTH == TN_IN and RET_V_WIDTH % TN_IN == 0
assert PROJ_WIDTH % TN_IN == 0 and TM_IN % CHUNK == 0 and N_COL_TILES - 1 >= TILE_G0
MIB = 1024 * 1024

F32 = jnp.float32
BF16 = jnp.bfloat16


def _silu(v):
    return v / (1.0 + jnp.exp(-v))


def _dot(a, b):
    return jnp.dot(a, b, preferred_element_type=F32)


def _dot_nt(a, b):
    return lax.dot_general(a, b, (((1,), (1,)), ((), ())), preferred_element_type=F32)


def _dot_tn(a, b):
    return lax.dot_general(a, b, (((0,), (0,)), ((), ())), preferred_element_type=F32)


def _split3(v):
    hi = v.astype(BF16)
    r1 = v - hi.astype(F32)
    mid = r1.astype(BF16)
    lo = (r1 - mid.astype(F32)).astype(BF16)
    return hi, mid, lo


def _log_gammas():
    return [float(np.log1p(-np.exp2(np.float32(-5.0 - hd)), dtype=np.float32)) for hd in range(RET_HEADS)]


def _mod_kernel(c_ref, w_hbm, b_ref, o_ref, wbuf, sem):
    n_tiles = o_ref.shape[1] // MOD_TILE

    def fetch(j):
        slot = j % MOD_BUFFERS
        return pltpu.make_async_copy(w_hbm.at[:, pl.ds(j * MOD_TILE, MOD_TILE)], wbuf.at[slot], sem.at[slot])

    for j in range(min(MOD_BUFFERS - 1, n_tiles)):
        fetch(j).start()
    cond = jnp.broadcast_to(_silu(c_ref[...]), (SUBLANES, D_MODEL))
    for j in range(n_tiles):
        if j + MOD_BUFFERS - 1 < n_tiles:
            fetch(j + MOD_BUFFERS - 1).start()
        fetch(j).wait()
        cols = slice(j * MOD_TILE, (j + 1) * MOD_TILE)
        o_ref[:, cols] = _dot(cond, wbuf[j % MOD_BUFFERS])[0:1, :] + b_ref[:, cols]


def _modulation(c, w_ada, b_ada):
    n = w_ada.shape[1]
    assert n % MOD_TILE == 0
    vmem = pl.BlockSpec(memory_space=pltpu.VMEM)
    return pl.pallas_call(
        _mod_kernel,
        in_specs=[vmem, pl.BlockSpec(memory_space=pl.ANY), vmem],
        out_specs=vmem,
        out_shape=jax.ShapeDtypeStruct((1, n), F32),
        scratch_shapes=[pltpu.VMEM((MOD_BUFFERS, D_MODEL, MOD_TILE), F32),
                        pltpu.SemaphoreType.DMA((MOD_BUFFERS,))],
        compiler_params=pltpu.CompilerParams(vmem_limit_bytes=24 * MIB),
        name="adaln_modulation",
    )(c, w_ada, b_ada.reshape(1, n))


def _inproj_kernel(x_ref, nw_ref, shift_ref, scale_ref, wa_ref, wb_ref, wd_ref, inv_ref, wout_ref,
                   o_ref, dt_ref, wout_bf_ref, u_ref, prev, cos_in, sin_in, cos_t, sin_t, decay, *, n_tiles):
    t = pl.program_id(0)
    tile = jnp.minimum(t, n_tiles - 1)
    i = tile // N_COL_TILES
    j = tile - i * N_COL_TILES
    je = lax.rem(jnp.maximum(t - 1, 0), N_COL_TILES)
    half = RET_QK_DIM // 2

    @pl.when(t == 0)
    def _():
        prev[...] = jnp.zeros_like(prev)
        rows = lambda n: lax.broadcasted_iota(jnp.int32, (n, RET_QK_DIM), 0).astype(F32)
        ang_r = rows(CHUNK) * inv_ref[...]
        ang_blk = rows(TM_IN // CHUNK) * float(CHUNK) * inv_ref[...]
        cos_r, sin_r, cos_blk, sin_blk = jnp.cos(ang_r), jnp.sin(ang_r), jnp.cos(ang_blk), jnp.sin(ang_blk)
        for blk in range(TM_IN // CHUNK):
            chunk = slice(blk * CHUNK, (blk + 1) * CHUNK)
            cb, sb = cos_blk[blk:blk + 1, :], sin_blk[blk:blk + 1, :]
            cos_in[chunk, :] = cos_r * cb - sin_r * sb
            sin_in[chunk, :] = sin_r * cb + cos_r * sb
        pos = lax.broadcasted_iota(jnp.int32, (CHUNK, RET_QK_DIM), 0).astype(F32) + 1.0
        for hd, lg in enumerate(_log_gammas()):
            decay[0, hd] = jnp.exp(pos * lg)
            decay[1, hd] = jnp.exp(pos * -lg) * RET_QK_DIM ** -0.5

    @pl.when(jnp.logical_and(j == 0, t < n_tiles))
    def _():
        x = x_ref[...]
        gain = nw_ref[...] * (1.0 + scale_ref[...])
        u = (x * lax.rsqrt(jnp.mean(x * x, axis=-1, keepdims=True) + EPS) * gain + shift_ref[...]).astype(BF16)
        u_ref[...] = u
        is_dt = lax.broadcasted_iota(jnp.int32, (LANES, D_MODEL), 0) < SSD_HEADS
        dt_ref[...] = _dot_nt(u, jnp.where(is_dt, wd_ref[...], 0.0).astype(BF16))

        base = jnp.broadcast_to((i * TM_IN).astype(F32) * inv_ref[...], (SUBLANES, RET_QK_DIM))
        cos_b = jnp.cos(base)[0:1, :]
        sin_b = jnp.sin(base)[0:1, :]
        cos_t[...] = cos_in[...] * cos_b - sin_in[...] * sin_b
        sin_pos = sin_in[...] * cos_b + cos_in[...] * sin_b
        first_half = lax.broadcasted_iota(jnp.int32, (TM_IN, RET_QK_DIM), 1) < half
        sin_t[...] = jnp.where(first_half, -sin_pos, sin_pos)

    def project():
        after_dt = j >= TILE_ROT0
        skip = pl.multiple_of(jnp.where(after_dt, SSD_HEADS, 0), SSD_HEADS)
        head = wa_ref[pl.ds(skip, TN_IN - SSD_HEADS), :]
        tail = jnp.where(after_dt, wb_ref[...], wa_ref[TN_IN - SSD_HEADS:, :])
        w = jnp.concatenate([head, tail], axis=0).astype(BF16)
        wout_bf_ref[...] = wout_ref[...].astype(BF16)
        return _dot_nt(u_ref[...], w)

    @pl.when(jnp.logical_and(jnp.logical_or(je < TILE_CONV0, je >= TILE_G0), t < n_tiles))
    def _():
        acc = project()
        o_ref[...] = _silu(prev[...]).astype(BF16)
        prev[...] = acc

    @pl.when(t == n_tiles)
    def _():
        o_ref[...] = _silu(prev[...]).astype(BF16)

    @pl.when(jnp.logical_or(jnp.logical_and(je >= TILE_CONV0, je < TILE_ROT0),
                            jnp.logical_and(je >= TILE_V0, je < TILE_G0)))
    def _():
        acc = project()
        o_ref[...] = prev[...].astype(BF16)
        prev[...] = acc

    @pl.when(jnp.logical_and(je >= TILE_ROT0, je < TILE_V0))
    def _():
        acc = project()
        side = je - TILE_ROT0
        for hd in range(RET_HEADS):
            cols = slice(hd * RET_QK_DIM, (hd + 1) * RET_QK_DIM)
            scale = decay[side, hd]
            for blk in range(TM_IN // CHUNK):
                rows = slice(blk * CHUNK, (blk + 1) * CHUNK)
                a = prev[rows, cols]
                rot = a * cos_t[rows, :] + pltpu.roll(a, half, 1) * sin_t[rows, :]
                o_ref[rows, cols] = (rot * scale).astype(BF16)
        prev[...] = acc


def _in_projection(h, norm_w, mod, w_in_t, inv, w_out):
    L = h.shape[0]
    n_tiles = (L // TM_IN) * N_COL_TILES
    n_cast = min(1 << (n_tiles.bit_length() - 1), MIX_WIDTH // BF16_SUBLANES)
    cast_block = pl.BlockSpec((MIX_WIDTH // n_cast, D_MODEL), lambda t: (jnp.minimum(t, n_cast - 1), 0))
    const = lambda t: (0, 0)
    mul_row = lambda t: jnp.minimum(t, n_tiles - 1) // N_COL_TILES
    mul_col = lambda t: lax.rem(jnp.minimum(t, n_tiles - 1), N_COL_TILES)
    epi = lambda t: jnp.maximum(t - 1, 0)
    table = pltpu.VMEM((TM_IN, RET_QK_DIM), F32)
    return pl.pallas_call(
        functools.partial(_inproj_kernel, n_tiles=n_tiles),
        grid=(n_tiles + 1,),
        in_specs=[pl.BlockSpec((TM_IN, D_MODEL), lambda t: (mul_row(t), 0)),
                  pl.BlockSpec((1, D_MODEL), const),
                  pl.BlockSpec((1, D_MODEL), lambda t: (0, 0)),
                  pl.BlockSpec((1, D_MODEL), lambda t: (0, 1)),
                  pl.BlockSpec((TN_IN, D_MODEL), lambda t: (mul_col(t), 0)),
                  pl.BlockSpec((SSD_HEADS, D_MODEL), lambda t: ((mul_col(t) + 1) * (TN_IN // SSD_HEADS), 0)),
                  pl.BlockSpec((LANES, D_MODEL), lambda t: (OFF_XBC // LANES, 0)),
                  pl.BlockSpec((1, RET_QK_DIM), const),
                  cast_block],
        out_specs=[pl.BlockSpec((TM_IN, TN_IN), lambda t: (epi(t) // N_COL_TILES, lax.rem(epi(t), N_COL_TILES))),
                   pl.BlockSpec((TM_IN, LANES), lambda t: (mul_row(t), 0)),
                   cast_block],
        out_shape=[jax.ShapeDtypeStruct((L, PROJ_WIDTH), BF16),
                   jax.ShapeDtypeStruct((L, LANES), F32),
                   jax.ShapeDtypeStruct((MIX_WIDTH, D_MODEL), BF16)],
        scratch_shapes=[pltpu.VMEM((TM_IN, D_MODEL), BF16),
                        pltpu.VMEM((TM_IN, TN_IN), F32),
                        table, table, table, table,
                        pltpu.VMEM((2, RET_HEADS, CHUNK, RET_QK_DIM), F32)],
        compiler_params=pltpu.CompilerParams(dimension_semantics=("arbitrary",),
                                             vmem_limit_bytes=60 * MIB),
        name="adaln_in_projection",
    )(h, norm_w.reshape(1, D_MODEL), mod, mod, w_in_t, w_in_t, w_in_t, inv, w_out)


def _ssd_init(xbuf, shifts, state, expand):
    xbuf[0:CONV_TAIL, :] = jnp.zeros((CONV_TAIL, SSD_CONV_DIM), BF16)
    row = lax.broadcasted_iota(jnp.int32, shifts.shape, 0)
    col = lax.broadcasted_iota(jnp.int32, shifts.shape, 1)
    delay = jnp.right_shift(row, CHUNK.bit_length() - 1) + 1
    t = jnp.bitwise_and(row, CHUNK - 1)
    shifts[...] = jnp.where(col == CONV_TAIL + t - delay, 1.0, 0.0).astype(BF16)
    state[...] = jnp.zeros_like(state)
    head_of_lane = jnp.right_shift(lax.broadcasted_iota(jnp.int32, (LANES, SSD_D_INNER), 1), HEAD_DIM_LOG2)
    row = lax.broadcasted_iota(jnp.int32, (LANES, SSD_D_INNER), 0)
    expand[...] = jnp.where(head_of_lane == row, 1.0, 0.0).astype(BF16)


def _ssd_chunk(z_ref, x_ref, b_ref, c_ref, dtraw_ref, cw_ref, cb_ref, dtb_ref, alog_ref, dskip_ref, nw_ref,
               y_ref, xbuf, shifts, state, expand):
    int_iota = lambda shape, dim: lax.broadcasted_iota(jnp.int32, shape, dim)

    xbuf[CONV_TAIL:, 0:SSD_D_INNER] = x_ref[...]
    xbuf[CONV_TAIL:, SSD_D_INNER:SSD_D_INNER + SSD_BC] = b_ref[...]
    xbuf[CONV_TAIL:, SSD_D_INNER + SSD_BC:] = c_ref[...]
    delayed = _dot(shifts[...], xbuf[...])
    acc = cb_ref[...] + cw_ref[SSD_CONV - 1:SSD_CONV, :] * xbuf[CONV_TAIL:, :].astype(F32)
    for k in range(SSD_CONV - 1):
        tap = SSD_CONV - 2 - k
        acc = acc + cw_ref[tap:tap + 1, :] * delayed[k * CHUNK:(k + 1) * CHUNK, :]
    xbuf[0:CONV_TAIL, :] = xbuf[CHUNK:CHUNK + CONV_TAIL, :]
    xbc = _silu(acc)
    xs = xbc[:, 0:SSD_D_INNER]
    xs_b = xs.astype(BF16)
    bmat = xbc[:, SSD_D_INNER:SSD_D_INNER + SSD_BC]
    cmat = xbc[:, SSD_D_INNER + SSD_BC:]

    pre = dtraw_ref[...] + dtb_ref[...]
    dt = jnp.maximum(pre, 0.0) + jnp.log1p(jnp.exp(-jnp.abs(pre)))
    d_a = dt * (-LOG2_E * jnp.exp(alog_ref[...]))
    causal = int_iota((CHUNK, CHUNK), 0) >= int_iota((CHUNK, CHUNK), 1)
    tril = jnp.where(causal, 1.0, 0.0).astype(BF16)
    acs = sum(_dot(tril, part) for part in _split3(d_a))
    acs_t = acs.T
    src_t = acs_t - jnp.log2(dt).T
    w_t = jnp.exp2(acs_t[:, CHUNK - 1:CHUNK] - src_t)
    end = jnp.broadcast_to(jnp.exp2(acs[CHUNK - 1:CHUNK, :]), (SUBLANES, LANES))
    end_row = sum(_dot(part, expand[...]) for part in _split3(end))[0:1, :]

    lane_blk = jnp.right_shift(int_iota((CHUNK, QUAD_W), 1), HEAD_DIM_LOG2)

    def block_diag(v):
        return jnp.concatenate([jnp.where(lane_blk == j, v, jnp.zeros_like(v)) for j in range(QUAD)], axis=0)

    def group(g, between_quads=lambda q: None):
        gs = slice(g * SSD_STATE, (g + 1) * SSD_STATE)
        b_g = bmat[:, gs]
        c_g = cmat[:, gs]
        cb_g = _dot_nt(c_g.astype(BF16), b_g.astype(BF16))
        b_gt = b_g.T
        y_parts = []
        for qi in range(HEADS_PER_GROUP // QUAD):
            q = g * (HEADS_PER_GROUP // QUAD) + qi
            between_quads(q)
            qs = slice(q * QUAD_W, (q + 1) * QUAD_W)
            lhs_diag, lhs_state, lhs_off = [], [], []
            for j in range(QUAD):
                hd = q * QUAD + j
                col = jnp.broadcast_to(acs[:, hd:hd + 1], (CHUNK, CHUNK))
                decay_dt = jnp.exp2(jnp.where(causal, col - src_t[hd:hd + 1, :], -jnp.inf))
                lhs_diag.append((cb_g * decay_dt).astype(BF16))
                lhs_state.append((b_gt * w_t[hd:hd + 1, :]).astype(BF16))
                lhs_off.append((c_g * jnp.exp2(col)).astype(BF16))
            x_blk = block_diag(xs_b[:, qs])
            s_prev = state[:, qs]
            s_blk = block_diag(s_prev.astype(BF16))
            y_q = _dot(jnp.concatenate(lhs_diag, axis=1), x_blk)
            y_q = y_q + _dot(jnp.concatenate(lhs_off, axis=1), s_blk)
            state[:, qs] = s_prev * end_row[:, qs] + _dot(jnp.concatenate(lhs_state, axis=1), x_blk)
            y_parts.append(y_q + xs[:, qs] * dskip_ref[:, qs])
        ws = slice(g * GROUP_W, (g + 1) * GROUP_W)
        u = jnp.concatenate(y_parts, axis=1) * z_ref[:, ws].astype(F32)
        u = u * lax.rsqrt(jnp.mean(u * u, axis=-1, keepdims=True) + EPS)
        y_ref[:, ws] = (u * nw_ref[:, ws]).astype(BF16)

    return group


def _ret_chunk(q_ref, k_ref, v0_ref, v1_ref, g0_ref, g1_ref, nw_ref, y_ref, state):
    heads_per_block = RET_QK_WIDTH // RET_V_DIM
    log_gammas = _log_gammas()
    causal = (lax.broadcasted_iota(jnp.int32, (CHUNK, CHUNK), 0)
              >= lax.broadcasted_iota(jnp.int32, (CHUNK, CHUNK), 1))

    def head(hd):
        ks = slice(hd * RET_QK_DIM, (hd + 1) * RET_QK_DIM)
        vs = slice(hd * RET_V_DIM, (hd + 1) * RET_V_DIM)
        v_ref, g_ref = (v0_ref, g0_ref) if hd < heads_per_block else (v1_ref, g1_ref)
        bs = slice((hd % heads_per_block) * RET_V_DIM, (hd % heads_per_block + 1) * RET_V_DIM)
        q_h = q_ref[:, ks]
        k_h = k_ref[:, ks]
        v_h = v_ref[:, bs]
        scores = jnp.where(causal, _dot_nt(q_h, k_h), 0.0)
        s_prev = state[hd]
        o = _dot(scores.astype(BF16), v_h) + _dot(q_h, s_prev.astype(BF16))
        state[hd] = (s_prev + _dot_tn(k_h, v_h)) * float(np.exp(np.float32(CHUNK * log_gammas[hd])))
        mu = jnp.mean(o, axis=-1, keepdims=True)
        d = o - mu
        y = d * lax.rsqrt(jnp.mean(d * d, axis=-1, keepdims=True) + EPS) * nw_ref[:, vs]
        y = (y * g_ref[:, bs].astype(F32)).astype(BF16)
        y_ref[:, SSD_D_INNER + hd * RET_V_DIM:SSD_D_INNER + (hd + 1) * RET_V_DIM] = y

    return head


def _mixer_kernel(z_ref, x_ref, b_ref, c_ref, dtraw_ref, cw_ref, cb_ref, dtb_ref, alog_ref, dskip_ref, snw_ref,
                  q_ref, k_ref, v0_ref, v1_ref, g0_ref, g1_ref, rnw_ref,
                  wout_hbm, h_ref, gate_ref, fw_ref, o_ref,
                  xbuf, shifts, ssd_state, expand, ret_state, ybuf, wout, wsem, *, final):
    step = pl.program_id(0)

    @pl.when(step == 0)
    def _():
        wout_copy = pltpu.make_async_copy(wout_hbm, wout, wsem.at[0])
        wout_copy.start()
        _ssd_init(xbuf, shifts, ssd_state, expand)
        ret_state[...] = jnp.zeros_like(ret_state)
        ybuf[...] = jnp.zeros_like(ybuf)
        wout_copy.wait()

    slot = lax.rem(step, 2)
    y_old = ybuf[1 - slot]
    piece_w = D_MODEL // RET_HEADS
    pieces = []
    for sub in range(CHUNKS_PER_STEP):
        rows = pl.ds(sub * CHUNK, CHUNK)
        sub_refs = lambda *refs: [r.at[rows] for r in refs]
        y_new = ybuf.at[slot, rows]
        z_c, x_c, b_c, c_c, dtraw_c = sub_refs(z_ref, x_ref, b_ref, c_ref, dtraw_ref)
        ssd_group = _ssd_chunk(z_c, x_c, b_c, c_c, dtraw_c, cw_ref, cb_ref, dtb_ref, alog_ref, dskip_ref, snw_ref,
                               y_new, xbuf, shifts, ssd_state, expand)
        q_c, k_c, v0_c, v1_c, g0_c, g1_c = sub_refs(q_ref, k_ref, v0_ref, v1_ref, g0_ref, g1_ref)
        ret_head = _ret_chunk(q_c, k_c, v0_c, v1_c, g0_c, g1_c, rnw_ref, y_new, ret_state)

        def between_quads(q):
            if q % CHUNKS_PER_STEP == 0:
                n = len(pieces)
                pieces.append(_dot(y_old, wout[:, n * piece_w:(n + 1) * piece_w]))
            ret_head(q)

        for g in range(SSD_GROUPS):
            ssd_group(g, between_quads)

    h = h_ref[...] + gate_ref[...] * jnp.concatenate(pieces, axis=1)
    if final:
        h = h * lax.rsqrt(jnp.mean(h * h, axis=-1, keepdims=True) + EPS) * fw_ref[...]
    o_ref[...] = h


def _mix_and_project(proj, dt_raw, h, mod, conv_w, conv_b, dt_bias, a_log, d_skip, ssd_norm_w, ret_norm_w, w_out,
                     final_norm_w, final):
    L = h.shape[0]
    rows = CHUNKS_PER_STEP * CHUNK
    n_steps = L // rows
    pad = lambda v: jnp.pad(v.reshape(1, SSD_HEADS), ((0, 0), (0, LANES - SSD_HEADS)))
    full = lambda shape: pl.BlockSpec(shape, lambda s: (0, 0))
    mixed = lambda width, col: pl.BlockSpec((rows, width), lambda s: (jnp.minimum(s, n_steps - 1), col))
    projected = lambda width, col: pl.BlockSpec((rows, width), lambda s: (jnp.maximum(s - 1, 0), col))
    return pl.pallas_call(
        functools.partial(_mixer_kernel, final=final),
        grid=(n_steps + 1,),
        in_specs=[mixed(SSD_D_INNER, COL_Z), mixed(SSD_D_INNER, COL_X), mixed(SSD_BC, COL_B), mixed(SSD_BC, COL_C),
                  mixed(LANES, 0),
                  full((SSD_CONV, SSD_CONV_DIM)), full((1, SSD_CONV_DIM)), full((1, LANES)), full((1, LANES)),
                  full((1, SSD_D_INNER)), full((1, SSD_D_INNER)),
                  mixed(RET_QK_WIDTH, COL_Q), mixed(RET_QK_WIDTH, COL_K),
                  mixed(RET_QK_WIDTH, COL_V), mixed(RET_QK_WIDTH, COL_V + 1),
                  mixed(RET_QK_WIDTH, COL_G), mixed(RET_QK_WIDTH, COL_G + 1),
                  full((1, RET_V_WIDTH)),
                  pl.BlockSpec(memory_space=pl.ANY),
                  projected(D_MODEL, 0),
                  pl.BlockSpec((1, D_MODEL), lambda s: (0, 2)),
                  full((1, D_MODEL))],
        out_specs=projected(D_MODEL, 0),
        out_shape=jax.ShapeDtypeStruct((L, D_MODEL), F32),
        scratch_shapes=[pltpu.VMEM((CONV_TAIL + CHUNK, SSD_CONV_DIM), BF16),
                        pltpu.VMEM(((SSD_CONV - 1) * CHUNK, CONV_TAIL + CHUNK), BF16),
                        pltpu.VMEM((SSD_STATE, SSD_D_INNER), F32),
                        pltpu.VMEM((LANES, SSD_D_INNER), BF16),
                        pltpu.VMEM((RET_HEADS, RET_QK_DIM, RET_V_DIM), F32),
                        pltpu.VMEM((2, rows, MIX_WIDTH), BF16),
                        pltpu.VMEM((MIX_WIDTH, D_MODEL), BF16),
                        pltpu.SemaphoreType.DMA((1,))],
        compiler_params=pltpu.CompilerParams(dimension_semantics=("arbitrary",),
                                             vmem_limit_bytes=60 * MIB),
        name="mixers_out_projection",
    )(proj, proj, proj, proj, dt_raw, conv_w, conv_b.reshape(1, SSD_CONV_DIM), pad(dt_bias), pad(a_log),
      jnp.repeat(d_skip, SSD_HEAD_DIM).reshape(1, SSD_D_INNER), ssd_norm_w.reshape(1, SSD_D_INNER),
      proj, proj, proj, proj, proj, proj, ret_norm_w.reshape(1, RET_V_WIDTH),
      w_out, h, mod, final_norm_w.reshape(1, D_MODEL))


def kernel(x, c, w_ada, b_ada, norm_w, w_in, conv_w, conv_b, dt_bias, a_log, d_skip, ssd_norm_w, ret_norm_w,
           w_out, final_norm_w):
    bsz, L, d_model = x.shape
    assert bsz == 1 and d_model == D_MODEL and L % TM_IN == 0 and w_in.shape[-1] == IN_WIDTH
    depth = w_in.shape[0]
    half = RET_QK_DIM // 2
    inv = ROPE_BASE ** (-jnp.arange(half, dtype=F32) / half)
    inv = jnp.concatenate([inv, inv]).reshape(1, RET_QK_DIM)
    h = x.reshape(L, D_MODEL)
    for layer in range(depth):
        mod = _modulation(c, w_ada[layer], b_ada[layer])
        proj, dt_raw, w_out_bf = _in_projection(h, norm_w[layer], mod, jnp.swapaxes(w_in[layer], 0, 1), inv,
                                                w_out[layer])
        h = _mix_and_project(proj, dt_raw, h, mod, conv_w[layer], conv_b[layer], dt_bias[layer], a_log[layer],
                             d_skip[layer], ssd_norm_w[layer], ret_norm_w[layer], w_out_bf, final_norm_w,
                             final=layer == depth - 1)
    return h.reshape(bsz, L, D_MODEL)
```

```python
import functools
import math

import numpy as np
import jax
import jax.numpy as jnp
from jax import lax
from jax.experimental import pallas as pl
from jax.experimental.pallas import tpu as pltpu

D_MODEL = 2048
SSD_D_INNER = D_MODEL
SSD_HEAD_DIM = 64
SSD_HEADS = SSD_D_INNER // SSD_HEAD_DIM
HEAD_DIM_LOG2 = SSD_HEAD_DIM.bit_length() - 1
SSD_GROUPS = 4
SSD_STATE = 128
SSD_CONV = 4
SSD_BC = SSD_GROUPS * SSD_STATE
SSD_CONV_DIM = SSD_D_INNER + 2 * SSD_BC
RET_HEADS = 8
RET_QK_DIM = 128
RET_V_DIM = 256
RET_QK_WIDTH = RET_HEADS * RET_QK_DIM
RET_V_WIDTH = RET_HEADS * RET_V_DIM
MIX_WIDTH = SSD_D_INNER + RET_V_WIDTH
CHUNK = 128
ROPE_BASE = 10000.0
EPS = 1e-6
LOG2_E = math.log2(math.e)

OFF_Z = SSD_D_INNER
OFF_XBC = OFF_Z + SSD_CONV_DIM
OFF_DT = OFF_XBC + SSD_HEADS
OFF_Q = OFF_DT + RET_QK_WIDTH
OFF_K = OFF_Q + RET_QK_WIDTH
OFF_V = OFF_K + RET_V_WIDTH
IN_WIDTH = OFF_V + RET_V_WIDTH

PROJ_WIDTH = IN_WIDTH - SSD_HEADS
COL_Z, COL_X = 0, 1
COL_B, COL_C = 8, 9
COL_Q, COL_K, COL_V, COL_G = 5, 6, 7, 9

LANES = 128
SUBLANES = 8
BF16_SUBLANES = 16
QUAD = 4
QUAD_W = QUAD * SSD_HEAD_DIM
HEADS_PER_GROUP = SSD_HEADS // SSD_GROUPS
GROUP_W = SSD_D_INNER // SSD_GROUPS
CONV_TAIL = BF16_SUBLANES
CHUNKS_PER_STEP = 2
N_PIECES = SSD_HEADS // QUAD

MOD_TILE = 512
MOD_BUFFERS = 3
TM_IN, TN_IN = 1024, 1024
TILE_CONV0, TILE_ROT0, TILE_V0, TILE_G0 = (c // TN_IN for c in (OFF_Z, OFF_XBC, OFF_XBC + 2 * RET_QK_WIDTH,
                                                                 OFF_XBC + 2 * RET_QK_WIDTH + RET_V_WIDTH))
N_COL_TILES = PROJ_WIDTH // TN_IN
assert OFF_Z % TN_IN == 0 and OFF_XBC % TN_IN == 0 and RET_QK_WIDTH == TN_IN and RET_V_WIDTH % TN_IN == 0
assert PROJ_WIDTH % TN_IN == 0 and TM_IN % CHUNK == 0 and N_COL_TILES - 1 >= TILE_G0
MIB = 1024 * 1024

F32 = jnp.float32
BF16 = jnp.bfloat16


def _silu(v):
    return v / (1.0 + jnp.exp(-v))


def _dot(a, b):
    return jnp.dot(a, b, preferred_element_type=F32)


def _dot_nt(a, b):
    return lax.dot_general(a, b, (((1,), (1,)), ((), ())), preferred_element_type=F32)


def _dot_tn(a, b):
    return lax.dot_general(a, b, (((0,), (0,)), ((), ())), preferred_element_type=F32)


def _split3(v):
    hi = v.astype(BF16)
    r1 = v - hi.astype(F32)
    mid = r1.astype(BF16)
    lo = (r1 - mid.astype(F32)).astype(BF16)
    return hi, mid, lo


def _log_gammas():
    return [float(np.log1p(-np.exp2(np.float32(-5.0 - hd)), dtype=np.float32)) for hd in range(RET_HEADS)]


def _mod_kernel(c_ref, w_hbm, b_ref, o_ref, wbuf, sem):
    n_tiles = o_ref.shape[1] // MOD_TILE

    def fetch(j):
        slot = j % MOD_BUFFERS
        return pltpu.make_async_copy(w_hbm.at[:, pl.ds(j * MOD_TILE, MOD_TILE)], wbuf.at[slot], sem.at[slot])

    for j in range(min(MOD_BUFFERS - 1, n_tiles)):
        fetch(j).start()
    cond = jnp.broadcast_to(_silu(c_ref[...]), (SUBLANES, D_MODEL))
    for j in range(n_tiles):
        if j + MOD_BUFFERS - 1 < n_tiles:
            fetch(j + MOD_BUFFERS - 1).start()
        fetch(j).wait()
        cols = slice(j * MOD_TILE, (j + 1) * MOD_TILE)
        o_ref[:, cols] = _dot(cond, wbuf[j % MOD_BUFFERS])[0:1, :] + b_ref[:, cols]


def _modulation(c, w_ada, b_ada):
    n = w_ada.shape[1]
    assert n % MOD_TILE == 0
    vmem = pl.BlockSpec(memory_space=pltpu.VMEM)
    return pl.pallas_call(
        _mod_kernel,
        in_specs=[vmem, pl.BlockSpec(memory_space=pl.ANY), vmem],
        out_specs=vmem,
        out_shape=jax.ShapeDtypeStruct((1, n), F32),
        scratch_shapes=[pltpu.VMEM((MOD_BUFFERS, D_MODEL, MOD_TILE), F32),
                        pltpu.SemaphoreType.DMA((MOD_BUFFERS,))],
        compiler_params=pltpu.CompilerParams(vmem_limit_bytes=24 * MIB),
        name="adaln_modulation",
    )(c, w_ada, b_ada.reshape(1, n))


def _inproj_kernel(x_ref, nw_ref, shift_ref, scale_ref, wa_ref, wb_ref, wd_ref, inv_ref, wout_ref,
                   o_ref, dt_ref, wout_bf_ref, u_ref, prev, cos_in, sin_in, cos_t, sin_t, decay, *, n_tiles):
    t = pl.program_id(0)
    tile = jnp.minimum(t, n_tiles - 1)
    i = tile // N_COL_TILES
    j = tile - i * N_COL_TILES
    je = lax.rem(jnp.maximum(t - 1, 0), N_COL_TILES)
    half = RET_QK_DIM // 2

    @pl.when(t == 0)
    def _():
        prev[...] = jnp.zeros_like(prev)
        rows = lambda n: lax.broadcasted_iota(jnp.int32, (n, RET_QK_DIM), 0).astype(F32)
        ang_r = rows(CHUNK) * inv_ref[...]
        ang_blk = rows(TM_IN // CHUNK) * float(CHUNK) * inv_ref[...]
        cos_r, sin_r, cos_blk, sin_blk = jnp.cos(ang_r), jnp.sin(ang_r), jnp.cos(ang_blk), jnp.sin(ang_blk)
        for blk in range(TM_IN // CHUNK):
            chunk = slice(blk * CHUNK, (blk + 1) * CHUNK)
            cb, sb = cos_blk[blk:blk + 1, :], sin_blk[blk:blk + 1, :]
            cos_in[chunk, :] = cos_r * cb - sin_r * sb
            sin_in[chunk, :] = sin_r * cb + cos_r * sb
        pos = lax.broadcasted_iota(jnp.int32, (CHUNK, RET_QK_DIM), 0).astype(F32) + 1.0
        for hd, lg in enumerate(_log_gammas()):
            decay[0, hd] = jnp.exp(pos * lg)
            decay[1, hd] = jnp.exp(pos * -lg) * RET_QK_DIM ** -0.5

    @pl.when(jnp.logical_and(j == 0, t < n_tiles))
    def _():
        x = x_ref[...]
        gain = nw_ref[...] * (1.0 + scale_ref[...])
        u = (x * lax.rsqrt(jnp.mean(x * x, axis=-1, keepdims=True) + EPS) * gain + shift_ref[...]).astype(BF16)
        u_ref[...] = u
        is_dt = lax.broadcasted_iota(jnp.int32, (LANES, D_MODEL), 0) < SSD_HEADS
        dt_ref[...] = _dot_nt(u, jnp.where(is_dt, wd_ref[...], 0.0).astype(BF16))

        base = jnp.broadcast_to((i * TM_IN).astype(F32) * inv_ref[...], (SUBLANES, RET_QK_DIM))
        cos_b = jnp.cos(base)[0:1, :]
        sin_b = jnp.sin(base)[0:1, :]
        cos_t[...] = cos_in[...] * cos_b - sin_in[...] * sin_b
        sin_pos = sin_in[...] * cos_b + cos_in[...] * sin_b
        first_half = lax.broadcasted_iota(jnp.int32, (TM_IN, RET_QK_DIM), 1) < half
        sin_t[...] = jnp.where(first_half, -sin_pos, sin_pos)

    def project():
        after_dt = j >= TILE_ROT0
        skip = pl.multiple_of(jnp.where(after_dt, SSD_HEADS, 0), SSD_HEADS)
        head = wa_ref[pl.ds(skip, TN_IN - SSD_HEADS), :]
        tail = jnp.where(after_dt, wb_ref[...], wa_ref[TN_IN - SSD_HEADS:, :])
        w = jnp.concatenate([head, tail], axis=0).astype(BF16)
        wout_bf_ref[...] = wout_ref[...].astype(BF16)
        return _dot_nt(u_ref[...], w)

    @pl.when(jnp.logical_and(jnp.logical_or(je < TILE_CONV0, je >= TILE_G0), t < n_tiles))
    def _():
        acc = project()
        o_ref[...] = _silu(prev[...]).astype(BF16)
        prev[...] = acc

    @pl.when(t == n_tiles)
    def _():
        o_ref[...] = _silu(prev[...]).astype(BF16)

    @pl.when(jnp.logical_or(jnp.logical_and(je >= TILE_CONV0, je < TILE_ROT0),
                            jnp.logical_and(je >= TILE_V0, je < TILE_G0)))
    def _():
        acc = project()
        o_ref[...] = prev[...].astype(BF16)
        prev[...] = acc

    @pl.when(jnp.logical_and(je >= TILE_ROT0, je < TILE_V0))
    def _():
        acc = project()
        side = je - TILE_ROT0
        for hd in range(RET_HEADS):
            cols = slice(hd * RET_QK_DIM, (hd + 1) * RET_QK_DIM)
            scale = decay[side, hd]
            for blk in range(TM_IN // CHUNK):
                rows = slice(blk * CHUNK, (blk + 1) * CHUNK)
                a = prev[rows, cols]
                rot = a * cos_t[rows, :] + pltpu.roll(a, half, 1) * sin_t[rows, :]
                o_ref[rows, cols] = (rot * scale).astype(BF16)
        prev[...] = acc


def _in_projection(h, norm_w, mod, w_in_t, inv, w_out):
    L = h.shape[0]
    n_tiles = (L // TM_IN) * N_COL_TILES
    n_cast = min(1 << (n_tiles.bit_length() - 1), MIX_WIDTH // BF16_SUBLANES)
    cast_block = pl.BlockSpec((MIX_WIDTH // n_cast, D_MODEL), lambda t: (jnp.minimum(t, n_cast - 1), 0))
    const = lambda t: (0, 0)
    mul_row = lambda t: jnp.minimum(t, n_tiles - 1) // N_COL_TILES
    mul_col = lambda t: lax.rem(jnp.minimum(t, n_tiles - 1), N_COL_TILES)
    epi = lambda t: jnp.maximum(t - 1, 0)
    table = pltpu.VMEM((TM_IN, RET_QK_DIM), F32)
    return pl.pallas_call(
        functools.partial(_inproj_kernel, n_tiles=n_tiles),
        grid=(n_tiles + 1,),
        in_specs=[pl.BlockSpec((TM_IN, D_MODEL), lambda t: (mul_row(t), 0)),
                  pl.BlockSpec((1, D_MODEL), const),
                  pl.BlockSpec((1, D_MODEL), lambda t: (0, 0)),
                  pl.BlockSpec((1, D_MODEL), lambda t: (0, 1)),
                  pl.BlockSpec((TN_IN, D_MODEL), lambda t: (mul_col(t), 0)),
                  pl.BlockSpec((SSD_HEADS, D_MODEL), lambda t: ((mul_col(t) + 1) * (TN_IN // SSD_HEADS), 0)),
                  pl.BlockSpec((LANES, D_MODEL), lambda t: (OFF_XBC // LANES, 0)),
                  pl.BlockSpec((1, RET_QK_DIM), const),
                  cast_block],
        out_specs=[pl.BlockSpec((TM_IN, TN_IN), lambda t: (epi(t) // N_COL_TILES, lax.rem(epi(t), N_COL_TILES))),
                   pl.BlockSpec((TM_IN, LANES), lambda t: (mul_row(t), 0)),
                   cast_block],
        out_shape=[jax.ShapeDtypeStruct((L, PROJ_WIDTH), BF16),
                   jax.ShapeDtypeStruct((L, LANES), F32),
                   jax.ShapeDtypeStruct((MIX_WIDTH, D_MODEL), BF16)],
        scratch_shapes=[pltpu.VMEM((TM_IN, D_MODEL), BF16),
                        pltpu.VMEM((TM_IN, TN_IN), F32),
                        table, table, table, table,
                        pltpu.VMEM((2, RET_HEADS, CHUNK, RET_QK_DIM), F32)],
        compiler_params=pltpu.CompilerParams(dimension_semantics=("arbitrary",),
                                             vmem_limit_bytes=60 * MIB),
        name="adaln_in_projection",
    )(h, norm_w.reshape(1, D_MODEL), mod, mod, w_in_t, w_in_t, w_in_t, inv, w_out)


def _ssd_init(xbuf, shifts, state, expand):
    xbuf[0:CONV_TAIL, :] = jnp.zeros((CONV_TAIL, SSD_CONV_DIM), BF16)
    row = lax.broadcasted_iota(jnp.int32, shifts.shape, 0)
    col = lax.broadcasted_iota(jnp.int32, shifts.shape, 1)
    delay = jnp.right_shift(row, CHUNK.bit_length() - 1) + 1
    t = jnp.bitwise_and(row, CHUNK - 1)
    shifts[...] = jnp.where(col == CONV_TAIL + t - delay, 1.0, 0.0).astype(BF16)
    state[...] = jnp.zeros_like(state)
    head_of_lane = jnp.right_shift(lax.broadcasted_iota(jnp.int32, (LANES, SSD_D_INNER), 1), HEAD_DIM_LOG2)
    row = lax.broadcasted_iota(jnp.int32, (LANES, SSD_D_INNER), 0)
    expand[...] = jnp.where(head_of_lane == row, 1.0, 0.0).astype(BF16)


def _ssd_chunk(z_ref, x_ref, b_ref, c_ref, dtraw_ref, cw_ref, cb_ref, dtb_ref, alog_ref, dskip_ref, nw_ref,
               y_ref, xbuf, shifts, state, expand):
    int_iota = lambda shape, dim: lax.broadcasted_iota(jnp.int32, shape, dim)

    xbuf[CONV_TAIL:, 0:SSD_D_INNER] = x_ref[...]
    xbuf[CONV_TAIL:, SSD_D_INNER:SSD_D_INNER + SSD_BC] = b_ref[...]
    xbuf[CONV_TAIL:, SSD_D_INNER + SSD_BC:] = c_ref[...]
    delayed = _dot(shifts[...], xbuf[...])
    acc = cb_ref[...] + cw_ref[SSD_CONV - 1:SSD_CONV, :] * xbuf[CONV_TAIL:, :].astype(F32)
    for k in range(SSD_CONV - 1):
        tap = SSD_CONV - 2 - k
        acc = acc + cw_ref[tap:tap + 1, :] * delayed[k * CHUNK:(k + 1) * CHUNK, :]
    xbuf[0:CONV_TAIL, :] = xbuf[CHUNK:CHUNK + CONV_TAIL, :]
    xbc = _silu(acc)
    xs = xbc[:, 0:SSD_D_INNER]
    xs_b = xs.astype(BF16)
    bmat = xbc[:, SSD_D_INNER:SSD_D_INNER + SSD_BC]
    cmat = xbc[:, SSD_D_INNER + SSD_BC:]

    pre = dtraw_ref[...] + dtb_ref[...]
    dt = jnp.maximum(pre, 0.0) + jnp.log1p(jnp.exp(-jnp.abs(pre)))
    d_a = dt * (-LOG2_E * jnp.exp(alog_ref[...]))
    causal = int_iota((CHUNK, CHUNK), 0) >= int_iota((CHUNK, CHUNK), 1)
    tril = jnp.where(causal, 1.0, 0.0).astype(BF16)
    acs = sum(_dot(tril, part) for part in _split3(d_a))
    acs_t = acs.T
    src_t = acs_t - jnp.log2(dt).T
    w_t = jnp.exp2(acs_t[:, CHUNK - 1:CHUNK] - src_t)
    end = jnp.broadcast_to(jnp.exp2(acs[CHUNK - 1:CHUNK, :]), (SUBLANES, LANES))
    end_row = sum(_dot(part, expand[...]) for part in _split3(end))[0:1, :]

    lane_blk = jnp.right_shift(int_iota((CHUNK, QUAD_W), 1), HEAD_DIM_LOG2)

    def block_diag(v):
        return jnp.concatenate([jnp.where(lane_blk == j, v, jnp.zeros_like(v)) for j in range(QUAD)], axis=0)

    def group(g, between_quads=lambda q: None):
        gs = slice(g * SSD_STATE, (g + 1) * SSD_STATE)
        b_g = bmat[:, gs]
        c_g = cmat[:, gs]
        cb_g = _dot_nt(c_g.astype(BF16), b_g.astype(BF16))
        b_gt = b_g.T
        y_parts = []
        for qi in range(HEADS_PER_GROUP // QUAD):
            q = g * (HEADS_PER_GROUP // QUAD) + qi
            between_quads(q)
            qs = slice(q * QUAD_W, (q + 1) * QUAD_W)
            lhs_diag, lhs_state, lhs_off = [], [], []
            for j in range(QUAD):
                hd = q * QUAD + j
                col = jnp.broadcast_to(acs[:, hd:hd + 1], (CHUNK, CHUNK))
                decay_dt = jnp.exp2(jnp.where(causal, col - src_t[hd:hd + 1, :], -jnp.inf))
                lhs_diag.append((cb_g * decay_dt).astype(BF16))
                lhs_state.append((b_gt * w_t[hd:hd + 1, :]).astype(BF16))
                lhs_off.append((c_g * jnp.exp2(col)).astype(BF16))
            x_blk = block_diag(xs_b[:, qs])
            s_prev = state[:, qs]
            s_blk = block_diag(s_prev.astype(BF16))
            y_q = _dot(jnp.concatenate(lhs_diag, axis=1), x_blk)
            y_q = y_q + _dot(jnp.concatenate(lhs_off, axis=1), s_blk)
            state[:, qs] = s_prev * end_row[:, qs] + _dot(jnp.concatenate(lhs_state, axis=1), x_blk)
            y_parts.append(y_q + xs[:, qs] * dskip_ref[:, qs])
        ws = slice(g * GROUP_W, (g + 1) * GROUP_W)
        u = jnp.concatenate(y_parts, axis=1) * z_ref[:, ws].astype(F32)
        u = u * lax.rsqrt(jnp.mean(u * u, axis=-1, keepdims=True) + EPS)
        y_ref[:, ws] = (u * nw_ref[:, ws]).astype(BF16)

    return group


def _ret_chunk(q_ref, k_ref, v0_ref, v1_ref, g0_ref, g1_ref, nw_ref, y_ref, state):
    heads_per_block = RET_QK_WIDTH // RET_V_DIM
    log_gammas = _log_gammas()
    causal = (lax.broadcasted_iota(jnp.int32, (CHUNK, CHUNK), 0)
              >= lax.broadcasted_iota(jnp.int32, (CHUNK, CHUNK), 1))

    def head(hd):
        ks = slice(hd * RET_QK_DIM, (hd + 1) * RET_QK_DIM)
        vs = slice(hd * RET_V_DIM, (hd + 1) * RET_V_DIM)
        v_ref, g_ref = (v0_ref, g0_ref) if hd < heads_per_block else (v1_ref, g1_ref)
        bs = slice((hd % heads_per_block) * RET_V_DIM, (hd % heads_per_block + 1) * RET_V_DIM)
        q_h = q_ref[:, ks]
        k_h = k_ref[:, ks]
        v_h = v_ref[:, bs]
        scores = jnp.where(causal, _dot_nt(q_h, k_h), 0.0)
        s_prev = state[hd]
        o = _dot(scores.astype(BF16), v_h) + _dot(q_h, s_prev.astype(BF16))
        state[hd] = (s_prev + _dot_tn(k_h, v_h)) * float(np.exp(np.float32(CHUNK * log_gammas[hd])))
        mu = jnp.mean(o, axis=-1, keepdims=True)
        d = o - mu
        y = d * lax.rsqrt(jnp.mean(d * d, axis=-1, keepdims=True) + EPS) * nw_ref[:, vs]
        y = (y * g_ref[:, bs].astype(F32)).astype(BF16)
        y_ref[:, SSD_D_INNER + hd * RET_V_DIM:SSD_D_INNER + (hd + 1) * RET_V_DIM] = y

    return head


def _mixer_kernel(z_ref, x_ref, b_ref, c_ref, dtraw_ref, cw_ref, cb_ref, dtb_ref, alog_ref, dskip_ref, snw_ref,
                  q_ref, k_ref, v0_ref, v1_ref, g0_ref, g1_ref, rnw_ref,
                  wout_hbm, h_ref, gate_ref, fw_ref, o_ref,
                  xbuf, shifts, ssd_state, expand, ret_state, ybuf, wout, wsem, *, n_steps, final):
    step = pl.program_id(0)

    @pl.when(step == 0)
    def _():
        wout_copy = pltpu.make_async_copy(wout_hbm, wout, wsem.at[0])
        wout_copy.start()
        _ssd_init(xbuf, shifts, ssd_state, expand)
        ret_state[...] = jnp.zeros_like(ret_state)
        ybuf[...] = jnp.zeros_like(ybuf)
        wout_copy.wait()

    slot = lax.rem(step, 2)
    piece_w = D_MODEL // N_PIECES

    def project_piece(n):
        return _dot(ybuf[1 - slot], wout[:, n * piece_w:(n + 1) * piece_w])

    def finish(pieces):
        h = h_ref[...] + gate_ref[...] * jnp.concatenate(pieces, axis=1)
        if final:
            h = h * lax.rsqrt(jnp.mean(h * h, axis=-1, keepdims=True) + EPS) * fw_ref[...]
        o_ref[...] = h

    @pl.when(step < n_steps)
    def _():
        pieces = []
        for sub in range(CHUNKS_PER_STEP):
            rows = pl.ds(sub * CHUNK, CHUNK)
            sub_refs = lambda *refs: [r.at[rows] for r in refs]
            y_new = ybuf.at[slot, rows]
            z_c, x_c, b_c, c_c, dtraw_c = sub_refs(z_ref, x_ref, b_ref, c_ref, dtraw_ref)
            ssd_group = _ssd_chunk(z_c, x_c, b_c, c_c, dtraw_c, cw_ref, cb_ref, dtb_ref, alog_ref, dskip_ref,
                                   snw_ref, y_new, xbuf, shifts, ssd_state, expand)
            q_c, k_c, v0_c, v1_c, g0_c, g1_c = sub_refs(q_ref, k_ref, v0_ref, v1_ref, g0_ref, g1_ref)
            ret_head = _ret_chunk(q_c, k_c, v0_c, v1_c, g0_c, g1_c, rnw_ref, y_new, ret_state)

            def between_quads(q):
                if q % CHUNKS_PER_STEP == 0:
                    pieces.append(project_piece(len(pieces)))
                ret_head(q)

            for g in range(SSD_GROUPS):
                ssd_group(g, between_quads)
        assert len(pieces) == N_PIECES
        finish(pieces)

    @pl.when(step == n_steps)
    def _():
        finish([project_piece(n) for n in range(N_PIECES)])


def _mix_and_project(proj, dt_raw, h, mod, conv_w, conv_b, dt_bias, a_log, d_skip, ssd_norm_w, ret_norm_w, w_out,
                     final_norm_w, final):
    L = h.shape[0]
    rows = CHUNKS_PER_STEP * CHUNK
    n_steps = L // rows
    pad = lambda v: jnp.pad(v.reshape(1, SSD_HEADS), ((0, 0), (0, LANES - SSD_HEADS)))
    full = lambda shape: pl.BlockSpec(shape, lambda s: (0, 0))
    mixed = lambda width, col: pl.BlockSpec((rows, width), lambda s: (jnp.minimum(s, n_steps - 1), col))
    projected = lambda width, col: pl.BlockSpec((rows, width), lambda s: (jnp.maximum(s - 1, 0), col))
    return pl.pallas_call(
        functools.partial(_mixer_kernel, n_steps=n_steps, final=final),
        grid=(n_steps + 1,),
        in_specs=[mixed(SSD_D_INNER, COL_Z), mixed(SSD_D_INNER, COL_X), mixed(SSD_BC, COL_B), mixed(SSD_BC, COL_C),
                  mixed(LANES, 0),
                  full((SSD_CONV, SSD_CONV_DIM)), full((1, SSD_CONV_DIM)), full((1, LANES)), full((1, LANES)),
                  full((1, SSD_D_INNER)), full((1, SSD_D_INNER)),
                  mixed(RET_QK_WIDTH, COL_Q), mixed(RET_QK_WIDTH, COL_K),
                  mixed(RET_QK_WIDTH, COL_V), mixed(RET_QK_WIDTH, COL_V + 1),
                  mixed(RET_QK_WIDTH, COL_G), mixed(RET_QK_WIDTH, COL_G + 1),
                  full((1, RET_V_WIDTH)),
                  pl.BlockSpec(memory_space=pl.ANY),
                  projected(D_MODEL, 0),
                  pl.BlockSpec((1, D_MODEL), lambda s: (0, 2)),
                  full((1, D_MODEL))],
        out_specs=projected(D_MODEL, 0),
        out_shape=jax.ShapeDtypeStruct((L, D_MODEL), F32),
        scratch_shapes=[pltpu.VMEM((CONV_TAIL + CHUNK, SSD_CONV_DIM), BF16),
                        pltpu.VMEM(((SSD_CONV - 1) * CHUNK, CONV_TAIL + CHUNK), BF16),
                        pltpu.VMEM((SSD_STATE, SSD_D_INNER), F32),
                        pltpu.VMEM((LANES, SSD_D_INNER), BF16),
                        pltpu.VMEM((RET_HEADS, RET_QK_DIM, RET_V_DIM), F32),
                        pltpu.VMEM((2, rows, MIX_WIDTH), BF16),
                        pltpu.VMEM((MIX_WIDTH, D_MODEL), BF16),
                        pltpu.SemaphoreType.DMA((1,))],
        compiler_params=pltpu.CompilerParams(dimension_semantics=("arbitrary",),
                                             vmem_limit_bytes=60 * MIB),
        name="mixers_out_projection",
    )(proj, proj, proj, proj, dt_raw, conv_w, conv_b.reshape(1, SSD_CONV_DIM), pad(dt_bias), pad(a_log),
      jnp.repeat(d_skip, SSD_HEAD_DIM).reshape(1, SSD_D_INNER), ssd_norm_w.reshape(1, SSD_D_INNER),
      proj, proj, proj, proj, proj, proj, ret_norm_w.reshape(1, RET_V_WIDTH),
      w_out, h, mod, final_norm_w.reshape(1, D_MODEL))


def kernel(x, c, w_ada, b_ada, norm_w, w_in, conv_w, conv_b, dt_bias, a_log, d_skip, ssd_norm_w, ret_norm_w,
           w_out, final_norm_w):
    bsz, L, d_model = x.shape
    assert bsz == 1 and d_model == D_MODEL and L % TM_IN == 0 and w_in.shape[-1] == IN_WIDTH
    depth = w_in.shape[0]
    half = RET_QK_DIM // 2
    inv = ROPE_BASE ** (-jnp.arange(half, dtype=F32) / half)
    inv = jnp.concatenate([inv, inv]).reshape(1, RET_QK_DIM)
    h = x.reshape(L, D_MODEL)
    for layer in range(depth):
        mod = _modulation(c, w_ada[layer], b_ada[layer])
        proj, dt_raw, w_out_bf = _in_projection(h, norm_w[layer], mod, jnp.swapaxes(w_in[layer], 0, 1), inv,
                                                w_out[layer])
        h = _mix_and_project(proj, dt_raw, h, mod, conv_w[layer], conv_b[layer], dt_bias[layer], a_log[layer],
                             d_skip[layer], ssd_norm_w[layer], ret_norm_w[layer], w_out_bf, final_norm_w,
                             final=layer == depth - 1)
    return h.reshape(bsz, L, D_MODEL)
```

```python
import functools
import math

import numpy as np
import jax
import jax.numpy as jnp
from jax import lax
from jax.experimental import pallas as pl
from jax.experimental.pallas import tpu as pltpu

D_MODEL = 2048
SSD_D_INNER = D_MODEL
SSD_HEAD_DIM = 64
SSD_HEADS = SSD_D_INNER // SSD_HEAD_DIM
HEAD_DIM_LOG2 = SSD_HEAD_DIM.bit_length() - 1
SSD_GROUPS = 4
SSD_STATE = 128
SSD_CONV = 4
SSD_BC = SSD_GROUPS * SSD_STATE
SSD_CONV_DIM = SSD_D_INNER + 2 * SSD_BC
RET_HEADS = 8
RET_QK_DIM = 128
RET_V_DIM = 256
RET_QK_WIDTH = RET_HEADS * RET_QK_DIM
RET_V_WIDTH = RET_HEADS * RET_V_DIM
MIX_WIDTH = SSD_D_INNER + RET_V_WIDTH
CHUNK = 128
ROPE_BASE = 10000.0
EPS = 1e-6
LOG2_E = math.log2(math.e)

OFF_Z = SSD_D_INNER
OFF_XBC = OFF_Z + SSD_CONV_DIM
OFF_DT = OFF_XBC + SSD_HEADS
OFF_Q = OFF_DT + RET_QK_WIDTH
OFF_K = OFF_Q + RET_QK_WIDTH
OFF_V = OFF_K + RET_V_WIDTH
IN_WIDTH = OFF_V + RET_V_WIDTH

PROJ_WIDTH = IN_WIDTH - SSD_HEADS
COL_Z, COL_X = 0, 1
COL_B, COL_C = 8, 9
COL_Q, COL_K, COL_V, COL_G = 5, 6, 7, 9

LANES = 128
SUBLANES = 8
BF16_SUBLANES = 16
QUAD = 4
QUAD_W = QUAD * SSD_HEAD_DIM
HEADS_PER_GROUP = SSD_HEADS // SSD_GROUPS
GROUP_W = SSD_D_INNER // SSD_GROUPS
CONV_TAIL = BF16_SUBLANES
CHUNKS_PER_STEP = 2
N_PIECES = SSD_HEADS // QUAD

MOD_TILE = 512
MOD_BUFFERS = 3
TM_IN, TN_IN = 1024, 1024
TILE_CONV0, TILE_ROT0, TILE_V0, TILE_G0 = (c // TN_IN for c in (OFF_Z, OFF_XBC, OFF_XBC + 2 * RET_QK_WIDTH,
                                                                 OFF_XBC + 2 * RET_QK_WIDTH + RET_V_WIDTH))
N_COL_TILES = PROJ_WIDTH // TN_IN
assert OFF_Z % TN_IN == 0 and OFF_XBC % TN_IN == 0 and RET_QK_WIDTH == TN_IN and RET_V_WIDTH % TN_IN == 0
assert PROJ_WIDTH % TN_IN == 0 and TM_IN % CHUNK == 0 and N_COL_TILES - 1 >= TILE_G0
MIB = 1024 * 1024

F32 = jnp.float32
BF16 = jnp.bfloat16


def _silu(v):
    return v / (1.0 + jnp.exp(-v))


def _dot(a, b):
    return jnp.dot(a, b, preferred_element_type=F32)


def _dot_nt(a, b):
    return lax.dot_general(a, b, (((1,), (1,)), ((), ())), preferred_element_type=F32)


def _dot_tn(a, b):
    return lax.dot_general(a, b, (((0,), (0,)), ((), ())), preferred_element_type=F32)


def _split3(v):
    hi = v.astype(BF16)
    r1 = v - hi.astype(F32)
    mid = r1.astype(BF16)
    lo = (r1 - mid.astype(F32)).astype(BF16)
    return hi, mid, lo


def _log_gammas():
    return [float(np.log1p(-np.exp2(np.float32(-5.0 - hd)), dtype=np.float32)) for hd in range(RET_HEADS)]


def _mod_kernel(c_ref, w_hbm, b_ref, o_ref, wbuf, sem):
    n_tiles = o_ref.shape[1] // MOD_TILE

    def fetch(j):
        slot = j % MOD_BUFFERS
        return pltpu.make_async_copy(w_hbm.at[:, pl.ds(j * MOD_TILE, MOD_TILE)], wbuf.at[slot], sem.at[slot])

    for j in range(min(MOD_BUFFERS - 1, n_tiles)):
        fetch(j).start()
    cond = jnp.broadcast_to(_silu(c_ref[...]), (SUBLANES, D_MODEL))
    for j in range(n_tiles):
        if j + MOD_BUFFERS - 1 < n_tiles:
            fetch(j + MOD_BUFFERS - 1).start()
        fetch(j).wait()
        cols = slice(j * MOD_TILE, (j + 1) * MOD_TILE)
        o_ref[:, cols] = _dot(cond, wbuf[j % MOD_BUFFERS])[0:1, :] + b_ref[:, cols]


def _modulation(c, w_ada, b_ada):
    n = w_ada.shape[1]
    assert n % MOD_TILE == 0
    vmem = pl.BlockSpec(memory_space=pltpu.VMEM)
    return pl.pallas_call(
        _mod_kernel,
        in_specs=[vmem, pl.BlockSpec(memory_space=pl.ANY), vmem],
        out_specs=vmem,
        out_shape=jax.ShapeDtypeStruct((1, n), F32),
        scratch_shapes=[pltpu.VMEM((MOD_BUFFERS, D_MODEL, MOD_TILE), F32),
                        pltpu.SemaphoreType.DMA((MOD_BUFFERS,))],
        compiler_params=pltpu.CompilerParams(vmem_limit_bytes=24 * MIB),
        name="adaln_modulation",
    )(c, w_ada, b_ada.reshape(1, n))


def _inproj_kernel(x_ref, nw_ref, shift_ref, scale_ref, wa_ref, wb_ref, wd_ref, inv_ref, wout_ref,
                   o_ref, dt_ref, wout_bf_ref, u_ref, prev, cos_in, sin_in, cos_t, sin_t, decay, *, n_tiles):
    t = pl.program_id(0)
    tile = jnp.minimum(t, n_tiles - 1)
    i = tile // N_COL_TILES
    j = tile - i * N_COL_TILES
    je = lax.rem(jnp.maximum(t - 1, 0), N_COL_TILES)
    half = RET_QK_DIM // 2

    @pl.when(t == 0)
    def _():
        prev[...] = jnp.zeros_like(prev)
        rows = lambda n: lax.broadcasted_iota(jnp.int32, (n, RET_QK_DIM), 0).astype(F32)
        ang_r = rows(CHUNK) * inv_ref[...]
        ang_blk = rows(TM_IN // CHUNK) * float(CHUNK) * inv_ref[...]
        cos_r, sin_r, cos_blk, sin_blk = jnp.cos(ang_r), jnp.sin(ang_r), jnp.cos(ang_blk), jnp.sin(ang_blk)
        for blk in range(TM_IN // CHUNK):
            chunk = slice(blk * CHUNK, (blk + 1) * CHUNK)
            cb, sb = cos_blk[blk:blk + 1, :], sin_blk[blk:blk + 1, :]
            cos_in[chunk, :] = cos_r * cb - sin_r * sb
            sin_in[chunk, :] = sin_r * cb + cos_r * sb
        pos = lax.broadcasted_iota(jnp.int32, (CHUNK, RET_QK_DIM), 0).astype(F32) + 1.0
        for hd, lg in enumerate(_log_gammas()):
            decay[0, hd] = jnp.exp(pos * lg)
            decay[1, hd] = jnp.exp(pos * -lg) * RET_QK_DIM ** -0.5

    @pl.when(jnp.logical_and(j == 0, t < n_tiles))
    def _():
        x = x_ref[...]
        gain = nw_ref[...] * (1.0 + scale_ref[...])
        u = (x * lax.rsqrt(jnp.mean(x * x, axis=-1, keepdims=True) + EPS) * gain + shift_ref[...]).astype(BF16)
        u_ref[...] = u
        is_dt = lax.broadcasted_iota(jnp.int32, (LANES, D_MODEL), 0) < SSD_HEADS
        dt_ref[...] = _dot_nt(u, jnp.where(is_dt, wd_ref[...], 0.0).astype(BF16))

        base = jnp.broadcast_to((i * TM_IN).astype(F32) * inv_ref[...], (SUBLANES, RET_QK_DIM))
        cos_b = jnp.cos(base)[0:1, :]
        sin_b = jnp.sin(base)[0:1, :]
        cos_t[...] = cos_in[...] * cos_b - sin_in[...] * sin_b
        sin_pos = sin_in[...] * cos_b + cos_in[...] * sin_b
        first_half = lax.broadcasted_iota(jnp.int32, (TM_IN, RET_QK_DIM), 1) < half
        sin_t[...] = jnp.where(first_half, -sin_pos, sin_pos)

    def project():
        after_dt = j >= TILE_ROT0
        skip = pl.multiple_of(jnp.where(after_dt, SSD_HEADS, 0), SSD_HEADS)
        head = wa_ref[pl.ds(skip, TN_IN - SSD_HEADS), :]
        tail = jnp.where(after_dt, wb_ref[...], wa_ref[TN_IN - SSD_HEADS:, :])
        w = jnp.concatenate([head, tail], axis=0).astype(BF16)
        wout_bf_ref[...] = wout_ref[...].astype(BF16)
        return _dot_nt(u_ref[...], w)

    @pl.when(jnp.logical_and(jnp.logical_or(je < TILE_CONV0, je >= TILE_G0), t < n_tiles))
    def _():
        acc = project()
        o_ref[...] = _silu(prev[...]).astype(BF16)
        prev[...] = acc

    @pl.when(t == n_tiles)
    def _():
        o_ref[...] = _silu(prev[...]).astype(BF16)

    @pl.when(jnp.logical_or(jnp.logical_and(je >= TILE_CONV0, je < TILE_ROT0),
                            jnp.logical_and(je >= TILE_V0, je < TILE_G0)))
    def _():
        acc = project()
        o_ref[...] = prev[...].astype(BF16)
        prev[...] = acc

    @pl.when(jnp.logical_and(je >= TILE_ROT0, je < TILE_V0))
    def _():
        acc = project()
        side = je - TILE_ROT0
        for hd in range(RET_HEADS):
            cols = slice(hd * RET_QK_DIM, (hd + 1) * RET_QK_DIM)
            scale = decay[side, hd]
            for blk in range(TM_IN // CHUNK):
                rows = slice(blk * CHUNK, (blk + 1) * CHUNK)
                a = prev[rows, cols]
                rot = a * cos_t[rows, :] + pltpu.roll(a, half, 1) * sin_t[rows, :]
                o_ref[rows, cols] = (rot * scale).astype(BF16)
        prev[...] = acc


def _in_projection(h, norm_w, mod, w_in_t, inv, w_out):
    L = h.shape[0]
    n_tiles = (L // TM_IN) * N_COL_TILES
    n_cast = min(1 << (n_tiles.bit_length() - 1), MIX_WIDTH // BF16_SUBLANES)
    cast_block = pl.BlockSpec((MIX_WIDTH // n_cast, D_MODEL), lambda t: (jnp.minimum(t, n_cast - 1), 0))
    const = lambda t: (0, 0)
    mul_row = lambda t: jnp.minimum(t, n_tiles - 1) // N_COL_TILES
    mul_col = lambda t: lax.rem(jnp.minimum(t, n_tiles - 1), N_COL_TILES)
    epi = lambda t: jnp.maximum(t - 1, 0)
    table = pltpu.VMEM((TM_IN, RET_QK_DIM), F32)
    return pl.pallas_call(
        functools.partial(_inproj_kernel, n_tiles=n_tiles),
        grid=(n_tiles + 1,),
        in_specs=[pl.BlockSpec((TM_IN, D_MODEL), lambda t: (mul_row(t), 0)),
                  pl.BlockSpec((1, D_MODEL), const),
                  pl.BlockSpec((1, D_MODEL), lambda t: (0, 0)),
                  pl.BlockSpec((1, D_MODEL), lambda t: (0, 1)),
                  pl.BlockSpec((TN_IN, D_MODEL), lambda t: (mul_col(t), 0)),
                  pl.BlockSpec((SSD_HEADS, D_MODEL), lambda t: ((mul_col(t) + 1) * (TN_IN // SSD_HEADS), 0)),
                  pl.BlockSpec((LANES, D_MODEL), lambda t: (OFF_XBC // LANES, 0)),
                  pl.BlockSpec((1, RET_QK_DIM), const),
                  cast_block],
        out_specs=[pl.BlockSpec((TM_IN, TN_IN), lambda t: (epi(t) // N_COL_TILES, lax.rem(epi(t), N_COL_TILES))),
                   pl.BlockSpec((TM_IN, LANES), lambda t: (mul_row(t), 0)),
                   cast_block],
        out_shape=[jax.ShapeDtypeStruct((L, PROJ_WIDTH), BF16),
                   jax.ShapeDtypeStruct((L, LANES), F32),
                   jax.ShapeDtypeStruct((MIX_WIDTH, D_MODEL), BF16)],
        scratch_shapes=[pltpu.VMEM((TM_IN, D_MODEL), BF16),
                        pltpu.VMEM((TM_IN, TN_IN), F32),
                        table, table, table, table,
                        pltpu.VMEM((2, RET_HEADS, CHUNK, RET_QK_DIM), F32)],
        compiler_params=pltpu.CompilerParams(dimension_semantics=("arbitrary",),
                                             vmem_limit_bytes=60 * MIB),
        name="adaln_in_projection",
    )(h, norm_w.reshape(1, D_MODEL), mod, mod, w_in_t, w_in_t, w_in_t, inv, w_out)


def _ssd_init(xbuf, shifts, state, expand):
    xbuf[0:CONV_TAIL, :] = jnp.zeros((CONV_TAIL, SSD_CONV_DIM), BF16)
    row = lax.broadcasted_iota(jnp.int32, shifts.shape, 0)
    col = lax.broadcasted_iota(jnp.int32, shifts.shape, 1)
    delay = jnp.right_shift(row, CHUNK.bit_length() - 1) + 1
    t = jnp.bitwise_and(row, CHUNK - 1)
    shifts[...] = jnp.where(col == CONV_TAIL + t - delay, 1.0, 0.0).astype(BF16)
    state[...] = jnp.zeros_like(state)
    head_of_lane = jnp.right_shift(lax.broadcasted_iota(jnp.int32, (LANES, SSD_D_INNER), 1), HEAD_DIM_LOG2)
    row = lax.broadcasted_iota(jnp.int32, (LANES, SSD_D_INNER), 0)
    expand[...] = jnp.where(head_of_lane == row, 1.0, 0.0).astype(BF16)


def _ssd_chunk(z_ref, x_ref, b_ref, c_ref, dtraw_ref, cw_ref, cb_ref, dtb_ref, alog_ref, dskip_ref, nw_ref,
               y_ref, xbuf, shifts, state, expand):
    int_iota = lambda shape, dim: lax.broadcasted_iota(jnp.int32, shape, dim)

    xbuf[CONV_TAIL:, 0:SSD_D_INNER] = x_ref[...]
    xbuf[CONV_TAIL:, SSD_D_INNER:SSD_D_INNER + SSD_BC] = b_ref[...]
    xbuf[CONV_TAIL:, SSD_D_INNER + SSD_BC:] = c_ref[...]
    delayed = _dot(shifts[...], xbuf[...])
    acc = cb_ref[...] + cw_ref[SSD_CONV - 1:SSD_CONV, :] * xbuf[CONV_TAIL:, :].astype(F32)
    for k in range(SSD_CONV - 1):
        tap = SSD_CONV - 2 - k
        acc = acc + cw_ref[tap:tap + 1, :] * delayed[k * CHUNK:(k + 1) * CHUNK, :]
    xbuf[0:CONV_TAIL, :] = xbuf[CHUNK:CHUNK + CONV_TAIL, :]
    xbc = _silu(acc)
    xs = xbc[:, 0:SSD_D_INNER]
    xs_b = xs.astype(BF16)
    bmat = xbc[:, SSD_D_INNER:SSD_D_INNER + SSD_BC]
    cmat = xbc[:, SSD_D_INNER + SSD_BC:]

    pre = dtraw_ref[...] + dtb_ref[...]
    dt = jnp.maximum(pre, 0.0) + jnp.log1p(jnp.exp(-jnp.abs(pre)))
    d_a = dt * (-LOG2_E * jnp.exp(alog_ref[...]))
    causal = int_iota((CHUNK, CHUNK), 0) >= int_iota((CHUNK, CHUNK), 1)
    tril = jnp.where(causal, 1.0, 0.0).astype(BF16)
    acs = sum(_dot(tril, part) for part in _split3(d_a))
    acs_t = acs.T
    src_t = acs_t - jnp.log2(dt).T
    w_t = jnp.exp2(acs_t[:, CHUNK - 1:CHUNK] - src_t)
    end = jnp.broadcast_to(jnp.exp2(acs[CHUNK - 1:CHUNK, :]), (SUBLANES, LANES))
    end_row = sum(_dot(part, expand[...]) for part in _split3(end))[0:1, :]

    lane_blk = jnp.right_shift(int_iota((CHUNK, QUAD_W), 1), HEAD_DIM_LOG2)

    def block_diag(v):
        return jnp.concatenate([jnp.where(lane_blk == j, v, jnp.zeros_like(v)) for j in range(QUAD)], axis=0)

    def group(g, between_quads=lambda q: None):
        gs = slice(g * SSD_STATE, (g + 1) * SSD_STATE)
        b_g = bmat[:, gs]
        c_g = cmat[:, gs]
        cb_g = _dot_nt(c_g.astype(BF16), b_g.astype(BF16))
        b_gt = b_g.T
        y_parts = []
        for qi in range(HEADS_PER_GROUP // QUAD):
            q = g * (HEADS_PER_GROUP // QUAD) + qi
            between_quads(q)
            qs = slice(q * QUAD_W, (q + 1) * QUAD_W)
            lhs_diag, lhs_state, lhs_off = [], [], []
            for j in range(QUAD):
                hd = q * QUAD + j
                col = jnp.broadcast_to(acs[:, hd:hd + 1], (CHUNK, CHUNK))
                decay_dt = jnp.exp2(jnp.where(causal, col - src_t[hd:hd + 1, :], -jnp.inf))
                lhs_diag.append((cb_g * decay_dt).astype(BF16))
                lhs_state.append((b_gt * w_t[hd:hd + 1, :]).astype(BF16))
                lhs_off.append((c_g * jnp.exp2(col)).astype(BF16))
            x_blk = block_diag(xs_b[:, qs])
            s_prev = state[:, qs]
            s_blk = block_diag(s_prev.astype(BF16))
            y_q = _dot(jnp.concatenate(lhs_diag, axis=1), x_blk)
            y_q = y_q + _dot(jnp.concatenate(lhs_off, axis=1), s_blk)
            state[:, qs] = s_prev * end_row[:, qs] + _dot(jnp.concatenate(lhs_state, axis=1), x_blk)
            y_parts.append(y_q + xs[:, qs] * dskip_ref[:, qs])
        ws = slice(g * GROUP_W, (g + 1) * GROUP_W)
        u = jnp.concatenate(y_parts, axis=1) * z_ref[:, ws].astype(F32)
        u = u * lax.rsqrt(jnp.mean(u * u, axis=-1, keepdims=True) + EPS)
        y_ref[:, ws] = (u * nw_ref[:, ws]).astype(BF16)

    return group


def _ret_chunk(q_ref, k_ref, v0_ref, v1_ref, g0_ref, g1_ref, nw_ref, y_ref, state):
    heads_per_block = RET_QK_WIDTH // RET_V_DIM
    log_gammas = _log_gammas()
    causal = (lax.broadcasted_iota(jnp.int32, (CHUNK, CHUNK), 0)
              >= lax.broadcasted_iota(jnp.int32, (CHUNK, CHUNK), 1))

    def head(hd):
        ks = slice(hd * RET_QK_DIM, (hd + 1) * RET_QK_DIM)
        vs = slice(hd * RET_V_DIM, (hd + 1) * RET_V_DIM)
        v_ref, g_ref = (v0_ref, g0_ref) if hd < heads_per_block else (v1_ref, g1_ref)
        bs = slice((hd % heads_per_block) * RET_V_DIM, (hd % heads_per_block + 1) * RET_V_DIM)
        q_h = q_ref[:, ks]
        k_h = k_ref[:, ks]
        v_h = v_ref[:, bs]
        scores = jnp.where(causal, _dot_nt(q_h, k_h), 0.0)
        s_prev = state[hd]
        o = _dot(scores.astype(BF16), v_h) + _dot(q_h, s_prev.astype(BF16))
        state[hd] = (s_prev + _dot_tn(k_h, v_h)) * float(np.exp(np.float32(CHUNK * log_gammas[hd])))
        mu = jnp.mean(o, axis=-1, keepdims=True)
        d = o - mu
        y = d * lax.rsqrt(jnp.mean(d * d, axis=-1, keepdims=True) + EPS) * nw_ref[:, vs]
        y = (y * g_ref[:, bs].astype(F32)).astype(BF16)
        y_ref[:, SSD_D_INNER + hd * RET_V_DIM:SSD_D_INNER + (hd + 1) * RET_V_DIM] = y

    return head


def _mixer_kernel(z_ref, x_ref, b_ref, c_ref, dtraw_ref, cw_ref, cb_ref, dtb_ref, alog_ref, dskip_ref, snw_ref,
                  q_ref, k_ref, v0_ref, v1_ref, g0_ref, g1_ref, rnw_ref,
                  wout_hbm, h_ref, gate_ref, fw_ref, o_ref,
                  xbuf, shifts, ssd_state, expand, ret_state, ybuf, wout, wsem, *, n_steps, final):
    step = pl.program_id(0)

    @pl.when(step == 0)
    def _():
        wout_copy = pltpu.make_async_copy(wout_hbm, wout, wsem.at[0])
        wout_copy.start()
        _ssd_init(xbuf, shifts, ssd_state, expand)
        ret_state[...] = jnp.zeros_like(ret_state)
        ybuf[...] = jnp.zeros_like(ybuf)
        wout_copy.wait()

    slot = lax.rem(step, 2)
    piece_w = D_MODEL // N_PIECES

    def project_piece(y_old, n):
        return _dot(y_old, wout[:, n * piece_w:(n + 1) * piece_w])

    def finish(pieces):
        h = h_ref[...] + gate_ref[...] * jnp.concatenate(pieces, axis=1)
        if final:
            h = h * lax.rsqrt(jnp.mean(h * h, axis=-1, keepdims=True) + EPS) * fw_ref[...]
        o_ref[...] = h

    @pl.when(step < n_steps)
    def _():
        y_old = ybuf[1 - slot]
        pieces = []
        for sub in range(CHUNKS_PER_STEP):
            rows = pl.ds(sub * CHUNK, CHUNK)
            sub_refs = lambda *refs: [r.at[rows] for r in refs]
            y_new = ybuf.at[slot, rows]
            z_c, x_c, b_c, c_c, dtraw_c = sub_refs(z_ref, x_ref, b_ref, c_ref, dtraw_ref)
            ssd_group = _ssd_chunk(z_c, x_c, b_c, c_c, dtraw_c, cw_ref, cb_ref, dtb_ref, alog_ref, dskip_ref,
                                   snw_ref, y_new, xbuf, shifts, ssd_state, expand)
            q_c, k_c, v0_c, v1_c, g0_c, g1_c = sub_refs(q_ref, k_ref, v0_ref, v1_ref, g0_ref, g1_ref)
            ret_head = _ret_chunk(q_c, k_c, v0_c, v1_c, g0_c, g1_c, rnw_ref, y_new, ret_state)

            def between_quads(q):
                if q % CHUNKS_PER_STEP == 0:
                    pieces.append(project_piece(y_old, len(pieces)))
                ret_head(q)

            for g in range(SSD_GROUPS):
                ssd_group(g, between_quads)
        assert len(pieces) == N_PIECES
        finish(pieces)

    @pl.when(step == n_steps)
    def _():
        y_old = ybuf[1 - slot]
        finish([project_piece(y_old, n) for n in range(N_PIECES)])


def _mix_and_project(proj, dt_raw, h, mod, conv_w, conv_b, dt_bias, a_log, d_skip, ssd_norm_w, ret_norm_w, w_out,
                     final_norm_w, final):
    L = h.shape[0]
    rows = CHUNKS_PER_STEP * CHUNK
    n_steps = L // rows
    pad = lambda v: jnp.pad(v.reshape(1, SSD_HEADS), ((0, 0), (0, LANES - SSD_HEADS)))
    full = lambda shape: pl.BlockSpec(shape, lambda s: (0, 0))
    mixed = lambda width, col: pl.BlockSpec((rows, width), lambda s: (jnp.minimum(s, n_steps - 1), col))
    projected = lambda width, col: pl.BlockSpec((rows, width), lambda s: (jnp.maximum(s - 1, 0), col))
    return pl.pallas_call(
        functools.partial(_mixer_kernel, n_steps=n_steps, final=final),
        grid=(n_steps + 1,),
        in_specs=[mixed(SSD_D_INNER, COL_Z), mixed(SSD_D_INNER, COL_X), mixed(SSD_BC, COL_B), mixed(SSD_BC, COL_C),
                  mixed(LANES, 0),
                  full((SSD_CONV, SSD_CONV_DIM)), full((1, SSD_CONV_DIM)), full((1, LANES)), full((1, LANES)),
                  full((1, SSD_D_INNER)), full((1, SSD_D_INNER)),
                  mixed(RET_QK_WIDTH, COL_Q), mixed(RET_QK_WIDTH, COL_K),
                  mixed(RET_QK_WIDTH, COL_V), mixed(RET_QK_WIDTH, COL_V + 1),
                  mixed(RET_QK_WIDTH, COL_G), mixed(RET_QK_WIDTH, COL_G + 1),
                  full((1, RET_V_WIDTH)),
                  pl.BlockSpec(memory_space=pl.ANY),
                  projected(D_MODEL, 0),
                  pl.BlockSpec((1, D_MODEL), lambda s: (0, 2)),
                  full((1, D_MODEL))],
        out_specs=projected(D_MODEL, 0),
        out_shape=jax.ShapeDtypeStruct((L, D_MODEL), F32),
        scratch_shapes=[pltpu.VMEM((CONV_TAIL + CHUNK, SSD_CONV_DIM), BF16),
                        pltpu.VMEM(((SSD_CONV - 1) * CHUNK, CONV_TAIL + CHUNK), BF16),
                        pltpu.VMEM((SSD_STATE, SSD_D_INNER), F32),
                        pltpu.VMEM((LANES, SSD_D_INNER), BF16),
                        pltpu.VMEM((RET_HEADS, RET_QK_DIM, RET_V_DIM), F32),
                        pltpu.VMEM((2, rows, MIX_WIDTH), BF16),
                        pltpu.VMEM((MIX_WIDTH, D_MODEL), BF16),
                        pltpu.SemaphoreType.DMA((1,))],
        compiler_params=pltpu.CompilerParams(dimension_semantics=("arbitrary",),
                                             vmem_limit_bytes=60 * MIB),
        name="mixers_out_projection",
    )(proj, proj, proj, proj, dt_raw, conv_w, conv_b.reshape(1, SSD_CONV_DIM), pad(dt_bias), pad(a_log),
      jnp.repeat(d_skip, SSD_HEAD_DIM).reshape(1, SSD_D_INNER), ssd_norm_w.reshape(1, SSD_D_INNER),
      proj, proj, proj, proj, proj, proj, ret_norm_w.reshape(1, RET_V_WIDTH),
      w_out, h, mod, final_norm_w.reshape(1, D_MODEL))


def kernel(x, c, w_ada, b_ada, norm_w, w_in, conv_w, conv_b, dt_bias, a_log, d_skip, ssd_norm_w, ret_norm_w,
           w_out, final_norm_w):
    bsz, L, d_model = x.shape
    assert bsz == 1 and d_model == D_MODEL and L % TM_IN == 0 and w_in.shape[-1] == IN_WIDTH
    depth = w_in.shape[0]
    half = RET_QK_DIM // 2
    inv = ROPE_BASE ** (-jnp.arange(half, dtype=F32) / half)
    inv = jnp.concatenate([inv, inv]).reshape(1, RET_QK_DIM)
    h = x.reshape(L, D_MODEL)
    for layer in range(depth):
        mod = _modulation(c, w_ada[layer], b_ada[layer])
        proj, dt_raw, w_out_bf = _in_projection(h, norm_w[layer], mod, jnp.swapaxes(w_in[layer], 0, 1), inv,
                                                w_out[layer])
        h = _mix_and_project(proj, dt_raw, h, mod, conv_w[layer], conv_b[layer], dt_bias[layer], a_log[layer],
                             d_skip[layer], ssd_norm_w[layer], ret_norm_w[layer], w_out_bf, final_norm_w,
                             final=layer == depth - 1)
    return h.reshape(bsz, L, D_MODEL)
```

```python
import functools
import math

import numpy as np
import jax
import jax.numpy as jnp
from jax import lax
from jax.experimental import pallas as pl
from jax.experimental.pallas import tpu as pltpu

D_MODEL = 2048
SSD_D_INNER = D_MODEL
SSD_HEAD_DIM = 64
SSD_HEADS = SSD_D_INNER // SSD_HEAD_DIM
HEAD_DIM_LOG2 = SSD_HEAD_DIM.bit_length() - 1
SSD_GROUPS = 4
SSD_STATE = 128
SSD_CONV = 4
SSD_BC = SSD_GROUPS * SSD_STATE
SSD_CONV_DIM = SSD_D_INNER + 2 * SSD_BC
RET_HEADS = 8
RET_QK_DIM = 128
RET_V_DIM = 256
RET_QK_WIDTH = RET_HEADS * RET_QK_DIM
RET_V_WIDTH = RET_HEADS * RET_V_DIM
MIX_WIDTH = SSD_D_INNER + RET_V_WIDTH
CHUNK = 128
ROPE_BASE = 10000.0
EPS = 1e-6
LOG2_E = math.log2(math.e)

OFF_Z = SSD_D_INNER
OFF_XBC = OFF_Z + SSD_CONV_DIM
OFF_DT = OFF_XBC + SSD_HEADS
OFF_Q = OFF_DT + RET_QK_WIDTH
OFF_K = OFF_Q + RET_QK_WIDTH
OFF_V = OFF_K + RET_V_WIDTH
IN_WIDTH = OFF_V + RET_V_WIDTH

PROJ_WIDTH = IN_WIDTH - SSD_HEADS
COL_Z, COL_X = 0, 1
COL_B, COL_C = 8, 9
COL_Q, COL_K, COL_V, COL_G = 5, 6, 7, 9

LANES = 128
SUBLANES = 8
BF16_SUBLANES = 16
QUAD = 4
QUAD_W = QUAD * SSD_HEAD_DIM
HEADS_PER_GROUP = SSD_HEADS // SSD_GROUPS
GROUP_W = SSD_D_INNER // SSD_GROUPS
CONV_TAIL = BF16_SUBLANES
CHUNKS_PER_STEP = 2
N_PIECES = SSD_HEADS // QUAD

MOD_TILE = 512
MOD_BUFFERS = 3
TM_IN, TN_IN = 1024, 1024
TILE_CONV0, TILE_ROT0, TILE_V0, TILE_G0 = (c // TN_IN for c in (OFF_Z, OFF_XBC, OFF_XBC + 2 * RET_QK_WIDTH,
                                                                 OFF_XBC + 2 * RET_QK_WIDTH + RET_V_WIDTH))
N_COL_TILES = PROJ_WIDTH // TN_IN
assert OFF_Z % TN_IN == 0 and OFF_XBC % TN_IN == 0 and RET_QK_WIDTH == TN_IN and RET_V_WIDTH % TN_IN == 0
assert PROJ_WIDTH % TN_IN == 0 and TM_IN % CHUNK == 0 and N_COL_TILES - 1 >= TILE_G0
MIB = 1024 * 1024

F32 = jnp.float32
BF16 = jnp.bfloat16


def _silu(v):
    return v / (1.0 + jnp.exp(-v))


def _dot(a, b):
    return jnp.dot(a, b, preferred_element_type=F32)


def _dot_nt(a, b):
    return lax.dot_general(a, b, (((1,), (1,)), ((), ())), preferred_element_type=F32)


def _dot_tn(a, b):
    return lax.dot_general(a, b, (((0,), (0,)), ((), ())), preferred_element_type=F32)


def _split3(v):
    hi = v.astype(BF16)
    r1 = v - hi.astype(F32)
    mid = r1.astype(BF16)
    lo = (r1 - mid.astype(F32)).astype(BF16)
    return hi, mid, lo


def _log_gammas():
    return [float(np.log1p(-np.exp2(np.float32(-5.0 - hd)), dtype=np.float32)) for hd in range(RET_HEADS)]


def _mod_kernel(c_ref, w_hbm, b_ref, o_ref, wbuf, sem):
    n_tiles = o_ref.shape[1] // MOD_TILE

    def fetch(j):
        slot = j % MOD_BUFFERS
        return pltpu.make_async_copy(w_hbm.at[:, pl.ds(j * MOD_TILE, MOD_TILE)], wbuf.at[slot], sem.at[slot])

    for j in range(min(MOD_BUFFERS - 1, n_tiles)):
        fetch(j).start()
    cond = jnp.broadcast_to(_silu(c_ref[...]), (SUBLANES, D_MODEL))
    for j in range(n_tiles):
        if j + MOD_BUFFERS - 1 < n_tiles:
            fetch(j + MOD_BUFFERS - 1).start()
        fetch(j).wait()
        cols = slice(j * MOD_TILE, (j + 1) * MOD_TILE)
        o_ref[:, cols] = _dot(cond, wbuf[j % MOD_BUFFERS])[0:1, :] + b_ref[:, cols]


def _modulation(c, w_ada, b_ada):
    n = w_ada.shape[1]
    assert n % MOD_TILE == 0
    vmem = pl.BlockSpec(memory_space=pltpu.VMEM)
    return pl.pallas_call(
        _mod_kernel,
        in_specs=[vmem, pl.BlockSpec(memory_space=pl.ANY), vmem],
        out_specs=vmem,
        out_shape=jax.ShapeDtypeStruct((1, n), F32),
        scratch_shapes=[pltpu.VMEM((MOD_BUFFERS, D_MODEL, MOD_TILE), F32),
                        pltpu.SemaphoreType.DMA((MOD_BUFFERS,))],
        compiler_params=pltpu.CompilerParams(vmem_limit_bytes=24 * MIB),
        name="adaln_modulation",
    )(c, w_ada, b_ada.reshape(1, n))


def _inproj_kernel(x_ref, nw_ref, shift_ref, scale_ref, wa_ref, wb_ref, wd_ref, inv_ref, wout_ref,
                   o_ref, dt_ref, wout_bf_ref, u_ref, prev, cos_in, sin_in, cos_t, sin_t, decay, *, n_tiles):
    t = pl.program_id(0)
    tile = jnp.minimum(t, n_tiles - 1)
    i = tile // N_COL_TILES
    j = tile - i * N_COL_TILES
    je = lax.rem(jnp.maximum(t - 1, 0), N_COL_TILES)
    half = RET_QK_DIM // 2

    @pl.when(t == 0)
    def _():
        prev[...] = jnp.zeros_like(prev)
        rows = lambda n: lax.broadcasted_iota(jnp.int32, (n, RET_QK_DIM), 0).astype(F32)
        ang_r = rows(CHUNK) * inv_ref[...]
        ang_blk = rows(TM_IN // CHUNK) * float(CHUNK) * inv_ref[...]
        cos_r, sin_r, cos_blk, sin_blk = jnp.cos(ang_r), jnp.sin(ang_r), jnp.cos(ang_blk), jnp.sin(ang_blk)
        for blk in range(TM_IN // CHUNK):
            chunk = slice(blk * CHUNK, (blk + 1) * CHUNK)
            cb, sb = cos_blk[blk:blk + 1, :], sin_blk[blk:blk + 1, :]
            cos_in[chunk, :] = cos_r * cb - sin_r * sb
            sin_in[chunk, :] = sin_r * cb + cos_r * sb
        pos = lax.broadcasted_iota(jnp.int32, (CHUNK, RET_QK_DIM), 0).astype(F32) + 1.0
        for hd, lg in enumerate(_log_gammas()):
            decay[0, hd] = jnp.exp(pos * lg)
            decay[1, hd] = jnp.exp(pos * -lg) * RET_QK_DIM ** -0.5

    @pl.when(jnp.logical_and(j == 0, t < n_tiles))
    def _():
        x = x_ref[...]
        gain = nw_ref[...] * (1.0 + scale_ref[...])
        u = (x * lax.rsqrt(jnp.mean(x * x, axis=-1, keepdims=True) + EPS) * gain + shift_ref[...]).astype(BF16)
        u_ref[...] = u
        is_dt = lax.broadcasted_iota(jnp.int32, (LANES, D_MODEL), 0) < SSD_HEADS
        dt_ref[...] = _dot_nt(u, jnp.where(is_dt, wd_ref[...], 0.0).astype(BF16))

        base = jnp.broadcast_to((i * TM_IN).astype(F32) * inv_ref[...], (SUBLANES, RET_QK_DIM))
        cos_b = jnp.cos(base)[0:1, :]
        sin_b = jnp.sin(base)[0:1, :]
        cos_t[...] = cos_in[...] * cos_b - sin_in[...] * sin_b
        sin_pos = sin_in[...] * cos_b + cos_in[...] * sin_b
        first_half = lax.broadcasted_iota(jnp.int32, (TM_IN, RET_QK_DIM), 1) < half
        sin_t[...] = jnp.where(first_half, -sin_pos, sin_pos)

    def project():
        after_dt = j >= TILE_ROT0
        skip = pl.multiple_of(jnp.where(after_dt, SSD_HEADS, 0), SSD_HEADS)
        head = wa_ref[pl.ds(skip, TN_IN - SSD_HEADS), :]
        tail = jnp.where(after_dt, wb_ref[...], wa_ref[TN_IN - SSD_HEADS:, :])
        w = jnp.concatenate([head, tail], axis=0).astype(BF16)
        wout_bf_ref[...] = wout_ref[...].astype(BF16)
        return _dot_nt(u_ref[...], w)

    @pl.when(jnp.logical_and(jnp.logical_or(je < TILE_CONV0, je >= TILE_G0), t < n_tiles))
    def _():
        acc = project()
        o_ref[...] = _silu(prev[...]).astype(BF16)
        prev[...] = acc

    @pl.when(t == n_tiles)
    def _():
        o_ref[...] = _silu(prev[...]).astype(BF16)

    @pl.when(jnp.logical_or(jnp.logical_and(je >= TILE_CONV0, je < TILE_ROT0),
                            jnp.logical_and(je >= TILE_V0, je < TILE_G0)))
    def _():
        acc = project()
        o_ref[...] = prev[...].astype(BF16)
        prev[...] = acc

    @pl.when(jnp.logical_and(je >= TILE_ROT0, je < TILE_V0))
    def _():
        acc = project()
        side = je - TILE_ROT0
        for hd in range(RET_HEADS):
            cols = slice(hd * RET_QK_DIM, (hd + 1) * RET_QK_DIM)
            scale = decay[side, hd]
            for blk in range(TM_IN // CHUNK):
                rows = slice(blk * CHUNK, (blk + 1) * CHUNK)
                a = prev[rows, cols]
                rot = a * cos_t[rows, :] + pltpu.roll(a, half, 1) * sin_t[rows, :]
                o_ref[rows, cols] = (rot * scale).astype(BF16)
        prev[...] = acc


def _in_projection(h, norm_w, mod, w_in_t, inv, w_out):
    L = h.shape[0]
    n_tiles = (L // TM_IN) * N_COL_TILES
    n_cast = min(1 << (n_tiles.bit_length() - 1), MIX_WIDTH // BF16_SUBLANES)
    cast_block = pl.BlockSpec((MIX_WIDTH // n_cast, D_MODEL), lambda t: (jnp.minimum(t, n_cast - 1), 0))
    const = lambda t: (0, 0)
    mul_row = lambda t: jnp.minimum(t, n_tiles - 1) // N_COL_TILES
    mul_col = lambda t: lax.rem(jnp.minimum(t, n_tiles - 1), N_COL_TILES)
    epi = lambda t: jnp.maximum(t - 1, 0)
    table = pltpu.VMEM((TM_IN, RET_QK_DIM), F32)
    return pl.pallas_call(
        functools.partial(_inproj_kernel, n_tiles=n_tiles),
        grid=(n_tiles + 1,),
        in_specs=[pl.BlockSpec((TM_IN, D_MODEL), lambda t: (mul_row(t), 0)),
                  pl.BlockSpec((1, D_MODEL), const),
                  pl.BlockSpec((1, D_MODEL), lambda t: (0, 0)),
                  pl.BlockSpec((1, D_MODEL), lambda t: (0, 1)),
                  pl.BlockSpec((TN_IN, D_MODEL), lambda t: (mul_col(t), 0)),
                  pl.BlockSpec((SSD_HEADS, D_MODEL), lambda t: ((mul_col(t) + 1) * (TN_IN // SSD_HEADS), 0)),
                  pl.BlockSpec((LANES, D_MODEL), lambda t: (OFF_XBC // LANES, 0)),
                  pl.BlockSpec((1, RET_QK_DIM), const),
                  cast_block],
        out_specs=[pl.BlockSpec((TM_IN, TN_IN), lambda t: (epi(t) // N_COL_TILES, lax.rem(epi(t), N_COL_TILES))),
                   pl.BlockSpec((TM_IN, LANES), lambda t: (mul_row(t), 0)),
                   cast_block],
        out_shape=[jax.ShapeDtypeStruct((L, PROJ_WIDTH), BF16),
                   jax.ShapeDtypeStruct((L, LANES), F32),
                   jax.ShapeDtypeStruct((MIX_WIDTH, D_MODEL), BF16)],
        scratch_shapes=[pltpu.VMEM((TM_IN, D_MODEL), BF16),
                        pltpu.VMEM((TM_IN, TN_IN), F32),
                        table, table, table, table,
                        pltpu.VMEM((2, RET_HEADS, CHUNK, RET_QK_DIM), F32)],
        compiler_params=pltpu.CompilerParams(dimension_semantics=("arbitrary",),
                                             vmem_limit_bytes=60 * MIB),
        name="adaln_in_projection",
    )(h, norm_w.reshape(1, D_MODEL), mod, mod, w_in_t, w_in_t, w_in_t, inv, w_out)


def _ssd_init(xbuf, shifts, state, expand):
    xbuf[0:CONV_TAIL, :] = jnp.zeros((CONV_TAIL, SSD_CONV_DIM), BF16)
    row = lax.broadcasted_iota(jnp.int32, shifts.shape, 0)
    col = lax.broadcasted_iota(jnp.int32, shifts.shape, 1)
    delay = jnp.right_shift(row, CHUNK.bit_length() - 1) + 1
    t = jnp.bitwise_and(row, CHUNK - 1)
    shifts[...] = jnp.where(col == CONV_TAIL + t - delay, 1.0, 0.0).astype(BF16)
    state[...] = jnp.zeros_like(state)
    head_of_lane = jnp.right_shift(lax.broadcasted_iota(jnp.int32, (LANES, SSD_D_INNER), 1), HEAD_DIM_LOG2)
    row = lax.broadcasted_iota(jnp.int32, (LANES, SSD_D_INNER), 0)
    expand[...] = jnp.where(head_of_lane == row, 1.0, 0.0).astype(BF16)


def _ssd_chunk(z_ref, x_ref, b_ref, c_ref, dtraw_ref, cw_ref, cb_ref, dtb_ref, alog_ref, dskip_ref, nw_ref,
               y_ref, xbuf, shifts, state, expand):
    int_iota = lambda shape, dim: lax.broadcasted_iota(jnp.int32, shape, dim)

    xbuf[CONV_TAIL:, 0:SSD_D_INNER] = x_ref[...]
    xbuf[CONV_TAIL:, SSD_D_INNER:SSD_D_INNER + SSD_BC] = b_ref[...]
    xbuf[CONV_TAIL:, SSD_D_INNER + SSD_BC:] = c_ref[...]
    delayed = _dot(shifts[...], xbuf[...])
    acc = cb_ref[...] + cw_ref[SSD_CONV - 1:SSD_CONV, :] * xbuf[CONV_TAIL:, :].astype(F32)
    for k in range(SSD_CONV - 1):
        tap = SSD_CONV - 2 - k
        acc = acc + cw_ref[tap:tap + 1, :] * delayed[k * CHUNK:(k + 1) * CHUNK, :]
    xbuf[0:CONV_TAIL, :] = xbuf[CHUNK:CHUNK + CONV_TAIL, :]
    xbc = _silu(acc)
    xs = xbc[:, 0:SSD_D_INNER]
    xs_b = xs.astype(BF16)
    bmat = xbc[:, SSD_D_INNER:SSD_D_INNER + SSD_BC]
    cmat = xbc[:, SSD_D_INNER + SSD_BC:]

    pre = dtraw_ref[...] + dtb_ref[...]
    dt = jnp.maximum(pre, 0.0) + jnp.log1p(jnp.exp(-jnp.abs(pre)))
    d_a = dt * (-LOG2_E * jnp.exp(alog_ref[...]))
    causal = int_iota((CHUNK, CHUNK), 0) >= int_iota((CHUNK, CHUNK), 1)
    tril = jnp.where(causal, 1.0, 0.0).astype(BF16)
    acs = sum(_dot(tril, part) for part in _split3(d_a))
    acs_t = acs.T
    src_t = acs_t - jnp.log2(dt).T
    w_t = jnp.exp2(acs_t[:, CHUNK - 1:CHUNK] - src_t)
    end = jnp.broadcast_to(jnp.exp2(acs[CHUNK - 1:CHUNK, :]), (SUBLANES, LANES))
    end_row = sum(_dot(part, expand[...]) for part in _split3(end))[0:1, :]

    lane_blk = jnp.right_shift(int_iota((CHUNK, QUAD_W), 1), HEAD_DIM_LOG2)

    def block_diag(v):
        return jnp.concatenate([jnp.where(lane_blk == j, v, jnp.zeros_like(v)) for j in range(QUAD)], axis=0)

    def group(g, between_quads=lambda q: None):
        gs = slice(g * SSD_STATE, (g + 1) * SSD_STATE)
        b_g = bmat[:, gs]
        c_g = cmat[:, gs]
        cb_g = _dot_nt(c_g.astype(BF16), b_g.astype(BF16))
        b_gt = b_g.T
        y_parts = []
        for qi in range(HEADS_PER_GROUP // QUAD):
            q = g * (HEADS_PER_GROUP // QUAD) + qi
            between_quads(q)
            qs = slice(q * QUAD_W, (q + 1) * QUAD_W)
            lhs_diag, lhs_state, lhs_off = [], [], []
            for j in range(QUAD):
                hd = q * QUAD + j
                col = jnp.broadcast_to(acs[:, hd:hd + 1], (CHUNK, CHUNK))
                decay_dt = jnp.exp2(jnp.where(causal, col - src_t[hd:hd + 1, :], -jnp.inf))
                lhs_diag.append((cb_g * decay_dt).astype(BF16))
                lhs_state.append((b_gt * w_t[hd:hd + 1, :]).astype(BF16))
                lhs_off.append((c_g * jnp.exp2(col)).astype(BF16))
            x_blk = block_diag(xs_b[:, qs])
            s_prev = state[:, qs]
            s_blk = block_diag(s_prev.astype(BF16))
            y_q = _dot(jnp.concatenate(lhs_diag, axis=1), x_blk)
            y_q = y_q + _dot(jnp.concatenate(lhs_off, axis=1), s_blk)
            state[:, qs] = s_prev * end_row[:, qs] + _dot(jnp.concatenate(lhs_state, axis=1), x_blk)
            y_parts.append(y_q + xs[:, qs] * dskip_ref[:, qs])
        ws = slice(g * GROUP_W, (g + 1) * GROUP_W)
        u = jnp.concatenate(y_parts, axis=1) * z_ref[:, ws].astype(F32)
        u = u * lax.rsqrt(jnp.mean(u * u, axis=-1, keepdims=True) + EPS)
        y_ref[:, ws] = (u * nw_ref[:, ws]).astype(BF16)

    return group


def _ret_chunk(q_ref, k_ref, v0_ref, v1_ref, g0_ref, g1_ref, nw_ref, y_ref, state):
    heads_per_block = RET_QK_WIDTH // RET_V_DIM
    log_gammas = _log_gammas()
    causal = (lax.broadcasted_iota(jnp.int32, (CHUNK, CHUNK), 0)
              >= lax.broadcasted_iota(jnp.int32, (CHUNK, CHUNK), 1))

    def head(hd):
        ks = slice(hd * RET_QK_DIM, (hd + 1) * RET_QK_DIM)
        vs = slice(hd * RET_V_DIM, (hd + 1) * RET_V_DIM)
        v_ref, g_ref = (v0_ref, g0_ref) if hd < heads_per_block else (v1_ref, g1_ref)
        bs = slice((hd % heads_per_block) * RET_V_DIM, (hd % heads_per_block + 1) * RET_V_DIM)
        q_h = q_ref[:, ks]
        k_h = k_ref[:, ks]
        v_h = v_ref[:, bs]
        scores = jnp.where(causal, _dot_nt(q_h, k_h), 0.0)
        s_prev = state[hd]
        o = _dot(scores.astype(BF16), v_h) + _dot(q_h, s_prev.astype(BF16))
        state[hd] = (s_prev + _dot_tn(k_h, v_h)) * float(np.exp(np.float32(CHUNK * log_gammas[hd])))
        mu = jnp.mean(o, axis=-1, keepdims=True)
        d = o - mu
        y = d * lax.rsqrt(jnp.mean(d * d, axis=-1, keepdims=True) + EPS) * nw_ref[:, vs]
        y = (y * g_ref[:, bs].astype(F32)).astype(BF16)
        y_ref[:, SSD_D_INNER + hd * RET_V_DIM:SSD_D_INNER + (hd + 1) * RET_V_DIM] = y

    return head


def _mixer_kernel(proj_ref, dtraw_ref, cw_ref, cb_ref, dtb_ref, alog_ref, dskip_ref, snw_ref, rnw_ref,
                  wout_hbm, h_ref, gate_ref, fw_ref, o_ref,
                  xbuf, shifts, ssd_state, expand, ret_state, ybuf, wout, wsem, *, n_steps, final):
    step = pl.program_id(0)

    @pl.when(step == 0)
    def _():
        wout_copy = pltpu.make_async_copy(wout_hbm, wout, wsem.at[0])
        wout_copy.start()
        _ssd_init(xbuf, shifts, ssd_state, expand)
        ret_state[...] = jnp.zeros_like(ret_state)
        ybuf[...] = jnp.zeros_like(ybuf)
        wout_copy.wait()

    slot = lax.rem(step, 2)
    piece_w = D_MODEL // N_PIECES
    cols = lambda width, blk: proj_ref.at[:, pl.ds(blk * width, width)]
    z_ref, x_ref = cols(SSD_D_INNER, COL_Z), cols(SSD_D_INNER, COL_X)
    b_ref, c_ref = cols(SSD_BC, COL_B), cols(SSD_BC, COL_C)
    q_ref, k_ref = cols(RET_QK_WIDTH, COL_Q), cols(RET_QK_WIDTH, COL_K)
    v0_ref, v1_ref = cols(RET_QK_WIDTH, COL_V), cols(RET_QK_WIDTH, COL_V + 1)
    g0_ref, g1_ref = cols(RET_QK_WIDTH, COL_G), cols(RET_QK_WIDTH, COL_G + 1)

    def project_piece(y_old, n):
        return _dot(y_old, wout[:, n * piece_w:(n + 1) * piece_w])

    def finish(pieces):
        h = h_ref[...] + gate_ref[...] * jnp.concatenate(pieces, axis=1)
        if final:
            h = h * lax.rsqrt(jnp.mean(h * h, axis=-1, keepdims=True) + EPS) * fw_ref[...]
        o_ref[...] = h

    @pl.when(step < n_steps)
    def _():
        y_old = ybuf[1 - slot]
        pieces = []
        for sub in range(CHUNKS_PER_STEP):
            rows = pl.ds(sub * CHUNK, CHUNK)
            sub_refs = lambda *refs: [r.at[rows] for r in refs]
            y_new = ybuf.at[slot, rows]
            z_c, x_c, b_c, c_c, dtraw_c = sub_refs(z_ref, x_ref, b_ref, c_ref, dtraw_ref)
            ssd_group = _ssd_chunk(z_c, x_c, b_c, c_c, dtraw_c, cw_ref, cb_ref, dtb_ref, alog_ref, dskip_ref,
                                   snw_ref, y_new, xbuf, shifts, ssd_state, expand)
            q_c, k_c, v0_c, v1_c, g0_c, g1_c = sub_refs(q_ref, k_ref, v0_ref, v1_ref, g0_ref, g1_ref)
            ret_head = _ret_chunk(q_c, k_c, v0_c, v1_c, g0_c, g1_c, rnw_ref, y_new, ret_state)

            def between_quads(q):
                if q % CHUNKS_PER_STEP == 0:
                    pieces.append(project_piece(y_old, len(pieces)))
                ret_head(q)

            for g in range(SSD_GROUPS):
                ssd_group(g, between_quads)
        assert len(pieces) == N_PIECES
        finish(pieces)

    @pl.when(step == n_steps)
    def _():
        y_old = ybuf[1 - slot]
        finish([project_piece(y_old, n) for n in range(N_PIECES)])


def _mix_and_project(proj, dt_raw, h, mod, conv_w, conv_b, dt_bias, a_log, d_skip, ssd_norm_w, ret_norm_w, w_out,
                     final_norm_w, final):
    L = h.shape[0]
    rows = CHUNKS_PER_STEP * CHUNK
    n_steps = L // rows
    pad = lambda v: jnp.pad(v.reshape(1, SSD_HEADS), ((0, 0), (0, LANES - SSD_HEADS)))
    full = lambda shape: pl.BlockSpec(shape, lambda s: (0, 0))
    mixed = lambda width, col: pl.BlockSpec((rows, width), lambda s: (jnp.minimum(s, n_steps - 1), col))
    projected = lambda width, col: pl.BlockSpec((rows, width), lambda s: (jnp.maximum(s - 1, 0), col))
    return pl.pallas_call(
        functools.partial(_mixer_kernel, n_steps=n_steps, final=final),
        grid=(n_steps + 1,),
        in_specs=[mixed(PROJ_WIDTH, 0), mixed(LANES, 0),
                  full((SSD_CONV, SSD_CONV_DIM)), full((1, SSD_CONV_DIM)), full((1, LANES)), full((1, LANES)),
                  full((1, SSD_D_INNER)), full((1, SSD_D_INNER)),
                  full((1, RET_V_WIDTH)),
                  pl.BlockSpec(memory_space=pl.ANY),
                  projected(D_MODEL, 0),
                  pl.BlockSpec((1, D_MODEL), lambda s: (0, 2)),
                  full((1, D_MODEL))],
        out_specs=projected(D_MODEL, 0),
        out_shape=jax.ShapeDtypeStruct((L, D_MODEL), F32),
        scratch_shapes=[pltpu.VMEM((CONV_TAIL + CHUNK, SSD_CONV_DIM), BF16),
                        pltpu.VMEM(((SSD_CONV - 1) * CHUNK, CONV_TAIL + CHUNK), BF16),
                        pltpu.VMEM((SSD_STATE, SSD_D_INNER), F32),
                        pltpu.VMEM((LANES, SSD_D_INNER), BF16),
                        pltpu.VMEM((RET_HEADS, RET_QK_DIM, RET_V_DIM), F32),
                        pltpu.VMEM((2, rows, MIX_WIDTH), BF16),
                        pltpu.VMEM((MIX_WIDTH, D_MODEL), BF16),
                        pltpu.SemaphoreType.DMA((1,))],
        compiler_params=pltpu.CompilerParams(dimension_semantics=("arbitrary",),
                                             vmem_limit_bytes=60 * MIB),
        name="mixers_out_projection",
    )(proj, dt_raw, conv_w, conv_b.reshape(1, SSD_CONV_DIM), pad(dt_bias), pad(a_log),
      jnp.repeat(d_skip, SSD_HEAD_DIM).reshape(1, SSD_D_INNER), ssd_norm_w.reshape(1, SSD_D_INNER),
      ret_norm_w.reshape(1, RET_V_WIDTH),
      w_out, h, mod, final_norm_w.reshape(1, D_MODEL))


def kernel(x, c, w_ada, b_ada, norm_w, w_in, conv_w, conv_b, dt_bias, a_log, d_skip, ssd_norm_w, ret_norm_w,
           w_out, final_norm_w):
    bsz, L, d_model = x.shape
    assert bsz == 1 and d_model == D_MODEL and L % TM_IN == 0 and w_in.shape[-1] == IN_WIDTH
    depth = w_in.shape[0]
    half = RET_QK_DIM // 2
    inv = ROPE_BASE ** (-jnp.arange(half, dtype=F32) / half)
    inv = jnp.concatenate([inv, inv]).reshape(1, RET_QK_DIM)
    h = x.reshape(L, D_MODEL)
    for layer in range(depth):
        mod = _modulation(c, w_ada[layer], b_ada[layer])
        proj, dt_raw, w_out_bf = _in_projection(h, norm_w[layer], mod, jnp.swapaxes(w_in[layer], 0, 1), inv,
                                                w_out[layer])
        h = _mix_and_project(proj, dt_raw, h, mod, conv_w[layer], conv_b[layer], dt_bias[layer], a_log[layer],
                             d_skip[layer], ssd_norm_w[layer], ret_norm_w[layer], w_out_bf, final_norm_w,
                             final=layer == depth - 1)
    return h.reshape(bsz, L, D_MODEL)
```

```python
import functools
import math

import numpy as np
import jax
import jax.numpy as jnp
from jax import lax
from jax.experimental import pallas as pl
from jax.experimental.pallas import tpu as pltpu

D_MODEL = 2048
SSD_D_INNER = D_MODEL
SSD_HEAD_DIM = 64
SSD_HEADS = SSD_D_INNER // SSD_HEAD_DIM
HEAD_DIM_LOG2 = SSD_HEAD_DIM.bit_length() - 1
SSD_GROUPS = 4
SSD_STATE = 128
SSD_CONV = 4
SSD_BC = SSD_GROUPS * SSD_STATE
SSD_CONV_DIM = SSD_D_INNER + 2 * SSD_BC
RET_HEADS = 8
RET_QK_DIM = 128
RET_V_DIM = 256
RET_QK_WIDTH = RET_HEADS * RET_QK_DIM
RET_V_WIDTH = RET_HEADS * RET_V_DIM
MIX_WIDTH = SSD_D_INNER + RET_V_WIDTH
CHUNK = 128
ROPE_BASE = 10000.0
EPS = 1e-6
LOG2_E = math.log2(math.e)

OFF_Z = SSD_D_INNER
OFF_XBC = OFF_Z + SSD_CONV_DIM
OFF_DT = OFF_XBC + SSD_HEADS
OFF_Q = OFF_DT + RET_QK_WIDTH
OFF_K = OFF_Q + RET_QK_WIDTH
OFF_V = OFF_K + RET_V_WIDTH
IN_WIDTH = OFF_V + RET_V_WIDTH

PROJ_WIDTH = IN_WIDTH - SSD_HEADS
COL_Z, COL_X = 0, 1
COL_B, COL_C = 8, 9
COL_Q, COL_K, COL_V, COL_G = 5, 6, 7, 9

LANES = 128
SUBLANES = 8
BF16_SUBLANES = 16
QUAD = 4
QUAD_W = QUAD * SSD_HEAD_DIM
HEADS_PER_GROUP = SSD_HEADS // SSD_GROUPS
GROUP_W = SSD_D_INNER // SSD_GROUPS
CONV_TAIL = BF16_SUBLANES
CHUNKS_PER_STEP = 2
N_PIECES = SSD_HEADS // QUAD

MOD_TILE = 512
MOD_BUFFERS = 3
TM_IN, TN_IN = 1024, 1024
TILE_CONV0, TILE_ROT0, TILE_V0, TILE_G0 = (c // TN_IN for c in (OFF_Z, OFF_XBC, OFF_XBC + 2 * RET_QK_WIDTH,
                                                                 OFF_XBC + 2 * RET_QK_WIDTH + RET_V_WIDTH))
N_COL_TILES = PROJ_WIDTH // TN_IN
assert OFF_Z % TN_IN == 0 and OFF_XBC % TN_IN == 0 and RET_QK_WIDTH == TN_IN and RET_V_WIDTH % TN_IN == 0
assert PROJ_WIDTH % TN_IN == 0 and TM_IN % CHUNK == 0 and N_COL_TILES - 1 >= TILE_G0
MIB = 1024 * 1024

F32 = jnp.float32
BF16 = jnp.bfloat16


def _silu(v):
    return v / (1.0 + jnp.exp(-v))


def _dot(a, b):
    return jnp.dot(a, b, preferred_element_type=F32)


def _dot_nt(a, b):
    return lax.dot_general(a, b, (((1,), (1,)), ((), ())), preferred_element_type=F32)


def _dot_tn(a, b):
    return lax.dot_general(a, b, (((0,), (0,)), ((), ())), preferred_element_type=F32)


def _split3(v):
    hi = v.astype(BF16)
    r1 = v - hi.astype(F32)
    mid = r1.astype(BF16)
    lo = (r1 - mid.astype(F32)).astype(BF16)
    return hi, mid, lo


def _log_gammas():
    return [float(np.log1p(-np.exp2(np.float32(-5.0 - hd)), dtype=np.float32)) for hd in range(RET_HEADS)]


def _mod_kernel(c_ref, w_hbm, b_ref, o_ref, wbuf, sem):
    n_tiles = o_ref.shape[1] // MOD_TILE

    def fetch(j):
        slot = j % MOD_BUFFERS
        return pltpu.make_async_copy(w_hbm.at[:, pl.ds(j * MOD_TILE, MOD_TILE)], wbuf.at[slot], sem.at[slot])

    for j in range(min(MOD_BUFFERS - 1, n_tiles)):
        fetch(j).start()
    cond = jnp.broadcast_to(_silu(c_ref[...]), (SUBLANES, D_MODEL))
    for j in range(n_tiles):
        if j + MOD_BUFFERS - 1 < n_tiles:
            fetch(j + MOD_BUFFERS - 1).start()
        fetch(j).wait()
        cols = slice(j * MOD_TILE, (j + 1) * MOD_TILE)
        o_ref[:, cols] = _dot(cond, wbuf[j % MOD_BUFFERS])[0:1, :] + b_ref[:, cols]


def _modulation(c, w_ada, b_ada):
    n = w_ada.shape[1]
    assert n % MOD_TILE == 0
    vmem = pl.BlockSpec(memory_space=pltpu.VMEM)
    return pl.pallas_call(
        _mod_kernel,
        in_specs=[vmem, pl.BlockSpec(memory_space=pl.ANY), vmem],
        out_specs=vmem,
        out_shape=jax.ShapeDtypeStruct((1, n), F32),
        scratch_shapes=[pltpu.VMEM((MOD_BUFFERS, D_MODEL, MOD_TILE), F32),
                        pltpu.SemaphoreType.DMA((MOD_BUFFERS,))],
        compiler_params=pltpu.CompilerParams(vmem_limit_bytes=24 * MIB),
        name="adaln_modulation",
    )(c, w_ada, b_ada.reshape(1, n))


def _inproj_kernel(x_ref, nw_ref, shift_ref, scale_ref, wa_ref, wb_ref, wd_ref, inv_ref, wout_ref,
                   o_ref, dt_ref, wout_bf_ref, u_ref, prev, cos_in, sin_in, cos_t, sin_t, decay, *, n_tiles):
    t = pl.program_id(0)
    tile = jnp.minimum(t, n_tiles - 1)
    i = tile // N_COL_TILES
    j = tile - i * N_COL_TILES
    je = lax.rem(jnp.maximum(t - 1, 0), N_COL_TILES)
    half = RET_QK_DIM // 2

    @pl.when(t == 0)
    def _():
        prev[...] = jnp.zeros_like(prev)
        rows = lambda n: lax.broadcasted_iota(jnp.int32, (n, RET_QK_DIM), 0).astype(F32)
        ang_r = rows(CHUNK) * inv_ref[...]
        ang_blk = rows(TM_IN // CHUNK) * float(CHUNK) * inv_ref[...]
        cos_r, sin_r, cos_blk, sin_blk = jnp.cos(ang_r), jnp.sin(ang_r), jnp.cos(ang_blk), jnp.sin(ang_blk)
        for blk in range(TM_IN // CHUNK):
            chunk = slice(blk * CHUNK, (blk + 1) * CHUNK)
            cb, sb = cos_blk[blk:blk + 1, :], sin_blk[blk:blk + 1, :]
            cos_in[chunk, :] = cos_r * cb - sin_r * sb
            sin_in[chunk, :] = sin_r * cb + cos_r * sb
        pos = lax.broadcasted_iota(jnp.int32, (CHUNK, RET_QK_DIM), 0).astype(F32) + 1.0
        for hd, lg in enumerate(_log_gammas()):
            decay[0, hd] = jnp.exp(pos * lg)
            decay[1, hd] = jnp.exp(pos * -lg) * RET_QK_DIM ** -0.5

    @pl.when(jnp.logical_and(j == 0, t < n_tiles))
    def _():
        x = x_ref[...]
        gain = nw_ref[...] * (1.0 + scale_ref[...])
        u = (x * lax.rsqrt(jnp.mean(x * x, axis=-1, keepdims=True) + EPS) * gain + shift_ref[...]).astype(BF16)
        u_ref[...] = u
        is_dt = lax.broadcasted_iota(jnp.int32, (LANES, D_MODEL), 0) < SSD_HEADS
        dt_ref[...] = _dot_nt(u, jnp.where(is_dt, wd_ref[...], 0.0).astype(BF16))

        base = jnp.broadcast_to((i * TM_IN).astype(F32) * inv_ref[...], (SUBLANES, RET_QK_DIM))
        cos_b = jnp.cos(base)[0:1, :]
        sin_b = jnp.sin(base)[0:1, :]
        cos_t[...] = cos_in[...] * cos_b - sin_in[...] * sin_b
        sin_pos = sin_in[...] * cos_b + cos_in[...] * sin_b
        first_half = lax.broadcasted_iota(jnp.int32, (TM_IN, RET_QK_DIM), 1) < half
        sin_t[...] = jnp.where(first_half, -sin_pos, sin_pos)

    def project():
        after_dt = j >= TILE_ROT0
        skip = pl.multiple_of(jnp.where(after_dt, SSD_HEADS, 0), SSD_HEADS)
        head = wa_ref[pl.ds(skip, TN_IN - SSD_HEADS), :]
        tail = jnp.where(after_dt, wb_ref[...], wa_ref[TN_IN - SSD_HEADS:, :])
        w = jnp.concatenate([head, tail], axis=0).astype(BF16)
        return _dot_nt(u_ref[...], w)

    @pl.when(jnp.logical_and(jnp.logical_or(je < TILE_CONV0, je >= TILE_G0), t < n_tiles))
    def _():
        acc = project()
        o_ref[...] = _silu(prev[...]).astype(BF16)
        prev[...] = acc

    @pl.when(t == n_tiles)
    def _():
        o_ref[...] = _silu(prev[...]).astype(BF16)

    @pl.when(jnp.logical_or(jnp.logical_and(je >= TILE_CONV0, je < TILE_ROT0),
                            jnp.logical_and(je >= TILE_V0, je < TILE_G0)))
    def _():
        wout_bf_ref[...] = wout_ref[...].astype(BF16)
        acc = project()
        o_ref[...] = prev[...].astype(BF16)
        prev[...] = acc

    @pl.when(jnp.logical_and(je >= TILE_ROT0, je < TILE_V0))
    def _():
        acc = project()
        side = je - TILE_ROT0
        for hd in range(RET_HEADS):
            cols = slice(hd * RET_QK_DIM, (hd + 1) * RET_QK_DIM)
            scale = decay[side, hd]
            for blk in range(TM_IN // CHUNK):
                rows = slice(blk * CHUNK, (blk + 1) * CHUNK)
                a = prev[rows, cols]
                rot = a * cos_t[rows, :] + pltpu.roll(a, half, 1) * sin_t[rows, :]
                o_ref[rows, cols] = (rot * scale).astype(BF16)
        prev[...] = acc


def _in_projection(h, norm_w, mod, w_in_t, inv, w_out):
    L = h.shape[0]
    n_tiles = (L // TM_IN) * N_COL_TILES
    n_conv, n_v = TILE_ROT0 - TILE_CONV0, TILE_G0 - TILE_V0
    n_plain_steps = (L // TM_IN) * (n_conv + n_v)
    n_cast = min(1 << (n_plain_steps.bit_length() - 1), MIX_WIDTH // BF16_SUBLANES)

    def cast_row(t):
        e = jnp.maximum(t - 1, 0)
        je = lax.rem(e, N_COL_TILES)
        done = ((e // N_COL_TILES) * (n_conv + n_v) + jnp.clip(je - TILE_CONV0, 0, n_conv)
                + jnp.clip(je - TILE_V0, 0, n_v))
        return (jnp.minimum(done, n_cast - 1), 0)

    cast_block = pl.BlockSpec((MIX_WIDTH // n_cast, D_MODEL), cast_row)
    const = lambda t: (0, 0)
    mul_row = lambda t: jnp.minimum(t, n_tiles - 1) // N_COL_TILES
    mul_col = lambda t: lax.rem(jnp.minimum(t, n_tiles - 1), N_COL_TILES)
    epi = lambda t: jnp.maximum(t - 1, 0)
    table = pltpu.VMEM((TM_IN, RET_QK_DIM), F32)
    return pl.pallas_call(
        functools.partial(_inproj_kernel, n_tiles=n_tiles),
        grid=(n_tiles + 1,),
        in_specs=[pl.BlockSpec((TM_IN, D_MODEL), lambda t: (mul_row(t), 0)),
                  pl.BlockSpec((1, D_MODEL), const),
                  pl.BlockSpec((1, D_MODEL), lambda t: (0, 0)),
                  pl.BlockSpec((1, D_MODEL), lambda t: (0, 1)),
                  pl.BlockSpec((TN_IN, D_MODEL), lambda t: (mul_col(t), 0)),
                  pl.BlockSpec((SSD_HEADS, D_MODEL), lambda t: ((mul_col(t) + 1) * (TN_IN // SSD_HEADS), 0)),
                  pl.BlockSpec((LANES, D_MODEL), lambda t: (OFF_XBC // LANES, 0)),
                  pl.BlockSpec((1, RET_QK_DIM), const),
                  cast_block],
        out_specs=[pl.BlockSpec((TM_IN, TN_IN), lambda t: (epi(t) // N_COL_TILES, lax.rem(epi(t), N_COL_TILES))),
                   pl.BlockSpec((TM_IN, LANES), lambda t: (mul_row(t), 0)),
                   cast_block],
        out_shape=[jax.ShapeDtypeStruct((L, PROJ_WIDTH), BF16),
                   jax.ShapeDtypeStruct((L, LANES), F32),
                   jax.ShapeDtypeStruct((MIX_WIDTH, D_MODEL), BF16)],
        scratch_shapes=[pltpu.VMEM((TM_IN, D_MODEL), BF16),
                        pltpu.VMEM((TM_IN, TN_IN), F32),
                        table, table, table, table,
                        pltpu.VMEM((2, RET_HEADS, CHUNK, RET_QK_DIM), F32)],
        compiler_params=pltpu.CompilerParams(dimension_semantics=("arbitrary",),
                                             vmem_limit_bytes=60 * MIB),
        name="adaln_in_projection",
    )(h, norm_w.reshape(1, D_MODEL), mod, mod, w_in_t, w_in_t, w_in_t, inv, w_out)


def _ssd_init(xbuf, shifts, state, expand):
    xbuf[0:CONV_TAIL, :] = jnp.zeros((CONV_TAIL, SSD_CONV_DIM), BF16)
    row = lax.broadcasted_iota(jnp.int32, shifts.shape, 0)
    col = lax.broadcasted_iota(jnp.int32, shifts.shape, 1)
    delay = jnp.right_shift(row, CHUNK.bit_length() - 1) + 1
    t = jnp.bitwise_and(row, CHUNK - 1)
    shifts[...] = jnp.where(col == CONV_TAIL + t - delay, 1.0, 0.0).astype(BF16)
    state[...] = jnp.zeros_like(state)
    head_of_lane = jnp.right_shift(lax.broadcasted_iota(jnp.int32, (LANES, SSD_D_INNER), 1), HEAD_DIM_LOG2)
    row = lax.broadcasted_iota(jnp.int32, (LANES, SSD_D_INNER), 0)
    expand[...] = jnp.where(head_of_lane == row, 1.0, 0.0).astype(BF16)


def _ssd_chunk(z_ref, x_ref, b_ref, c_ref, dtraw_ref, cw_ref, cb_ref, dtb_ref, alog_ref, dskip_ref, nw_ref,
               y_ref, xbuf, shifts, state, expand):
    int_iota = lambda shape, dim: lax.broadcasted_iota(jnp.int32, shape, dim)

    xbuf[CONV_TAIL:, 0:SSD_D_INNER] = x_ref[...]
    xbuf[CONV_TAIL:, SSD_D_INNER:SSD_D_INNER + SSD_BC] = b_ref[...]
    xbuf[CONV_TAIL:, SSD_D_INNER + SSD_BC:] = c_ref[...]
    delayed = _dot(shifts[...], xbuf[...])
    acc = cb_ref[...] + cw_ref[SSD_CONV - 1:SSD_CONV, :] * xbuf[CONV_TAIL:, :].astype(F32)
    for k in range(SSD_CONV - 1):
        tap = SSD_CONV - 2 - k
        acc = acc + cw_ref[tap:tap + 1, :] * delayed[k * CHUNK:(k + 1) * CHUNK, :]
    xbuf[0:CONV_TAIL, :] = xbuf[CHUNK:CHUNK + CONV_TAIL, :]
    xbc = _silu(acc)
    xs = xbc[:, 0:SSD_D_INNER]
    xs_b = xs.astype(BF16)
    bmat = xbc[:, SSD_D_INNER:SSD_D_INNER + SSD_BC]
    cmat = xbc[:, SSD_D_INNER + SSD_BC:]

    pre = dtraw_ref[...] + dtb_ref[...]
    dt = jnp.maximum(pre, 0.0) + jnp.log1p(jnp.exp(-jnp.abs(pre)))
    d_a = dt * (-LOG2_E * jnp.exp(alog_ref[...]))
    causal = int_iota((CHUNK, CHUNK), 0) >= int_iota((CHUNK, CHUNK), 1)
    tril = jnp.where(causal, 1.0, 0.0).astype(BF16)
    acs = sum(_dot(tril, part) for part in _split3(d_a))
    acs_t = acs.T
    src_t = acs_t - jnp.log2(dt).T
    w_t = jnp.exp2(acs_t[:, CHUNK - 1:CHUNK] - src_t)
    end = jnp.broadcast_to(jnp.exp2(acs[CHUNK - 1:CHUNK, :]), (SUBLANES, LANES))
    end_row = sum(_dot(part, expand[...]) for part in _split3(end))[0:1, :]

    lane_blk = jnp.right_shift(int_iota((CHUNK, QUAD_W), 1), HEAD_DIM_LOG2)

    def block_diag(v):
        return jnp.concatenate([jnp.where(lane_blk == j, v, jnp.zeros_like(v)) for j in range(QUAD)], axis=0)

    def group(g, between_quads=lambda q: None):
        gs = slice(g * SSD_STATE, (g + 1) * SSD_STATE)
        b_g = bmat[:, gs]
        c_g = cmat[:, gs]
        cb_g = _dot_nt(c_g.astype(BF16), b_g.astype(BF16))
        b_gt = b_g.T
        y_parts = []
        for qi in range(HEADS_PER_GROUP // QUAD):
            q = g * (HEADS_PER_GROUP // QUAD) + qi
            between_quads(q)
            qs = slice(q * QUAD_W, (q + 1) * QUAD_W)
            lhs_diag, lhs_state, lhs_off = [], [], []
            for j in range(QUAD):
                hd = q * QUAD + j
                col = jnp.broadcast_to(acs[:, hd:hd + 1], (CHUNK, CHUNK))
                decay_dt = jnp.exp2(jnp.where(causal, col - src_t[hd:hd + 1, :], -jnp.inf))
                lhs_diag.append((cb_g * decay_dt).astype(BF16))
                lhs_state.append((b_gt * w_t[hd:hd + 1, :]).astype(BF16))
                lhs_off.append((c_g * jnp.exp2(col)).astype(BF16))
            x_blk = block_diag(xs_b[:, qs])
            s_prev = state[:, qs]
            s_blk = block_diag(s_prev.astype(BF16))
            y_q = _dot(jnp.concatenate(lhs_diag, axis=1), x_blk)
            y_q = y_q + _dot(jnp.concatenate(lhs_off, axis=1), s_blk)
            state[:, qs] = s_prev * end_row[:, qs] + _dot(jnp.concatenate(lhs_state, axis=1), x_blk)
            y_parts.append(y_q + xs[:, qs] * dskip_ref[:, qs])
        ws = slice(g * GROUP_W, (g + 1) * GROUP_W)
        u = jnp.concatenate(y_parts, axis=1) * z_ref[:, ws].astype(F32)
        u = u * lax.rsqrt(jnp.mean(u * u, axis=-1, keepdims=True) + EPS)
        y_ref[:, ws] = (u * nw_ref[:, ws]).astype(BF16)

    return group


def _ret_chunk(q_ref, k_ref, v0_ref, v1_ref, g0_ref, g1_ref, nw_ref, y_ref, state):
    heads_per_block = RET_QK_WIDTH // RET_V_DIM
    log_gammas = _log_gammas()
    causal = (lax.broadcasted_iota(jnp.int32, (CHUNK, CHUNK), 0)
              >= lax.broadcasted_iota(jnp.int32, (CHUNK, CHUNK), 1))

    def head(hd):
        ks = slice(hd * RET_QK_DIM, (hd + 1) * RET_QK_DIM)
        vs = slice(hd * RET_V_DIM, (hd + 1) * RET_V_DIM)
        v_ref, g_ref = (v0_ref, g0_ref) if hd < heads_per_block else (v1_ref, g1_ref)
        bs = slice((hd % heads_per_block) * RET_V_DIM, (hd % heads_per_block + 1) * RET_V_DIM)
        q_h = q_ref[:, ks]
        k_h = k_ref[:, ks]
        v_h = v_ref[:, bs]
        scores = jnp.where(causal, _dot_nt(q_h, k_h), 0.0)
        s_prev = state[hd]
        o = _dot(scores.astype(BF16), v_h) + _dot(q_h, s_prev.astype(BF16))
        state[hd] = (s_prev + _dot_tn(k_h, v_h)) * float(np.exp(np.float32(CHUNK * log_gammas[hd])))
        mu = jnp.mean(o, axis=-1, keepdims=True)
        d = o - mu
        y = d * lax.rsqrt(jnp.mean(d * d, axis=-1, keepdims=True) + EPS) * nw_ref[:, vs]
        y = (y * g_ref[:, bs].astype(F32)).astype(BF16)
        y_ref[:, SSD_D_INNER + hd * RET_V_DIM:SSD_D_INNER + (hd + 1) * RET_V_DIM] = y

    return head


def _mixer_kernel(proj_ref, dtraw_ref, cw_ref, cb_ref, dtb_ref, alog_ref, dskip_ref, snw_ref, rnw_ref,
                  wout_hbm, h_ref, gate_ref, fw_ref, o_ref,
                  xbuf, shifts, ssd_state, expand, ret_state, ybuf, wout, wsem, *, n_steps, final):
    step = pl.program_id(0)

    @pl.when(step == 0)
    def _():
        wout_copy = pltpu.make_async_copy(wout_hbm, wout, wsem.at[0])
        wout_copy.start()
        _ssd_init(xbuf, shifts, ssd_state, expand)
        ret_state[...] = jnp.zeros_like(ret_state)
        ybuf[...] = jnp.zeros_like(ybuf)
        wout_copy.wait()

    slot = lax.rem(step, 2)
    piece_w = D_MODEL // N_PIECES
    cols = lambda width, blk: proj_ref.at[:, pl.ds(blk * width, width)]
    z_ref, x_ref = cols(SSD_D_INNER, COL_Z), cols(SSD_D_INNER, COL_X)
    b_ref, c_ref = cols(SSD_BC, COL_B), cols(SSD_BC, COL_C)
    q_ref, k_ref = cols(RET_QK_WIDTH, COL_Q), cols(RET_QK_WIDTH, COL_K)
    v0_ref, v1_ref = cols(RET_QK_WIDTH, COL_V), cols(RET_QK_WIDTH, COL_V + 1)
    g0_ref, g1_ref = cols(RET_QK_WIDTH, COL_G), cols(RET_QK_WIDTH, COL_G + 1)

    def project_piece(y_old, n):
        return _dot(y_old, wout[:, n * piece_w:(n + 1) * piece_w])

    def finish(pieces):
        h = h_ref[...] + gate_ref[...] * jnp.concatenate(pieces, axis=1)
        if final:
            h = h * lax.rsqrt(jnp.mean(h * h, axis=-1, keepdims=True) + EPS) * fw_ref[...]
        o_ref[...] = h

    @pl.when(step < n_steps)
    def _():
        y_old = ybuf[1 - slot]
        pieces = []
        for sub in range(CHUNKS_PER_STEP):
            rows = pl.ds(sub * CHUNK, CHUNK)
            sub_refs = lambda *refs: [r.at[rows] for r in refs]
            y_new = ybuf.at[slot, rows]
            z_c, x_c, b_c, c_c, dtraw_c = sub_refs(z_ref, x_ref, b_ref, c_ref, dtraw_ref)
            ssd_group = _ssd_chunk(z_c, x_c, b_c, c_c, dtraw_c, cw_ref, cb_ref, dtb_ref, alog_ref, dskip_ref,
                                   snw_ref, y_new, xbuf, shifts, ssd_state, expand)
            q_c, k_c, v0_c, v1_c, g0_c, g1_c = sub_refs(q_ref, k_ref, v0_ref, v1_ref, g0_ref, g1_ref)
            ret_head = _ret_chunk(q_c, k_c, v0_c, v1_c, g0_c, g1_c, rnw_ref, y_new, ret_state)

            def between_quads(q):
                if q % CHUNKS_PER_STEP == 0:
                    pieces.append(project_piece(y_old, len(pieces)))
                ret_head(q)

            for g in range(SSD_GROUPS):
                ssd_group(g, between_quads)
        assert len(pieces) == N_PIECES
        finish(pieces)

    @pl.when(step == n_steps)
    def _():
        y_old = ybuf[1 - slot]
        finish([project_piece(y_old, n) for n in range(N_PIECES)])


def _mix_and_project(proj, dt_raw, h, mod, conv_w, conv_b, dt_bias, a_log, d_skip, ssd_norm_w, ret_norm_w, w_out,
                     final_norm_w, final):
    L = h.shape[0]
    rows = CHUNKS_PER_STEP * CHUNK
    n_steps = L // rows
    pad = lambda v: jnp.pad(v.reshape(1, SSD_HEADS), ((0, 0), (0, LANES - SSD_HEADS)))
    full = lambda shape: pl.BlockSpec(shape, lambda s: (0, 0))
    mixed = lambda width, col: pl.BlockSpec((rows, width), lambda s: (jnp.minimum(s, n_steps - 1), col))
    projected = lambda width, col: pl.BlockSpec((rows, width), lambda s: (jnp.maximum(s - 1, 0), col))
    return pl.pallas_call(
        functools.partial(_mixer_kernel, n_steps=n_steps, final=final),
        grid=(n_steps + 1,),
        in_specs=[mixed(PROJ_WIDTH, 0), mixed(LANES, 0),
                  full((SSD_CONV, SSD_CONV_DIM)), full((1, SSD_CONV_DIM)), full((1, LANES)), full((1, LANES)),
                  full((1, SSD_D_INNER)), full((1, SSD_D_INNER)),
                  full((1, RET_V_WIDTH)),
                  pl.BlockSpec(memory_space=pl.ANY),
                  projected(D_MODEL, 0),
                  pl.BlockSpec((1, D_MODEL), lambda s: (0, 2)),
                  full((1, D_MODEL))],
        out_specs=projected(D_MODEL, 0),
        out_shape=jax.ShapeDtypeStruct((L, D_MODEL), F32),
        scratch_shapes=[pltpu.VMEM((CONV_TAIL + CHUNK, SSD_CONV_DIM), BF16),
                        pltpu.VMEM(((SSD_CONV - 1) * CHUNK, CONV_TAIL + CHUNK), BF16),
                        pltpu.VMEM((SSD_STATE, SSD_D_INNER), F32),
                        pltpu.VMEM((LANES, SSD_D_INNER), BF16),
                        pltpu.VMEM((RET_HEADS, RET_QK_DIM, RET_V_DIM), F32),
                        pltpu.VMEM((2, rows, MIX_WIDTH), BF16),
                        pltpu.VMEM((MIX_WIDTH, D_MODEL), BF16),
                        pltpu.SemaphoreType.DMA((1,))],
        compiler_params=pltpu.CompilerParams(dimension_semantics=("arbitrary",),
                                             vmem_limit_bytes=60 * MIB),
        name="mixers_out_projection",
    )(proj, dt_raw, conv_w, conv_b.reshape(1, SSD_CONV_DIM), pad(dt_bias), pad(a_log),
      jnp.repeat(d_skip, SSD_HEAD_DIM).reshape(1, SSD_D_INNER), ssd_norm_w.reshape(1, SSD_D_INNER),
      ret_norm_w.reshape(1, RET_V_WIDTH),
      w_out, h, mod, final_norm_w.reshape(1, D_MODEL))


def kernel(x, c, w_ada, b_ada, norm_w, w_in, conv_w, conv_b, dt_bias, a_log, d_skip, ssd_norm_w, ret_norm_w,
           w_out, final_norm_w):
    bsz, L, d_model = x.shape
    assert bsz == 1 and d_model == D_MODEL and L % TM_IN == 0 and w_in.shape[-1] == IN_WIDTH
    depth = w_in.shape[0]
    half = RET_QK_DIM // 2
    inv = ROPE_BASE ** (-jnp.arange(half, dtype=F32) / half)
    inv = jnp.concatenate([inv, inv]).reshape(1, RET_QK_DIM)
    h = x.reshape(L, D_MODEL)
    for layer in range(depth):
        mod = _modulation(c, w_ada[layer], b_ada[layer])
        proj, dt_raw, w_out_bf = _in_projection(h, norm_w[layer], mod, jnp.swapaxes(w_in[layer], 0, 1), inv,
                                                w_out[layer])
        h = _mix_and_project(proj, dt_raw, h, mod, conv_w[layer], conv_b[layer], dt_bias[layer], a_log[layer],
                             d_skip[layer], ssd_norm_w[layer], ret_norm_w[layer], w_out_bf, final_norm_w,
                             final=layer == depth - 1)
    return h.reshape(bsz, L, D_MODEL)
```
